```python
import math
import jax, jax.numpy as jnp
from jax import lax
import numpy as np

D_MODEL = 1024
BATCH = 8
SEQ = 2048
DEPTH = 4
DEC_BATCH = 128
DEC_SEQ = 8
PAST_LEN = 16384
PAGE_SIZE = 128

SSD_HEAD_DIM = 64
SSD_HEADS = D_MODEL // SSD_HEAD_DIM
SSD_INNER = SSD_HEADS * SSD_HEAD_DIM
SSD_GROUPS = 2
SSD_STATE = 128
SSD_CONV = 4
SSD_CHUNK = 128
SSD_CONV_DIM = SSD_INNER + 2 * SSD_GROUPS * SSD_STATE
DT_MIN = 0.001
DT_MAX = 0.1
SGU_WIDTH = D_MODEL // 2
SGU_GROUPS = 4
SGU_CHUNK = 128
SGU_GW = SGU_WIDTH // SGU_GROUPS
POOL_WIDTH = D_MODEL // 2
POOL_WINDOWS = (2, 4, 8, 16)
POOL_GROUPS = len(POOL_WINDOWS)
POOL_GW = POOL_WIDTH // POOL_GROUPS
POOL_BUF = max(POOL_WINDOWS) - 1
D_FF = -(-8 * D_MODEL // (3 * 256)) * 256
PLE_DIM = 256
N_BRANCH = 3
EPS = 1e-6
O_Z = 0
O_XBC = O_Z + SSD_INNER
O_DT = O_XBC + SSD_CONV_DIM
O_UV = O_DT + SSD_HEADS
O_POOL = O_UV + 2 * SGU_WIDTH
O_GATE = O_POOL + POOL_WIDTH
IN_DIM = O_GATE + N_BRANCH * D_MODEL

kernel_name = 'hybrid_ssd_sgu_pool_step'


def rmsnorm(x, g):
    xf = x.astype(jnp.float32)
    y = xf * lax.rsqrt(jnp.mean(xf * xf, axis=-1, keepdims=True) + EPS)
    return (y * g.astype(jnp.float32)).astype(x.dtype)


def segsum(a):
    cs = jnp.cumsum(a, axis=-1)
    diff = cs[..., :, None] - cs[..., None, :]
    T = a.shape[-1]
    mask = jnp.tril(jnp.ones((T, T), dtype=bool))
    return jnp.where(mask, diff, -jnp.inf)


def ssd_scan(xdt, da, b, c, h0):
    n, L, H, P = xdt.shape
    G, N = b.shape[2], b.shape[3]
    R = H // G
    Q = min(SSD_CHUNK, L)
    Lp = -(-L // Q) * Q
    pad = Lp - L
    if pad:
        pw = ((0, 0), (0, pad), (0, 0), (0, 0))
        xdt = jnp.pad(xdt, pw)
        b = jnp.pad(b, pw)
        c = jnp.pad(c, pw)
        da = jnp.pad(da, ((0, 0), (0, pad), (0, 0)))
    nc = Lp // Q
    X = xdt.reshape(n, nc, Q, G, R, P)
    A = da.reshape(n, nc, Q, G, R).transpose(0, 3, 4, 1, 2)
    Bc = b.reshape(n, nc, Q, G, N)
    Cc = c.reshape(n, nc, Q, G, N)
    A_cs = jnp.cumsum(A, axis=-1)
    Lmat = jnp.exp(segsum(A))
    CB = jnp.einsum('bclgn,bcsgn->bgcls', Cc, Bc)
    y_diag = jnp.einsum('bgrcls,bcsgrp->bclgrp', CB[:, :, None] * Lmat, X)
    decay_states = jnp.exp(A_cs[..., -1:] - A_cs)
    states = jnp.einsum('bclgn,bgrcl,bclgrp->bcgrpn', Bc, decay_states, X)
    states = jnp.concatenate([h0.reshape(n, 1, G, R, P, N), states], axis=1)
    chunk_tot = jnp.pad(A_cs[..., -1], ((0, 0), (0, 0), (0, 0), (1, 0)))
    decay_chunk = jnp.exp(segsum(chunk_tot))
    new_states = jnp.einsum('bgrzc,bcgrpn->bzgrpn', decay_chunk, states)
    prev_states, final = new_states[:, :-1], new_states[:, -1]
    y_off = jnp.einsum('bclgn,bcgrpn,bgrcl->bclgrp', Cc, prev_states, jnp.exp(A_cs))
    y = (y_diag + y_off).reshape(n, Lp, H, P)[:, :L]
    return y, final.reshape(n, H, P, N)


def ssd_branch(z, xbc, dt_raw, conv_buf, h0, conv_w, conv_b, dt_bias, a_log, d_skip, norm_g):
    n, L, _ = xbc.shape
    dtype = xbc.dtype
    f32 = jnp.float32
    xp = jnp.concatenate([conv_buf.astype(dtype), xbc], axis=1)
    new_conv = xp[:, -(SSD_CONV - 1):]
    acc = xp[:, 0:L] * conv_w[0]
    for k in range(1, SSD_CONV):
        acc = acc + xp[:, k:k + L] * conv_w[k]
    xbc_c = jax.nn.silu((acc + conv_b).astype(f32))
    xs = xbc_c[..., :SSD_INNER].reshape(n, L, SSD_HEADS, SSD_HEAD_DIM)
    bs = xbc_c[..., SSD_INNER:SSD_INNER + SSD_GROUPS * SSD_STATE].reshape(n, L, SSD_GROUPS, SSD_STATE)
    cs = xbc_c[..., SSD_INNER + SSD_GROUPS * SSD_STATE:].reshape(n, L, SSD_GROUPS, SSD_STATE)
    dt = jax.nn.softplus(dt_raw.astype(f32) + dt_bias.astype(f32))
    a = -jnp.exp(a_log.astype(f32))
    y, h_last = ssd_scan(xs * dt[..., None], dt * a, bs, cs, h0.astype(f32))
    y = y + d_skip.astype(f32)[:, None] * xs
    y = y.reshape(n, L, SSD_INNER) * jax.nn.silu(z.astype(f32))
    yg = y.reshape(n, L, SSD_GROUPS, SSD_INNER // SSD_GROUPS)
    yg = yg * lax.rsqrt(jnp.mean(yg * yg, axis=-1, keepdims=True) + EPS)
    y = yg.reshape(n, L, SSD_INNER) * norm_g.astype(f32)
    return y.astype(dtype), new_conv, h_last.astype(h0.dtype)


def sgu_branch(uv, ln_g, ln_b, w_sp, b_sp):
    n, L, _ = uv.shape
    dtype = uv.dtype
    f32 = jnp.float32
    a = jax.nn.gelu(uv.astype(f32))
    u, v = a[..., :SGU_WIDTH], a[..., SGU_WIDTH:]
    mu = jnp.mean(v, axis=-1, keepdims=True)
    var = jnp.mean(jnp.square(v - mu), axis=-1, keepdims=True)
    vn = (v - mu) * lax.rsqrt(var + EPS) * ln_g.astype(f32) + ln_b.astype(f32)
    Lp = -(-L // SGU_CHUNK) * SGU_CHUNK
    vp = jnp.pad(vn, ((0, 0), (0, Lp - L), (0, 0)))
    vc = vp.reshape(n, Lp // SGU_CHUNK, SGU_CHUNK, SGU_GROUPS, SGU_GW)
    wm = w_sp.astype(f32) * jnp.tril(jnp.ones((SGU_CHUNK, SGU_CHUNK), f32))
    s = jnp.einsum('gts,bcsgd->bctgd', wm, vc) + b_sp.astype(f32).T[None, None, :, :, None]
    s = s.reshape(n, Lp, SGU_WIDTH)[:, :L]
    return (u * s).astype(dtype), vn.astype(dtype)


def pool_branch(xc, buf, start, pool_w, pool_scale):
    n, L, C = xc.shape
    f32 = jnp.float32
    xf = xc.astype(f32)
    xp = jnp.concatenate([buf.astype(f32), xf], axis=1)
    new_buf = xp[:, -POOL_BUF:].astype(xc.dtype)
    cs = jnp.concatenate([jnp.zeros((n, 1, C), f32), jnp.cumsum(xp, axis=1)], axis=1)
    pos = start + jnp.arange(L)
    means = []
    for g, w in enumerate(POOL_WINDOWS):
        sl = slice(g * POOL_GW, (g + 1) * POOL_GW)
        hi = cs[:, POOL_BUF + 1:POOL_BUF + 1 + L, sl]
        lo = cs[:, POOL_BUF + 1 - w:POOL_BUF + 1 - w + L, sl]
        cnt = jnp.minimum(pos + 1, w).astype(f32)
        means.append((hi - lo) / cnt[None, :, None])
    pooled = jnp.concatenate(means, axis=-1)
    d = (pooled - xf).reshape(n, L, POOL_GROUPS, POOL_GW)
    y = jnp.einsum('blgc,gcd->blgd', d, pool_w.astype(f32)).reshape(n, L, C) * pool_scale.astype(f32)
    return y.astype(xc.dtype), new_buf


def trunk(x, p, conv_state, ssm_state, pool_state, start, W):
    (norm_mix, w_in, conv_w, conv_b, dt_bias, a_log, d_skip, ssd_norm, sgu_ln_g, sgu_ln_b,
     w_spatial, b_spatial, pool_w, pool_scale, w_br_a, w_br_b, w_br_c, w_out, norm_ffn,
     w_gate_up, w_down, norm_ple, w_ple_gate, w_ple_up, final_norm) = W
    n, L, _ = x.shape
    v_from = ((start + L - 1) // SGU_CHUNK) * SGU_CHUNK - start
    convs, ssms, pools, vrows = [], [], [], []
    for i in range(DEPTH):
        h = rmsnorm(x, norm_mix[i])
        proj = h @ w_in[i]
        ya, c_new, s_new = ssd_branch(proj[..., O_Z:O_XBC], proj[..., O_XBC:O_DT], proj[..., O_DT:O_UV],
                                      conv_state[i], ssm_state[i], conv_w[i], conv_b[i], dt_bias[i],
                                      a_log[i], d_skip[i], ssd_norm[i])
        yb, v_new = sgu_branch(proj[..., O_UV:O_POOL], sgu_ln_g[i], sgu_ln_b[i], w_spatial[i], b_spatial[i])
        yc, p_new = pool_branch(proj[..., O_POOL:O_GATE], pool_state[i], start, pool_w[i], pool_scale[i])
        g = jax.nn.sigmoid(proj[..., O_GATE:].astype(jnp.float32)).astype(x.dtype)
        merged = (g[..., :D_MODEL] * (ya @ w_br_a[i])
                  + g[..., D_MODEL:2 * D_MODEL] * (yb @ w_br_b[i])
                  + g[..., 2 * D_MODEL:] * (yc @ w_br_c[i]))
        x = x + merged @ w_out[i]
        h = rmsnorm(x, norm_ffn[i])
        gu = h @ w_gate_up[i]
        x = x + (jax.nn.silu(gu[..., :D_FF]) * gu[..., D_FF:]) @ w_down[i]
        h = rmsnorm(x, norm_ple[i])
        x = x + (p[i] @ w_ple_up[i]) * jax.nn.sigmoid(h @ w_ple_gate[i])
        convs.append(c_new)
        ssms.append(s_new)
        pools.append(p_new)
        vrows.append(v_new[:, v_from:])
    y = rmsnorm(x, final_norm)
    return y, jnp.stack(convs), jnp.stack(ssms), jnp.stack(pools), jnp.stack(vrows)


def setup_inputs(seed: int = 0) -> dict:
    key = jax.random.key(seed)
    keys = list(jax.random.split(key, 32))
    f32 = jnp.float32

    def nrm(k, shape, scale):
        return jax.random.normal(k, shape, f32) * scale

    x_prompt = nrm(keys[0], (BATCH, SEQ, D_MODEL), 1.0)
    x_sample = nrm(keys[1], (DEC_BATCH, DEC_SEQ, D_MODEL), 1.0)
    state_conv = nrm(keys[2], (DEPTH, DEC_BATCH, SSD_CONV - 1, SSD_CONV_DIM), 1.0)
    state_ssm = nrm(keys[3], (DEPTH, DEC_BATCH, SSD_HEADS, SSD_HEAD_DIM, SSD_STATE), 0.3)
    state_pool = nrm(keys[4], (DEPTH, DEC_BATCH, POOL_BUF, POOL_WIDTH), 1.0)
    p_prompt = nrm(keys[5], (DEPTH, BATCH, SEQ, PLE_DIM), 1.0)
    p_sample = nrm(keys[6], (DEPTH, DEC_BATCH, DEC_SEQ, PLE_DIM), 1.0)
    norm_mix = 1.0 + nrm(keys[7], (DEPTH, D_MODEL), 0.02)
    w_in = nrm(keys[8], (DEPTH, D_MODEL, IN_DIM), D_MODEL ** -0.5)
    conv_w = nrm(keys[9], (DEPTH, SSD_CONV, SSD_CONV_DIM), SSD_CONV ** -0.5)
    conv_b = nrm(keys[10], (DEPTH, SSD_CONV_DIM), 0.02)
    dt0 = jnp.exp(jax.random.uniform(keys[11], (DEPTH, SSD_HEADS), f32, math.log(DT_MIN), math.log(DT_MAX)))
    dt_bias = dt0 + jnp.log(-jnp.expm1(-dt0))
    a_log = jnp.log(jax.random.uniform(keys[12], (DEPTH, SSD_HEADS), f32, 1.0, 16.0))
    d_skip = 1.0 + nrm(keys[13], (DEPTH, SSD_HEADS), 0.02)
    ssd_norm = 1.0 + nrm(keys[14], (DEPTH, SSD_INNER), 0.02)
    sgu_ln_g = 1.0 + nrm(keys[15], (DEPTH, SGU_WIDTH), 0.02)
    sgu_ln_b = nrm(keys[16], (DEPTH, SGU_WIDTH), 0.02)
    w_spatial = nrm(keys[17], (DEPTH, SGU_GROUPS, SGU_CHUNK, SGU_CHUNK), SGU_CHUNK ** -0.5)
    b_spatial = 1.0 + nrm(keys[18], (DEPTH, SGU_GROUPS, SGU_CHUNK), 0.02)
    pool_w = nrm(keys[19], (DEPTH, POOL_GROUPS, POOL_GW, POOL_GW), POOL_GW ** -0.5)
    pool_scale = 1.0 + nrm(keys[20], (DEPTH, POOL_WIDTH), 0.02)
    w_br_a = nrm(keys[21], (DEPTH, SSD_INNER, D_MODEL), SSD_INNER ** -0.5)
    w_br_b = nrm(keys[22], (DEPTH, SGU_WIDTH, D_MODEL), SGU_WIDTH ** -0.5)
    w_br_c = nrm(keys[23], (DEPTH, POOL_WIDTH, D_MODEL), POOL_WIDTH ** -0.5)
    w_out = nrm(keys[24], (DEPTH, D_MODEL, D_MODEL), D_MODEL ** -0.5)
    norm_ffn = 1.0 + nrm(keys[25], (DEPTH, D_MODEL), 0.02)
    w_gate_up = nrm(keys[26], (DEPTH, D_MODEL, 2 * D_FF), D_MODEL ** -0.5)
    w_down = nrm(keys[27], (DEPTH, D_FF, D_MODEL), D_FF ** -0.5)
    norm_ple = 1.0 + nrm(keys[28], (DEPTH, D_MODEL), 0.02)
    w_ple_gate = nrm(keys[29], (DEPTH, D_MODEL, D_MODEL), D_MODEL ** -0.5)
    w_ple_up = nrm(keys[30], (DEPTH, PLE_DIM, D_MODEL), PLE_DIM ** -0.5)
    final_norm = 1.0 + nrm(keys[31], (D_MODEL,), 0.02)
    return {'x_prompt': x_prompt, 'x_sample': x_sample, 'state_conv': state_conv, 'state_ssm': state_ssm,
            'state_pool': state_pool, 'p_prompt': p_prompt, 'p_sample': p_sample, 'norm_mix': norm_mix,
            'w_in': w_in, 'conv_w': conv_w, 'conv_b': conv_b, 'dt_bias': dt_bias, 'a_log': a_log,
            'd_skip': d_skip, 'ssd_norm': ssd_norm, 'sgu_ln_g': sgu_ln_g, 'sgu_ln_b': sgu_ln_b,
            'w_spatial': w_spatial, 'b_spatial': b_spatial, 'pool_w': pool_w, 'pool_scale': pool_scale,
            'w_br_a': w_br_a, 'w_br_b': w_br_b, 'w_br_c': w_br_c, 'w_out': w_out, 'norm_ffn': norm_ffn,
            'w_gate_up': w_gate_up, 'w_down': w_down, 'norm_ple': norm_ple, 'w_ple_gate': w_ple_gate,
            'w_ple_up': w_ple_up, 'final_norm': final_norm}


def reference(x_prompt, x_sample, state_conv, state_ssm, state_pool, p_prompt, p_sample, norm_mix, w_in,
              conv_w, conv_b, dt_bias, a_log, d_skip, ssd_norm, sgu_ln_g, sgu_ln_b, w_spatial, b_spatial,
              pool_w, pool_scale, w_br_a, w_br_b, w_br_c, w_out, norm_ffn, w_gate_up, w_down, norm_ple,
              w_ple_gate, w_ple_up, final_norm):
    W = (norm_mix, w_in, conv_w, conv_b, dt_bias, a_log, d_skip, ssd_norm, sgu_ln_g, sgu_ln_b,
         w_spatial, b_spatial, pool_w, pool_scale, w_br_a, w_br_b, w_br_c, w_out, norm_ffn,
         w_gate_up, w_down, norm_ple, w_ple_gate, w_ple_up, final_norm)
    dt = x_prompt.dtype
    zero_conv = jnp.zeros((DEPTH, BATCH, SSD_CONV - 1, SSD_CONV_DIM), dt)
    zero_ssm = jnp.zeros((DEPTH, BATCH, SSD_HEADS, SSD_HEAD_DIM, SSD_STATE), dt)
    zero_pool = jnp.zeros((DEPTH, BATCH, POOL_BUF, POOL_WIDTH), dt)
    y_prompt, conv_p, ssm_p, pool_p, v_p = trunk(x_prompt, p_prompt, zero_conv, zero_ssm, zero_pool, 0, W)
    y_sample, conv_s, ssm_s, pool_s, v_s = trunk(x_sample, p_sample, state_conv, state_ssm, state_pool,
                                                 PAST_LEN, W)
    return (y_prompt, y_sample, conv_p, ssm_p, pool_p, v_p, conv_s, ssm_s, pool_s, v_s)
```

```python
import functools
import math

import jax
import jax.numpy as jnp
from jax import lax
from jax.experimental import pallas as pl
from jax.experimental.pallas import tpu as pltpu

F32 = jnp.float32
BF16 = jnp.bfloat16

D_MODEL = 1024
DEPTH = 4
PAST_LEN = 16384
SSD_HEAD_DIM = 64
SSD_HEADS = 16
SSD_INNER = 1024
SSD_GROUPS = 2
SSD_STATE = 128
SSD_CONV = 4
SSD_CHUNK = 128
SSD_CONV_DIM = 1536
SGU_WIDTH = 512
SGU_GROUPS = 4
SGU_CHUNK = 128
SGU_GW = 128
POOL_WIDTH = 512
POOL_WINDOWS = (2, 4, 8, 16)
POOL_GW = 128
POOL_BUF = 15
D_FF = 2816
PLE_DIM = 256
EPS = 1e-6
O_Z = 0
O_XBC = 1024
O_DT = 2560
O_UV = 2576
O_POOL = 3600
O_GATE = 4112

LANES = 128
SUBLANES = 8
HEADS_PER_GROUP = SSD_HEADS // SSD_GROUPS
GROUP_INNER = SSD_INNER // SSD_GROUPS
VMEM_LIMIT = 56 * 1024 * 1024

PROMPT_BLOCK = 256
SAMPLE_SEQS = 16
TOKEN_BLOCK = 512
FF_CHUNK = D_FF // 2


def _dot(a, b):
    return jnp.dot(a, b, preferred_element_type=F32)


def _dot_nt(a, b):
    return lax.dot_general(a, b, (((1,), (1,)), ((), ())), preferred_element_type=F32)


def _dot_tn(a, b):
    return lax.dot_general(a, b, (((0,), (0,)), ((), ())), preferred_element_type=F32)


def _sigmoid(x):
    return 0.5 * jnp.tanh(0.5 * x) + 0.5


def _silu(x):
    return x * _sigmoid(x)


def _gelu_tanh(x):
    c = math.sqrt(2.0 / math.pi)
    return 0.5 * x * (1.0 + jnp.tanh(c * (x + 0.044715 * (x * x * x))))


def _softplus(x):
    return jnp.maximum(x, 0.0) + jnp.log1p(jnp.exp(-jnp.abs(x)))


def _rmsnorm(x, g):
    ms = jnp.mean(x * x, axis=-1, keepdims=True)
    return x * lax.rsqrt(ms + EPS) * g


def _split2(v):
    hi = v.astype(BF16)
    lo = (v - hi.astype(F32)).astype(BF16)
    return jnp.concatenate([hi, lo], axis=-1)


def _expand_heads(v, e2):
    return _dot(_split2(v), e2)


def _group_rmsnorm(y, g):
    parts = []
    for k in range(SSD_GROUPS):
        yk = y[:, k * GROUP_INNER:(k + 1) * GROUP_INNER]
        ms = jnp.mean(yk * yk, axis=-1, keepdims=True)
        parts.append(yk * lax.rsqrt(ms + EPS))
    return jnp.concatenate(parts, axis=-1) * g


def _row_iota(shape):
    return lax.broadcasted_iota(jnp.int32, shape, 0)


def _lane_iota(shape):
    return lax.broadcasted_iota(jnp.int32, shape, 1)


def _ssd_chunk(xs, bm, cm, dtc, a_row, h_ref, e2, et2):
    q = SSD_CHUNK
    tri = _row_iota((q, q)) >= _lane_iota((q, q))
    tril_bf = jnp.where(tri, 1.0, 0.0).astype(BF16)
    da = dtc * a_row
    hi = da.astype(BF16)
    r1 = da - hi.astype(F32)
    mid = r1.astype(BF16)
    lo = (r1 - mid.astype(F32)).astype(BF16)
    cs3 = _dot(tril_bf, jnp.concatenate([hi, mid, lo], axis=1))
    a_cs = cs3[:, :LANES] + cs3[:, LANES:2 * LANES] + cs3[:, 2 * LANES:]
    a_t = a_cs.T
    tot_row = a_cs[q - 1:q, :]
    exp_a_e = _expand_heads(jnp.exp(a_cs), e2)
    decst_e = _expand_heads(jnp.exp(tot_row - a_cs), e2)
    dt_e = _expand_heads(dtc, e2)
    xdt = xs * dt_e
    xdt_bf = xdt.astype(BF16)
    xd_bf = (xdt * decst_e).astype(BF16)
    b_bf = bm.astype(BF16)
    c_bf = cm.astype(BF16)
    dec_b = jnp.broadcast_to(jnp.exp(a_t[:, q - 1:q]), (LANES, LANES))
    dec_hi = dec_b.astype(BF16)
    dec_lo = (dec_b - dec_hi.astype(F32)).astype(BF16)
    dec_col = _dot(et2, jnp.concatenate([dec_hi, dec_lo], axis=0))
    lane = _lane_iota((q, LANES))
    ys = []
    for g in range(SSD_GROUPS):
        cg = c_bf[:, g * SSD_STATE:(g + 1) * SSD_STATE]
        bg = b_bf[:, g * SSD_STATE:(g + 1) * SSD_STATE]
        cb = _dot_nt(cg, bg)
        hg = h_ref[g * GROUP_INNER:(g + 1) * GROUP_INNER, :]
        yoff = _dot_nt(cg, hg.astype(BF16))
        for pr in range(HEADS_PER_GROUP // 2):
            h0 = g * HEADS_PER_GROUP + 2 * pr
            ms = []
            for h in (h0, h0 + 1):
                diff = a_cs[:, h:h + 1] - a_t[h:h + 1, :]
                ms.append((jnp.where(tri, jnp.exp(diff), 0.0) * cb).astype(BF16))
            lhs = jnp.concatenate(ms, axis=1)
            xpair = xdt_bf[:, h0 * SSD_HEAD_DIM:(h0 + 2) * SSD_HEAD_DIM]
            zero = jnp.zeros_like(xpair)
            rhs = jnp.concatenate([jnp.where(lane < SSD_HEAD_DIM, xpair, zero),
                                   jnp.where(lane >= SSD_HEAD_DIM, xpair, zero)], axis=0)
            sl = slice(h0 * SSD_HEAD_DIM, (h0 + 2) * SSD_HEAD_DIM)
            ys.append(_dot(lhs, rhs) + yoff[:, pr * LANES:(pr + 1) * LANES] * exp_a_e[:, sl])
        ds = _dot_tn(xd_bf[:, g * GROUP_INNER:(g + 1) * GROUP_INNER], bg)
        h_ref[g * GROUP_INNER:(g + 1) * GROUP_INNER, :] = (
            dec_col[g * GROUP_INNER:(g + 1) * GROUP_INNER, :] * hg + ds)
    return jnp.concatenate(ys, axis=1)


def _branch_prompt_kernel(x_ref, nm_ref, wz_ref, wxbc_ref, wdt_ref, wuv_ref, wpool_ref,
                          convw_ref, convb_ref, dtb_ref, alog_ref, dskip_ref, ssdn_ref,
                          lng_ref, lnb_ref, wsp_ref, bsp_ref, poolw_ref, pools_ref, e2_ref, et2_ref,
                          ya_ref, yb_ref, yc_ref, convo_ref, ssmo_ref, poolo_ref, vo_ref,
                          hb_ref, h_ref, xp_ref, xcv_ref, dt_ref, y_ref, pp_ref, *, tb, nblk):
    j = pl.program_id(1)
    nchunk = tb // SSD_CHUNK

    @pl.when(j == 0)
    def _():
        h_ref[...] = jnp.zeros_like(h_ref)
        xp_ref[0:SUBLANES, :] = jnp.zeros((SUBLANES, SSD_CONV_DIM), F32)
        pp_ref[0:2 * SUBLANES, :] = jnp.zeros((2 * SUBLANES, POOL_WIDTH), F32)

    hb_ref[...] = _rmsnorm(x_ref[...], nm_ref[...]).astype(BF16)

    xp_ref[SUBLANES:SUBLANES + tb, :] = _dot(hb_ref[...], wxbc_ref[...])
    acc = xp_ref[pl.ds(SUBLANES - (SSD_CONV - 1), tb), :] * convw_ref[0:1, :]
    for k in range(1, SSD_CONV):
        acc = acc + xp_ref[pl.ds(SUBLANES - (SSD_CONV - 1) + k, tb), :] * convw_ref[k:k + 1, :]
    xcv_ref[...] = _silu(acc + convb_ref[...])
    dt_ref[...] = _softplus(_dot(hb_ref[...], wdt_ref[...]) + dtb_ref[...])
    a_row = -jnp.exp(alog_ref[...])
    for c in range(nchunk):
        rows = slice(c * SSD_CHUNK, (c + 1) * SSD_CHUNK)
        y_ref[rows, :] = _ssd_chunk(xcv_ref[rows, 0:SSD_INNER],
                                    xcv_ref[rows, SSD_INNER:SSD_INNER + SSD_GROUPS * SSD_STATE],
                                    xcv_ref[rows, SSD_INNER + SSD_GROUPS * SSD_STATE:SSD_CONV_DIM],
                                    dt_ref[rows, :], a_row, h_ref, e2_ref[...], et2_ref[...])
    z = _dot(hb_ref[...], wz_ref[...])
    y = (y_ref[...] + dskip_ref[...] * xcv_ref[:, 0:SSD_INNER]) * _silu(z)
    ya_ref[...] = _group_rmsnorm(y, ssdn_ref[...]).astype(BF16)
    tail = xp_ref[tb:tb + SUBLANES, :]
    xp_ref[0:SUBLANES, :] = tail

    a = _gelu_tanh(_dot(hb_ref[...], wuv_ref[...]))
    u = a[:, :SGU_WIDTH]
    v = a[:, SGU_WIDTH:]
    mu = jnp.mean(v, axis=-1, keepdims=True)
    vc = v - mu
    var = jnp.mean(vc * vc, axis=-1, keepdims=True)
    vn = vc * lax.rsqrt(var + EPS) * lng_ref[...] + lnb_ref[...]
    vn_bf = vn.astype(BF16)
    tri = _row_iota((SGU_CHUNK, SGU_CHUNK)) >= _lane_iota((SGU_CHUNK, SGU_CHUNK))
    s_rows = []
    for c in range(nchunk):
        s_cols = []
        for g in range(SGU_GROUPS):
            wm = jnp.where(tri, wsp_ref[g], 0.0).astype(BF16)
            s_cols.append(_dot(wm, vn_bf[c * SGU_CHUNK:(c + 1) * SGU_CHUNK, g * SGU_GW:(g + 1) * SGU_GW]))
        s_rows.append(jnp.concatenate(s_cols, axis=1) + bsp_ref[...])
    yb_ref[...] = (u * jnp.concatenate(s_rows, axis=0)).astype(BF16)

    pp_ref[2 * SUBLANES:2 * SUBLANES + tb, :] = _dot(hb_ref[...], wpool_ref[...])
    pos = j * tb + _row_iota((tb, POOL_GW))
    yc_cols = []
    for g, w in enumerate(POOL_WINDOWS):
        cols = slice(g * POOL_GW, (g + 1) * POOL_GW)
        cur = pp_ref[pl.ds(2 * SUBLANES, tb), cols]
        wsum = cur
        for i in range(1, w):
            wsum = wsum + pp_ref[pl.ds(2 * SUBLANES - i, tb), cols]
        cnt = jnp.minimum(pos + 1, w).astype(F32)
        d = wsum / cnt - cur
        yc_cols.append(_dot(d.astype(BF16), poolw_ref[g]))
    yc_ref[...] = (jnp.concatenate(yc_cols, axis=1) * pools_ref[...]).astype(BF16)
    ptail = pp_ref[tb:tb + 2 * SUBLANES, :]
    pp_ref[0:2 * SUBLANES, :] = ptail

    @pl.when(j == nblk - 1)
    def _():
        convo_ref[...] = tail
        ssmo_ref[...] = h_ref[...]
        poolo_ref[...] = ptail
        vo_ref[...] = vn[tb - SGU_CHUNK:tb, :]


def _const_spec(shape):
    nd = len(shape)
    return pl.BlockSpec(shape, lambda *_: (0,) * nd, pipeline_mode=pl.Buffered(1))


def _branch_prompt(x, lw):
    b, l, _ = x.shape
    tb = PROMPT_BLOCK
    nblk = l // tb
    assert l % tb == 0 and tb % SSD_CHUNK == 0 and tb >= 2 * SUBLANES
    consts = [lw['norm_mix'], lw['w_z'], lw['w_xbc'], lw['w_dt'], lw['w_uv'], lw['w_pool'],
              lw['conv_w'], lw['conv_b'], lw['dt_bias'], lw['a_log'], lw['d_skip'], lw['ssd_norm'],
              lw['ln_g'], lw['ln_b'], lw['w_sp'], lw['b_sp'], lw['pool_w'], lw['pool_scale'],
              lw['e2'], lw['et2']]
    tok = lambda width: pl.BlockSpec((None, tb, width), lambda i, j: (i, j, 0))
    per_seq = lambda rows, width: pl.BlockSpec((None, rows, width), lambda i, j: (i, 0, 0))
    out_shape = (jax.ShapeDtypeStruct((b, l, SSD_INNER), BF16),
                 jax.ShapeDtypeStruct((b, l, SGU_WIDTH), BF16),
                 jax.ShapeDtypeStruct((b, l, POOL_WIDTH), BF16),
                 jax.ShapeDtypeStruct((b, SUBLANES, SSD_CONV_DIM), F32),
                 jax.ShapeDtypeStruct((b, SSD_INNER, SSD_STATE), F32),
                 jax.ShapeDtypeStruct((b, 2 * SUBLANES, POOL_WIDTH), F32),
                 jax.ShapeDtypeStruct((b, SGU_CHUNK, SGU_WIDTH), F32))
    out_specs = (tok(SSD_INNER), tok(SGU_WIDTH), tok(POOL_WIDTH),
                 per_seq(SUBLANES, SSD_CONV_DIM), per_seq(SSD_INNER, SSD_STATE),
                 per_seq(2 * SUBLANES, POOL_WIDTH), per_seq(SGU_CHUNK, SGU_WIDTH))
    scratch = [pltpu.VMEM((tb, D_MODEL), BF16),
               pltpu.VMEM((SSD_INNER, SSD_STATE), F32),
               pltpu.VMEM((tb + SUBLANES, SSD_CONV_DIM), F32),
               pltpu.VMEM((tb, SSD_CONV_DIM), F32),
               pltpu.VMEM((tb, LANES), F32),
               pltpu.VMEM((tb, SSD_INNER), F32),
               pltpu.VMEM((tb + 2 * SUBLANES, POOL_WIDTH), F32)]
    ya, yb, yc, conv8, ssm, pool16, vrows = pl.pallas_call(
        functools.partial(_branch_prompt_kernel, tb=tb, nblk=nblk),
        grid=(b, nblk),
        in_specs=[tok(D_MODEL)] + [_const_spec(c.shape) for c in consts],
        out_specs=out_specs, out_shape=out_shape, scratch_shapes=scratch,
        compiler_params=pltpu.CompilerParams(dimension_semantics=("arbitrary", "arbitrary"),
                                             vmem_limit_bytes=VMEM_LIMIT),
        name="branch_prompt",
    )(x, *consts)
    conv_new = conv8[:, SUBLANES - (SSD_CONV - 1):, :]
    ssm_new = ssm.reshape(b, SSD_HEADS, SSD_HEAD_DIM, SSD_STATE)
    pool_new = pool16[:, 2 * SUBLANES - POOL_BUF:, :]
    return ya, yb, yc, conv_new, ssm_new, pool_new, vrows


def _branch_sample_kernel(x_ref, convs_ref, ssms_ref, pools_in_ref, nm_ref, wz_ref, wxbc_ref, wdt_ref, wuv_ref,
                          wpool_ref, convw_ref, convb_ref, dtb_ref, alog_ref, dskip_ref, ssdn_ref,
                          lng_ref, lnb_ref, wd_ref, b8_ref, poolw_ref, pools_ref, e2_ref,
                          ya_ref, yb_ref, yc_ref, convo_ref, ssmo_ref, poolo_ref, vo_ref,
                          c_ref, bm_ref, xd_ref, daug_ref, yoff_ref, *, nb, seq, start):
    r = nb * seq
    assert seq == SUBLANES
    t128 = _row_iota((r, LANES)) % seq

    def shift_rows(cur, prev, j, width):
        tt = _row_iota((r, width)) % seq
        a = pltpu.roll(cur, j, 0)
        if prev is None:
            return jnp.where(tt >= j, a, 0.0)
        return jnp.where(tt >= j, a, pltpu.roll(prev, r - seq + j, 0))

    hb = _rmsnorm(x_ref[...].reshape(r, D_MODEL), nm_ref[...]).astype(BF16)

    xbc = _dot(hb, wxbc_ref[...])
    cbuf = convs_ref[...].reshape(r, SSD_CONV_DIM)
    acc = xbc * convw_ref[SSD_CONV - 1:SSD_CONV, :]
    for k in range(SSD_CONV - 1):
        acc = acc + shift_rows(xbc, cbuf, SSD_CONV - 1 - k, SSD_CONV_DIM) * convw_ref[k:k + 1, :]
    xcv = _silu(acc + convb_ref[...])
    convo_ref[...] = xbc.reshape(nb, seq, SSD_CONV_DIM)
    xs = xcv[:, 0:SSD_INNER]
    bm = xcv[:, SSD_INNER:SSD_INNER + SSD_GROUPS * SSD_STATE]
    cm = xcv[:, SSD_INNER + SSD_GROUPS * SSD_STATE:SSD_CONV_DIM]
    dt = _softplus(_dot(hb, wdt_ref[...]) + dtb_ref[...])
    a_row = -jnp.exp(alog_ref[...])
    a_cs = dt * a_row
    for s in (1, 2, 4):
        a_cs = a_cs + jnp.where(t128 >= s, pltpu.roll(a_cs, s, 0), 0.0)
    a3 = a_cs.reshape(nb, seq, LANES)
    tot = jnp.broadcast_to(a3[:, seq - 1:seq, :], (nb, seq, LANES)).reshape(r, LANES)
    e2 = e2_ref[...]
    exp_a_e = _expand_heads(jnp.exp(a_cs), e2)
    decst_e = _expand_heads(jnp.exp(tot - a_cs), e2)
    dt_e = _expand_heads(dt, e2)
    dectot_e = _expand_heads(jnp.exp(tot), e2)
    xdt = xs * dt_e
    lane = _lane_iota((r, LANES))
    y = jnp.zeros((r, SSD_INNER), F32)
    for j in range(seq):
        if j == 0:
            lj = jnp.ones((r, LANES), F32)
            b_sh, x_sh = bm, xdt
        else:
            lj = jnp.where(t128 >= j, jnp.exp(a_cs - pltpu.roll(a_cs, j, 0)), 0.0)
            b_sh, x_sh = pltpu.roll(bm, j, 0), pltpu.roll(xdt, j, 0)
        prod = cm * b_sh
        cb0 = jnp.sum(prod[:, 0:SSD_STATE], axis=-1, keepdims=True)
        cb1 = jnp.sum(prod[:, SSD_STATE:2 * SSD_STATE], axis=-1, keepdims=True)
        mj = lj * jnp.where(lane < HEADS_PER_GROUP, cb0, cb1)
        y = y + _expand_heads(mj, e2) * x_sh
    c_ref[...] = cm
    bm_ref[...] = bm
    xd_ref[...] = xdt * decst_e
    dec_hi = dectot_e.astype(BF16).astype(F32)
    t1024 = _row_iota((r, SSD_INNER)) % seq
    daug_ref[...] = jnp.where(t1024 == 0, dec_hi, jnp.where(t1024 == 1, dectot_e - dec_hi, 0.0))
    t8 = _row_iota((seq, LANES))
    ones2 = jnp.where(t8 < 2, 1.0, 0.0)
    zeros8 = jnp.zeros((seq, LANES), F32)

    def per_seq(n, carry):
        rows = pl.ds(pl.multiple_of(n * seq, seq), seq)
        for g in range(SSD_GROUPS):
            cols = slice(g * GROUP_INNER, (g + 1) * GROUP_INNER)
            scol = slice(g * SSD_STATE, (g + 1) * SSD_STATE)
            h0 = ssms_ref[n, cols, :]
            cg = c_ref[rows, scol].astype(BF16)
            yoff_ref[rows, cols] = _dot_nt(cg, h0.astype(BF16))
            lhs = jnp.concatenate([xd_ref[rows, cols], daug_ref[rows, cols]], axis=0).astype(BF16)
            rhs = jnp.concatenate([jnp.concatenate([bm_ref[rows, scol], zeros8], axis=1),
                                   jnp.concatenate([zeros8, ones2], axis=1)], axis=0).astype(BF16)
            upd = _dot_tn(lhs, rhs)
            ssmo_ref[n, cols, :] = upd[:, SSD_STATE:] * h0 + upd[:, :SSD_STATE]
        return carry

    lax.fori_loop(0, nb, per_seq, 0)
    z = _dot(hb, wz_ref[...])
    y = (y + yoff_ref[...] * exp_a_e + dskip_ref[...] * xs) * _silu(z)
    ya_ref[...] = _group_rmsnorm(y, ssdn_ref[...]).astype(BF16)

    a = _gelu_tanh(_dot(hb, wuv_ref[...]))
    u = a[:, :SGU_WIDTH]
    v = a[:, SGU_WIDTH:]
    mu = jnp.mean(v, axis=-1, keepdims=True)
    vc = v - mu
    var = jnp.mean(vc * vc, axis=-1, keepdims=True)
    vn = vc * lax.rsqrt(var + EPS) * lng_ref[...] + lnb_ref[...]
    vo_ref[...] = vn
    s = vn.reshape(nb, seq, SGU_WIDTH) * wd_ref[0] + b8_ref[...]
    for j in range(1, seq):
        s = s + pltpu.roll(vn, j, 0).reshape(nb, seq, SGU_WIDTH) * wd_ref[j]
    yb_ref[...] = (u * s.reshape(r, SGU_WIDTH)).astype(BF16)

    xc = _dot(hb, wpool_ref[...])
    pbuf = pools_in_ref[...]
    t0 = pbuf[:, 0:seq, :].reshape(r, POOL_WIDTH)
    t1 = pbuf[:, seq:2 * seq, :].reshape(r, POOL_WIDTH)
    poolo_ref[:, 0:seq, :] = pbuf[:, seq:2 * seq, :]
    poolo_ref[:, seq:2 * seq, :] = xc.reshape(nb, seq, POOL_WIDTH)
    pos = start + (_row_iota((r, POOL_GW)) % seq)
    yc_cols = []
    for g, w in enumerate(POOL_WINDOWS):
        cols = slice(g * POOL_GW, (g + 1) * POOL_GW)
        tiles = [t0[:, cols], t1[:, cols], xc[:, cols]]
        span = 1
        while span < min(w, seq):
            prev = [None] + tiles[:-1]
            tiles = [tl + shift_rows(tl, pv, span, POOL_GW) for tl, pv in zip(tiles, prev)]
            span *= 2
        wsum = tiles[2] if w <= seq else tiles[2] + tiles[1]
        cnt = jnp.minimum(pos + 1, w).astype(F32)
        d = wsum / cnt - xc[:, cols]
        yc_cols.append(_dot(d.astype(BF16), poolw_ref[g]))
    yc_ref[...] = (jnp.concatenate(yc_cols, axis=1) * pools_ref[...]).astype(BF16)


def _branch_sample(x, conv_state, ssm_state, pool_state, lw, start):
    n, seq, _ = x.shape
    nb = SAMPLE_SEQS
    assert n % nb == 0 and seq == SUBLANES and start >= max(POOL_WINDOWS)
    r = nb * seq
    conv_pad = jnp.pad(conv_state, ((0, 0), (SUBLANES - (SSD_CONV - 1), 0), (0, 0)))
    pool_pad = jnp.pad(pool_state, ((0, 0), (2 * SUBLANES - POOL_BUF, 0), (0, 0)))
    ssm_flat = ssm_state.reshape(n, SSD_INNER, SSD_STATE)
    consts = [lw['norm_mix'], lw['w_z'], lw['w_xbc'], lw['w_dt'], lw['w_uv'], lw['w_pool'],
              lw['conv_w'], lw['conv_b'], lw['dt_bias'], lw['a_log'], lw['d_skip'], lw['ssd_norm'],
              lw['ln_g'], lw['ln_b'], lw['w_diag8'], lw['b8'], lw['pool_w'], lw['pool_scale'], lw['e2']]
    seq3 = lambda rows, width: pl.BlockSpec((nb, rows, width), lambda i: (i, 0, 0))
    tok = lambda width: pl.BlockSpec((r, width), lambda i: (i, 0))
    out_shape = (jax.ShapeDtypeStruct((n * seq, SSD_INNER), BF16),
                 jax.ShapeDtypeStruct((n * seq, SGU_WIDTH), BF16),
                 jax.ShapeDtypeStruct((n * seq, POOL_WIDTH), BF16),
                 jax.ShapeDtypeStruct((n, seq, SSD_CONV_DIM), F32),
                 jax.ShapeDtypeStruct((n, SSD_INNER, SSD_STATE), F32),
                 jax.ShapeDtypeStruct((n, 2 * seq, POOL_WIDTH), F32),
                 jax.ShapeDtypeStruct((n * seq, SGU_WIDTH), F32))
    out_specs = (tok(SSD_INNER), tok(SGU_WIDTH), tok(POOL_WIDTH), seq3(seq, SSD_CONV_DIM),
                 seq3(SSD_INNER, SSD_STATE), seq3(2 * seq, POOL_WIDTH), tok(SGU_WIDTH))
    scratch = [pltpu.VMEM((r, SSD_GROUPS * SSD_STATE), F32),
               pltpu.VMEM((r, SSD_GROUPS * SSD_STATE), F32),
               pltpu.VMEM((r, SSD_INNER), F32),
               pltpu.VMEM((r, SSD_INNER), F32),
               pltpu.VMEM((r, SSD_INNER), F32)]
    ya, yb, yc, xbc, ssm, pool16, vrows = pl.pallas_call(
        functools.partial(_branch_sample_kernel, nb=nb, seq=seq, start=start),
        grid=(n // nb,),
        in_specs=[seq3(seq, D_MODEL), seq3(seq, SSD_CONV_DIM), seq3(SSD_INNER, SSD_STATE),
                  seq3(2 * seq, POOL_WIDTH)] + [_const_spec(c.shape) for c in consts],
        out_specs=out_specs, out_shape=out_shape, scratch_shapes=scratch,
        compiler_params=pltpu.CompilerParams(dimension_semantics=("arbitrary",),
                                             vmem_limit_bytes=VMEM_LIMIT),
        name="branch_sample",
    )(x, conv_pad, ssm_flat, pool_pad, *consts)
    conv_new = xbc[:, seq - (SSD_CONV - 1):, :]
    ssm_new = ssm.reshape(n, SSD_HEADS, SSD_HEAD_DIM, SSD_STATE)
    pool_new = pool16[:, 2 * seq - POOL_BUF:, :]
    return ya, yb, yc, conv_new, ssm_new, pool_new, vrows.reshape(n, seq, SGU_WIDTH)


def _merge_kernel(x_ref, ya_ref, yb_ref, yc_ref, nm_ref, wg_ref, wa_ref, wb_ref, wc_ref, wo_ref, o_ref):
    x = x_ref[...]
    hb = _rmsnorm(x, nm_ref[...]).astype(BF16)
    m = _sigmoid(_dot(hb, wg_ref[:, 0:D_MODEL])) * _dot(ya_ref[...], wa_ref[...])
    m = m + _sigmoid(_dot(hb, wg_ref[:, D_MODEL:2 * D_MODEL])) * _dot(yb_ref[...], wb_ref[...])
    m = m + _sigmoid(_dot(hb, wg_ref[:, 2 * D_MODEL:3 * D_MODEL])) * _dot(yc_ref[...], wc_ref[...])
    o_ref[...] = x + _dot(m.astype(BF16), wo_ref[...])


def _merge(x, ya, yb, yc, lw):
    t = x.shape[0]
    tm = min(TOKEN_BLOCK, t)
    assert t % tm == 0
    consts = [lw['norm_mix'], lw['w_gate'], lw['w_br_a'], lw['w_br_b'], lw['w_br_c'], lw['w_out']]
    tok = lambda width: pl.BlockSpec((tm, width), lambda i: (i, 0))
    return pl.pallas_call(
        _merge_kernel, grid=(t // tm,),
        in_specs=[tok(D_MODEL), tok(SSD_INNER), tok(SGU_WIDTH), tok(POOL_WIDTH)] + [_const_spec(c.shape) for c in consts],
        out_specs=tok(D_MODEL), out_shape=jax.ShapeDtypeStruct((t, D_MODEL), F32),
        compiler_params=pltpu.CompilerParams(dimension_semantics=("arbitrary",), vmem_limit_bytes=VMEM_LIMIT),
        name="merge",
    )(x, ya, yb, yc, *consts)


def _ffn_kernel(x_ref, p_ref, nf_ref, wg_ref, wu_ref, wd_ref, np_ref, wpg_ref, wpu_ref, fn_ref, o_ref, *, final):
    x = x_ref[...]
    hb = _rmsnorm(x, nf_ref[...]).astype(BF16)
    acc = x
    for c in range(D_FF // FF_CHUNK):
        cols = slice(c * FF_CHUNK, (c + 1) * FF_CHUNK)
        act = (_silu(_dot(hb, wg_ref[:, cols])) * _dot(hb, wu_ref[:, cols])).astype(BF16)
        acc = acc + _dot(act, wd_ref[cols, :])
    hb = _rmsnorm(acc, np_ref[...]).astype(BF16)
    out = acc + _dot(p_ref[...].astype(BF16), wpu_ref[...]) * _sigmoid(_dot(hb, wpg_ref[...]))
    if final:
        out = _rmsnorm(out, fn_ref[...])
    o_ref[...] = out


def _ffn(x, p, lw, final_norm, final):
    t = x.shape[0]
    tm = min(TOKEN_BLOCK, t)
    assert t % tm == 0
    consts = [lw['norm_ffn'], lw['w_ffn_gate'], lw['w_ffn_up'], lw['w_down'], lw['norm_ple'],
              lw['w_ple_gate'], lw['w_ple_up'], final_norm]
    tok = lambda width: pl.BlockSpec((tm, width), lambda i: (i, 0))
    return pl.pallas_call(
        functools.partial(_ffn_kernel, final=final), grid=(t // tm,),
        in_specs=[tok(D_MODEL), tok(PLE_DIM)] + [_const_spec(c.shape) for c in consts],
        out_specs=tok(D_MODEL), out_shape=jax.ShapeDtypeStruct((t, D_MODEL), F32),
        compiler_params=pltpu.CompilerParams(dimension_semantics=("arbitrary",), vmem_limit_bytes=VMEM_LIMIT),
        name="ffn",
    )(x, p, *consts)


def _prepare_layer_weights(i, norm_mix, w_in, conv_w, conv_b, dt_bias, a_log, d_skip, ssd_norm, sgu_ln_g, sgu_ln_b,
                           w_spatial, b_spatial, pool_w, pool_scale, w_br_a, w_br_b, w_br_c, w_out, norm_ffn,
                           w_gate_up, w_down, norm_ple, w_ple_gate, w_ple_up, e2, et2):
    row = lambda v: v.reshape(1, -1).astype(F32)
    pad_heads = lambda v: jnp.pad(v.reshape(1, -1).astype(F32), ((0, 0), (0, LANES - SSD_HEADS)))
    wi = w_in[i]
    seq = SUBLANES
    w8 = w_spatial[i][:, :seq, :seq]
    tt = jnp.arange(seq)
    lag = tt[None, :] - tt[:, None]
    w_lag = jnp.where(lag[None] >= 0, w8[:, tt[None, :], jnp.clip(lag, 0, seq - 1)], 0.0)
    w_diag8 = jnp.repeat(jnp.transpose(w_lag, (1, 2, 0)), SGU_GW, axis=2).astype(F32)
    b_full = jnp.repeat(b_spatial[i].T, SGU_GW, axis=1).astype(F32)
    return {
        'norm_mix': row(norm_mix[i]),
        'w_z': wi[:, O_Z:O_XBC].astype(BF16),
        'w_xbc': wi[:, O_XBC:O_DT].astype(BF16),
        'w_dt': jnp.pad(wi[:, O_DT:O_UV], ((0, 0), (0, LANES - SSD_HEADS))).astype(BF16),
        'w_uv': wi[:, O_UV:O_POOL].astype(BF16),
        'w_pool': wi[:, O_POOL:O_GATE].astype(BF16),
        'w_gate': wi[:, O_GATE:].astype(BF16),
        'conv_w': conv_w[i].astype(F32), 'conv_b': row(conv_b[i]),
        'dt_bias': pad_heads(dt_bias[i]), 'a_log': pad_heads(a_log[i]),
        'd_skip': row(jnp.repeat(d_skip[i], SSD_HEAD_DIM)),
        'ssd_norm': row(ssd_norm[i]),
        'ln_g': row(sgu_ln_g[i]), 'ln_b': row(sgu_ln_b[i]),
        'w_sp': w_spatial[i].astype(F32), 'b_sp': b_full,
        'w_diag8': w_diag8, 'b8': b_full[:seq],
        'pool_w': pool_w[i].astype(BF16), 'pool_scale': row(pool_scale[i]),
        'w_br_a': w_br_a[i].astype(BF16), 'w_br_b': w_br_b[i].astype(BF16), 'w_br_c': w_br_c[i].astype(BF16),
        'w_out': w_out[i].astype(BF16),
        'norm_ffn': row(norm_ffn[i]),
        'w_ffn_gate': w_gate_up[i][:, :D_FF].astype(BF16), 'w_ffn_up': w_gate_up[i][:, D_FF:].astype(BF16),
        'w_down': w_down[i].astype(BF16),
        'norm_ple': row(norm_ple[i]),
        'w_ple_gate': w_ple_gate[i].astype(BF16), 'w_ple_up': w_ple_up[i].astype(BF16),
        'e2': e2, 'et2': et2,
    }


def _head_expansion():
    rows = jnp.arange(2 * LANES)[:, None] % LANES
    cols = jnp.arange(SSD_INNER)[None, :] // SSD_HEAD_DIM
    e2 = (rows == cols).astype(BF16)
    return e2, e2.T


def kernel(x_prompt, x_sample, state_conv, state_ssm, state_pool, p_prompt, p_sample, norm_mix, w_in, conv_w, conv_b, dt_bias, a_log, d_skip, ssd_norm, sgu_ln_g, sgu_ln_b, w_spatial, b_spatial, pool_w, pool_scale, w_br_a, w_br_b, w_br_c, w_out, norm_ffn, w_gate_up, w_down, norm_ple, w_ple_gate, w_ple_up, final_norm):
    b, l, _ = x_prompt.shape
    n, s, _ = x_sample.shape
    depth = w_in.shape[0]
    e2, et2 = _head_expansion()
    fin = final_norm.reshape(1, -1).astype(F32)
    xp = x_prompt
    xs = x_sample
    outs_p = [[], [], [], []]
    outs_s = [[], [], [], []]
    for i in range(depth):
        lw = _prepare_layer_weights(i, norm_mix, w_in, conv_w, conv_b, dt_bias, a_log, d_skip, ssd_norm, sgu_ln_g,
                                    sgu_ln_b, w_spatial, b_spatial, pool_w, pool_scale, w_br_a, w_br_b, w_br_c,
                                    w_out, norm_ffn, w_gate_up, w_down, norm_ple, w_ple_gate, w_ple_up, e2, et2)
        final = i == depth - 1
        ya, yb, yc, c_new, s_new, p_new, v_new = _branch_prompt(xp, lw)
        for acc, val in zip(outs_p, (c_new, s_new, p_new, v_new)):
            acc.append(val)
        x1 = _merge(xp.reshape(b * l, D_MODEL), ya.reshape(b * l, -1), yb.reshape(b * l, -1), yc.reshape(b * l, -1), lw)
        xp = _ffn(x1, p_prompt[i].reshape(b * l, PLE_DIM), lw, fin, final).reshape(b, l, D_MODEL)
        ya, yb, yc, c_new, s_new, p_new, v_new = _branch_sample(xs, state_conv[i], state_ssm[i], state_pool[i], lw, PAST_LEN)
        for acc, val in zip(outs_s, (c_new, s_new, p_new, v_new)):
            acc.append(val)
        x1 = _merge(xs.reshape(n * s, D_MODEL), ya, yb, yc, lw)
        xs = _ffn(x1, p_sample[i].reshape(n * s, PLE_DIM), lw, fin, final).reshape(n, s, D_MODEL)
    stack = lambda vals: jnp.stack(vals)
    return (xp, xs, stack(outs_p[0]), stack(outs_p[1]), stack(outs_p[2]), stack(outs_p[3]),
            stack(outs_s[0]), stack(outs_s[1]), stack(outs_s[2]), stack(outs_s[3]))
```

```python
import functools
import math

import jax
import jax.numpy as jnp
from jax import lax
from jax.experimental import pallas as pl
from jax.experimental.pallas import tpu as pltpu

F32 = jnp.float32
BF16 = jnp.bfloat16

D_MODEL = 1024
DEPTH = 4
PAST_LEN = 16384
SSD_HEAD_DIM = 64
SSD_HEADS = 16
SSD_INNER = 1024
SSD_GROUPS = 2
SSD_STATE = 128
SSD_CONV = 4
SSD_CHUNK = 128
SSD_CONV_DIM = 1536
SGU_WIDTH = 512
SGU_GROUPS = 4
SGU_CHUNK = 128
SGU_GW = 128
POOL_WIDTH = 512
POOL_WINDOWS = (2, 4, 8, 16)
POOL_GW = 128
POOL_BUF = 15
D_FF = 2816
PLE_DIM = 256
EPS = 1e-6
O_Z = 0
O_XBC = 1024
O_DT = 2560
O_UV = 2576
O_POOL = 3600
O_GATE = 4112

LANES = 128
SUBLANES = 8
HEADS_PER_GROUP = SSD_HEADS // SSD_GROUPS
GROUP_INNER = SSD_INNER // SSD_GROUPS
VMEM_LIMIT = 56 * 1024 * 1024

PROMPT_BLOCK = 256
SAMPLE_SEQS = 16
TOKEN_BLOCK = 512
FF_CHUNK = D_FF // 2


def _dot(a, b):
    return jnp.dot(a, b, preferred_element_type=F32)


def _dot_nt(a, b):
    return lax.dot_general(a, b, (((1,), (1,)), ((), ())), preferred_element_type=F32)


def _dot_tn(a, b):
    return lax.dot_general(a, b, (((0,), (0,)), ((), ())), preferred_element_type=F32)


def _sigmoid(x):
    return 0.5 * jnp.tanh(0.5 * x) + 0.5


def _silu(x):
    return x * _sigmoid(x)


def _gelu_tanh(x):
    c = math.sqrt(2.0 / math.pi)
    return 0.5 * x * (1.0 + jnp.tanh(c * (x + 0.044715 * (x * x * x))))


def _softplus(x):
    return jnp.maximum(x, 0.0) + jnp.log1p(jnp.exp(-jnp.abs(x)))


def _rmsnorm(x, g):
    ms = jnp.mean(x * x, axis=-1, keepdims=True)
    return x * lax.rsqrt(ms + EPS) * g


def _split2(v):
    hi = v.astype(BF16)
    lo = (v - hi.astype(F32)).astype(BF16)
    return jnp.concatenate([hi, lo], axis=-1)


def _expand_heads(v, e2):
    return _dot(_split2(v), e2)


def _group_rmsnorm(y, g):
    parts = []
    for k in range(SSD_GROUPS):
        yk = y[:, k * GROUP_INNER:(k + 1) * GROUP_INNER]
        ms = jnp.mean(yk * yk, axis=-1, keepdims=True)
        parts.append(yk * lax.rsqrt(ms + EPS))
    return jnp.concatenate(parts, axis=-1) * g


def _row_iota(shape):
    return lax.broadcasted_iota(jnp.int32, shape, 0)


def _lane_iota(shape):
    return lax.broadcasted_iota(jnp.int32, shape, 1)


def _ssd_chunk(xs_bf, bm, cm, dtc, a_row, ht_ref):
    q = SSD_CHUNK
    tri = _row_iota((q, q)) >= _lane_iota((q, q))
    tril_bf = jnp.where(tri, 1.0, 0.0).astype(BF16)
    da = dtc * a_row
    hi = da.astype(BF16)
    r1 = da - hi.astype(F32)
    mid = r1.astype(BF16)
    lo = (r1 - mid.astype(F32)).astype(BF16)
    cs3 = _dot(tril_bf, jnp.concatenate([hi, mid, lo], axis=1))
    a_cs = cs3[:, :LANES] + cs3[:, LANES:2 * LANES] + cs3[:, 2 * LANES:]
    a_t = a_cs.T
    ap_t = (a_cs - jnp.log(dtc)).T
    w_t = jnp.exp(a_t[:, q - 1:q] - ap_t)
    c_bf = cm.astype(BF16)
    lane = _lane_iota((q, LANES))
    ys = []
    for g in range(SSD_GROUPS):
        scol = slice(g * SSD_STATE, (g + 1) * SSD_STATE)
        cg = c_bf[:, scol]
        cb = _dot_nt(cg, bm[:, scol].astype(BF16))
        bg_t = bm[:, scol].T
        ht_g = ht_ref[:, g * GROUP_INNER:(g + 1) * GROUP_INNER]
        yoff = _dot(cg, ht_g.astype(BF16))
        for pr in range(HEADS_PER_GROUP // 2):
            h0 = g * HEADS_PER_GROUP + 2 * pr
            acols, ms, bws = [], [], []
            for h in (h0, h0 + 1):
                acol = jnp.broadcast_to(a_cs[:, h:h + 1], (q, q))
                acols.append(acol)
                ms.append((jnp.where(tri, jnp.exp(acol - ap_t[h:h + 1, :]), 0.0) * cb).astype(BF16))
                bws.append((bg_t * w_t[h:h + 1, :]).astype(BF16))
            lhs = jnp.concatenate([jnp.concatenate(ms, axis=1), jnp.concatenate(bws, axis=1)], axis=0)
            xpair = xs_bf[:, h0 * SSD_HEAD_DIM:(h0 + 2) * SSD_HEAD_DIM]
            zero = jnp.zeros_like(xpair)
            rhs = jnp.concatenate([jnp.where(lane < SSD_HEAD_DIM, xpair, zero),
                                   jnp.where(lane >= SSD_HEAD_DIM, xpair, zero)], axis=0)
            out = _dot(lhs, rhs)
            ea = jnp.exp(jnp.where(lane < SSD_HEAD_DIM, acols[0], acols[1]))
            ys.append(out[:q] + yoff[:, pr * LANES:(pr + 1) * LANES] * ea)
            cols = slice(h0 * SSD_HEAD_DIM, (h0 + 2) * SSD_HEAD_DIM)
            ht_ref[:, cols] = ht_g[:, pr * LANES:(pr + 1) * LANES] * ea[q - 1:q, :] + out[q:]
    return jnp.concatenate(ys, axis=1)


def _branch_prompt_kernel(x_ref, nm_ref, wz_ref, wxbc_ref, wdt_ref, wuv_ref, wpool_ref,
                          convw_ref, convb_ref, dtb_ref, alog_ref, dskip_ref, ssdn_ref,
                          lng_ref, lnb_ref, wsp_ref, bsp_ref, poolw_ref, pools_ref,
                          ya_ref, yb_ref, yc_ref, convo_ref, ssmo_ref, poolo_ref, vo_ref,
                          hb_ref, ht_ref, xp_ref, xcv_ref, dt_ref, y_ref, pp_ref, *, tb, nblk):
    j = pl.program_id(1)
    nchunk = tb // SSD_CHUNK

    @pl.when(j == 0)
    def _():
        ht_ref[...] = jnp.zeros_like(ht_ref)
        xp_ref[0:SUBLANES, :] = jnp.zeros((SUBLANES, SSD_CONV_DIM), F32)
        pp_ref[0:2 * SUBLANES, :] = jnp.zeros((2 * SUBLANES, POOL_WIDTH), F32)

    hb_ref[...] = _rmsnorm(x_ref[...], nm_ref[...]).astype(BF16)

    xp_ref[SUBLANES:SUBLANES + tb, :] = _dot(hb_ref[...], wxbc_ref[...])
    xp = xp_ref[...]
    acc = xp[SUBLANES:, :] * convw_ref[SSD_CONV - 1:SSD_CONV, :]
    for k in range(SSD_CONV - 1):
        acc = acc + pltpu.roll(xp, SSD_CONV - 1 - k, 0)[SUBLANES:, :] * convw_ref[k:k + 1, :]
    xcv_ref[...] = _silu(acc + convb_ref[...])
    dt_ref[...] = _softplus(_dot(hb_ref[...], wdt_ref[...]) + dtb_ref[...])
    a_row = -jnp.exp(alog_ref[...])
    for c in range(nchunk):
        rows = slice(c * SSD_CHUNK, (c + 1) * SSD_CHUNK)
        y_ref[rows, :] = _ssd_chunk(xcv_ref[rows, 0:SSD_INNER].astype(BF16),
                                    xcv_ref[rows, SSD_INNER:SSD_INNER + SSD_GROUPS * SSD_STATE],
                                    xcv_ref[rows, SSD_INNER + SSD_GROUPS * SSD_STATE:SSD_CONV_DIM],
                                    dt_ref[rows, :], a_row, ht_ref)
    z = _dot(hb_ref[...], wz_ref[...])
    y = (y_ref[...] + dskip_ref[...] * xcv_ref[:, 0:SSD_INNER]) * _silu(z)
    ya_ref[...] = _group_rmsnorm(y, ssdn_ref[...]).astype(BF16)
    tail = xp[tb:tb + SUBLANES, :]
    xp_ref[0:SUBLANES, :] = tail

    a = _gelu_tanh(_dot(hb_ref[...], wuv_ref[...]))
    u = a[:, :SGU_WIDTH]
    v = a[:, SGU_WIDTH:]
    mu = jnp.mean(v, axis=-1, keepdims=True)
    vc = v - mu
    var = jnp.mean(vc * vc, axis=-1, keepdims=True)
    vn = vc * lax.rsqrt(var + EPS) * lng_ref[...] + lnb_ref[...]
    vn_bf = vn.astype(BF16)
    tri = _row_iota((SGU_CHUNK, SGU_CHUNK)) >= _lane_iota((SGU_CHUNK, SGU_CHUNK))
    s_groups = []
    for g in range(SGU_GROUPS):
        wm = jnp.where(tri, wsp_ref[g], 0.0).astype(BF16)
        vg = jnp.concatenate([vn_bf[c * SGU_CHUNK:(c + 1) * SGU_CHUNK, g * SGU_GW:(g + 1) * SGU_GW]
                              for c in range(nchunk)], axis=1)
        s_groups.append(_dot(wm, vg))
    s = jnp.concatenate([jnp.concatenate([sg[:, c * SGU_GW:(c + 1) * SGU_GW] for sg in s_groups], axis=1)
                         + bsp_ref[...] for c in range(nchunk)], axis=0)
    yb_ref[...] = (u * s).astype(BF16)

    pp_ref[2 * SUBLANES:2 * SUBLANES + tb, :] = _dot(hb_ref[...], wpool_ref[...])
    pos = j * tb + _row_iota((tb, POOL_GW))
    yc_cols = []
    for g, w in enumerate(POOL_WINDOWS):
        cols = slice(g * POOL_GW, (g + 1) * POOL_GW)
        wsum = pp_ref[:, cols]
        span = 1
        while span < w:
            wsum = wsum + pltpu.roll(wsum, span, 0)
            span *= 2
        cur = pp_ref[2 * SUBLANES:2 * SUBLANES + tb, cols]
        cnt = jnp.minimum(pos + 1, w).astype(F32)
        d = wsum[2 * SUBLANES:, :] / cnt - cur
        yc_cols.append(_dot(d.astype(BF16), poolw_ref[g]))
    yc_ref[...] = (jnp.concatenate(yc_cols, axis=1) * pools_ref[...]).astype(BF16)
    ptail = pp_ref[tb:tb + 2 * SUBLANES, :]
    pp_ref[0:2 * SUBLANES, :] = ptail

    @pl.when(j == nblk - 1)
    def _():
        convo_ref[...] = tail
        ssmo_ref[...] = ht_ref[...].T
        poolo_ref[...] = ptail
        vo_ref[...] = vn[tb - SGU_CHUNK:tb, :]


def _const_spec(arr):
    nd = arr.ndim
    return pl.BlockSpec(arr.shape, lambda *_: (0,) * nd, pipeline_mode=pl.Buffered(1))


def _layer_spec(arr, layer):
    nd = arr.ndim - 1
    return pl.BlockSpec((None,) + arr.shape[1:], lambda *_: (layer,) + (0,) * nd, pipeline_mode=pl.Buffered(1))


def _skip_refs(body, first, count):
    def wrapped(*refs):
        return body(*refs[:first], *refs[first + count:])
    return wrapped


BRANCH_WEIGHTS = ('norm_mix', 'w_z', 'w_xbc', 'w_dt', 'w_uv', 'w_pool', 'conv_w', 'conv_b', 'dt_bias', 'a_log',
                  'd_skip', 'ssd_norm', 'ln_g', 'ln_b')
N_STATE_OUTPUTS = 4


def _branch_prompt(x, sw, layer, prev_states):
    b, l, _ = x.shape
    depth = sw['w_z'].shape[0]
    tb = PROMPT_BLOCK
    nblk = l // tb
    assert l % tb == 0 and tb % SSD_CHUNK == 0 and tb >= 2 * SUBLANES
    stacked = [sw[k] for k in BRANCH_WEIGHTS + ('w_sp', 'b_sp', 'pool_w', 'pool_scale')]
    shared = []
    tok = lambda width: pl.BlockSpec((None, tb, width), lambda i, j: (i, j, 0))
    per_seq = lambda rows, width: pl.BlockSpec((None, None, rows, width), lambda i, j: (layer, i, 0, 0))
    out_shape = (jax.ShapeDtypeStruct((b, l, SSD_INNER), BF16),
                 jax.ShapeDtypeStruct((b, l, SGU_WIDTH), BF16),
                 jax.ShapeDtypeStruct((b, l, POOL_WIDTH), BF16),
                 jax.ShapeDtypeStruct((depth, b, SUBLANES, SSD_CONV_DIM), F32),
                 jax.ShapeDtypeStruct((depth, b, SSD_INNER, SSD_STATE), F32),
                 jax.ShapeDtypeStruct((depth, b, 2 * SUBLANES, POOL_WIDTH), F32),
                 jax.ShapeDtypeStruct((depth, b, SGU_CHUNK, SGU_WIDTH), F32))
    out_specs = (tok(SSD_INNER), tok(SGU_WIDTH), tok(POOL_WIDTH),
                 per_seq(SUBLANES, SSD_CONV_DIM), per_seq(SSD_INNER, SSD_STATE),
                 per_seq(2 * SUBLANES, POOL_WIDTH), per_seq(SGU_CHUNK, SGU_WIDTH))
    scratch = [pltpu.VMEM((tb, D_MODEL), BF16),
               pltpu.VMEM((SSD_STATE, SSD_INNER), F32),
               pltpu.VMEM((tb + SUBLANES, SSD_CONV_DIM), F32),
               pltpu.VMEM((tb, SSD_CONV_DIM), F32),
               pltpu.VMEM((tb, LANES), F32),
               pltpu.VMEM((tb, SSD_INNER), F32),
               pltpu.VMEM((tb + 2 * SUBLANES, POOL_WIDTH), F32)]
    body = functools.partial(_branch_prompt_kernel, tb=tb, nblk=nblk)
    in_specs = ([tok(D_MODEL)] + [_layer_spec(w, layer) for w in stacked] + [_const_spec(w) for w in shared])
    args = [x] + stacked + shared
    aliases = {}
    if prev_states is not None:
        body = _skip_refs(body, len(args), N_STATE_OUTPUTS)
        aliases = {len(args) + k: 3 + k for k in range(N_STATE_OUTPUTS)}
        in_specs = in_specs + [pl.BlockSpec(memory_space=pl.ANY)] * N_STATE_OUTPUTS
        args = args + list(prev_states)
    ya, yb, yc, *states = pl.pallas_call(
        body, grid=(b, nblk), in_specs=in_specs,
        out_specs=out_specs, out_shape=out_shape, scratch_shapes=scratch,
        input_output_aliases=aliases,
        compiler_params=pltpu.CompilerParams(dimension_semantics=("arbitrary", "arbitrary"),
                                             vmem_limit_bytes=VMEM_LIMIT),
        name="branch_prompt",
    )(*args)
    return ya, yb, yc, states


def _branch_sample_kernel(x_ref, convs_ref, ssms_ref, pools_in_ref, nm_ref, wz_ref, wxbc_ref, wdt_ref, wuv_ref,
                          wpool_ref, convw_ref, convb_ref, dtb_ref, alog_ref, dskip_ref, ssdn_ref,
                          lng_ref, lnb_ref, wd_ref, b8_ref, poolw_ref, pools_ref, e2_ref,
                          ya_ref, yb_ref, yc_ref, convo_ref, ssmo_ref, poolo_ref, vo_ref,
                          c_ref, bm_ref, xd_ref, daug_ref, yoff_ref, *, nb, seq, start):
    r = nb * seq
    assert seq == SUBLANES
    t128 = _row_iota((r, LANES)) % seq

    def shift_rows(cur, prev, j, width):
        tt = _row_iota((r, width)) % seq
        a = pltpu.roll(cur, j, 0)
        if prev is None:
            return jnp.where(tt >= j, a, 0.0)
        return jnp.where(tt >= j, a, pltpu.roll(prev, r - seq + j, 0))

    hb = _rmsnorm(x_ref[...].reshape(r, D_MODEL), nm_ref[...]).astype(BF16)

    xbc = _dot(hb, wxbc_ref[...])
    cbuf = convs_ref[...].reshape(r, SSD_CONV_DIM)
    acc = xbc * convw_ref[SSD_CONV - 1:SSD_CONV, :]
    for k in range(SSD_CONV - 1):
        acc = acc + shift_rows(xbc, cbuf, SSD_CONV - 1 - k, SSD_CONV_DIM) * convw_ref[k:k + 1, :]
    xcv = _silu(acc + convb_ref[...])
    convo_ref[...] = xbc.reshape(nb, seq, SSD_CONV_DIM)
    xs = xcv[:, 0:SSD_INNER]
    bm = xcv[:, SSD_INNER:SSD_INNER + SSD_GROUPS * SSD_STATE]
    cm = xcv[:, SSD_INNER + SSD_GROUPS * SSD_STATE:SSD_CONV_DIM]
    dt = _softplus(_dot(hb, wdt_ref[...]) + dtb_ref[...])
    a_row = -jnp.exp(alog_ref[...])
    a_cs = dt * a_row
    for s in (1, 2, 4):
        a_cs = a_cs + jnp.where(t128 >= s, pltpu.roll(a_cs, s, 0), 0.0)
    a3 = a_cs.reshape(nb, seq, LANES)
    tot = jnp.broadcast_to(a3[:, seq - 1:seq, :], (nb, seq, LANES)).reshape(r, LANES)
    e2 = e2_ref[...]
    exp_a_e = _expand_heads(jnp.exp(a_cs), e2)
    decst_e = _expand_heads(jnp.exp(tot - a_cs), e2)
    dt_e = _expand_heads(dt, e2)
    dectot_e = _expand_heads(jnp.exp(tot), e2)
    xdt = xs * dt_e
    lane = _lane_iota((r, LANES))
    y = jnp.zeros((r, SSD_INNER), F32)
    for j in range(seq):
        if j == 0:
            lj = jnp.ones((r, LANES), F32)
            b_sh, x_sh = bm, xdt
        else:
            lj = jnp.where(t128 >= j, jnp.exp(a_cs - pltpu.roll(a_cs, j, 0)), 0.0)
            b_sh, x_sh = pltpu.roll(bm, j, 0), pltpu.roll(xdt, j, 0)
        prod = cm * b_sh
        cb0 = jnp.sum(prod[:, 0:SSD_STATE], axis=-1, keepdims=True)
        cb1 = jnp.sum(prod[:, SSD_STATE:2 * SSD_STATE], axis=-1, keepdims=True)
        mj = lj * jnp.where(lane < HEADS_PER_GROUP, cb0, cb1)
        y = y + _expand_heads(mj, e2) * x_sh
    c_ref[...] = cm
    bm_ref[...] = bm
    xd_ref[...] = xdt * decst_e
    dec_hi = dectot_e.astype(BF16).astype(F32)
    t1024 = _row_iota((r, SSD_INNER)) % seq
    daug_ref[...] = jnp.where(t1024 == 0, dec_hi, jnp.where(t1024 == 1, dectot_e - dec_hi, 0.0))
    t8 = _row_iota((seq, LANES))
    ones2 = jnp.where(t8 < 2, 1.0, 0.0)
    zeros8 = jnp.zeros((seq, LANES), F32)

    def per_seq(n, carry):
        rows = pl.ds(pl.multiple_of(n * seq, seq), seq)
        for g in range(SSD_GROUPS):
            cols = slice(g * GROUP_INNER, (g + 1) * GROUP_INNER)
            scol = slice(g * SSD_STATE, (g + 1) * SSD_STATE)
            h0 = ssms_ref[n, cols, :]
            cg = c_ref[rows, scol].astype(BF16)
            yoff_ref[rows, cols] = _dot_nt(cg, h0.astype(BF16))
            lhs = jnp.concatenate([xd_ref[rows, cols], daug_ref[rows, cols]], axis=0).astype(BF16)
            rhs = jnp.concatenate([jnp.concatenate([bm_ref[rows, scol], zeros8], axis=1),
                                   jnp.concatenate([zeros8, ones2], axis=1)], axis=0).astype(BF16)
            upd = _dot_tn(lhs, rhs)
            ssmo_ref[n, cols, :] = upd[:, SSD_STATE:] * h0 + upd[:, :SSD_STATE]
        return carry

    lax.fori_loop(0, nb, per_seq, 0)
    z = _dot(hb, wz_ref[...])
    y = (y + yoff_ref[...] * exp_a_e + dskip_ref[...] * xs) * _silu(z)
    ya_ref[...] = _group_rmsnorm(y, ssdn_ref[...]).astype(BF16)

    a = _gelu_tanh(_dot(hb, wuv_ref[...]))
    u = a[:, :SGU_WIDTH]
    v = a[:, SGU_WIDTH:]
    mu = jnp.mean(v, axis=-1, keepdims=True)
    vc = v - mu
    var = jnp.mean(vc * vc, axis=-1, keepdims=True)
    vn = vc * lax.rsqrt(var + EPS) * lng_ref[...] + lnb_ref[...]
    vo_ref[...] = vn
    s = vn.reshape(nb, seq, SGU_WIDTH) * wd_ref[0] + b8_ref[...]
    for j in range(1, seq):
        s = s + pltpu.roll(vn, j, 0).reshape(nb, seq, SGU_WIDTH) * wd_ref[j]
    yb_ref[...] = (u * s.reshape(r, SGU_WIDTH)).astype(BF16)

    xc = _dot(hb, wpool_ref[...])
    pbuf = pools_in_ref[...]
    t0 = pbuf[:, 0:seq, :].reshape(r, POOL_WIDTH)
    t1 = pbuf[:, seq:2 * seq, :].reshape(r, POOL_WIDTH)
    poolo_ref[:, 0:seq, :] = pbuf[:, seq:2 * seq, :]
    poolo_ref[:, seq:2 * seq, :] = xc.reshape(nb, seq, POOL_WIDTH)
    pos = start + (_row_iota((r, POOL_GW)) % seq)
    yc_cols = []
    for g, w in enumerate(POOL_WINDOWS):
        cols = slice(g * POOL_GW, (g + 1) * POOL_GW)
        tiles = [t0[:, cols], t1[:, cols], xc[:, cols]]
        span = 1
        while span < min(w, seq):
            prev = [None] + tiles[:-1]
            tiles = [tl + shift_rows(tl, pv, span, POOL_GW) for tl, pv in zip(tiles, prev)]
            span *= 2
        wsum = tiles[2] if w <= seq else tiles[2] + tiles[1]
        cnt = jnp.minimum(pos + 1, w).astype(F32)
        d = wsum / cnt - xc[:, cols]
        yc_cols.append(_dot(d.astype(BF16), poolw_ref[g]))
    yc_ref[...] = (jnp.concatenate(yc_cols, axis=1) * pools_ref[...]).astype(BF16)


def _branch_sample(x, conv_pad, ssm_flat, pool_pad, sw, layer, prev_states, start):
    n, seq, _ = x.shape
    depth = sw['w_z'].shape[0]
    nb = SAMPLE_SEQS
    assert n % nb == 0 and seq == SUBLANES and start >= max(POOL_WINDOWS)
    r = nb * seq
    stacked = [sw[k] for k in BRANCH_WEIGHTS + ('w_diag8', 'b8', 'pool_w', 'pool_scale')]
    shared = [sw['e2']]
    seq3 = lambda rows, width: pl.BlockSpec((nb, rows, width), lambda i: (i, 0, 0))
    seq4 = lambda rows, width: pl.BlockSpec((None, nb, rows, width), lambda i: (layer, i, 0, 0))
    tok = lambda width: pl.BlockSpec((r, width), lambda i: (i, 0))
    out_shape = (jax.ShapeDtypeStruct((n * seq, SSD_INNER), BF16),
                 jax.ShapeDtypeStruct((n * seq, SGU_WIDTH), BF16),
                 jax.ShapeDtypeStruct((n * seq, POOL_WIDTH), BF16),
                 jax.ShapeDtypeStruct((depth, n, seq, SSD_CONV_DIM), F32),
                 jax.ShapeDtypeStruct((depth, n, SSD_INNER, SSD_STATE), F32),
                 jax.ShapeDtypeStruct((depth, n, 2 * seq, POOL_WIDTH), F32),
                 jax.ShapeDtypeStruct((depth, n * seq, SGU_WIDTH), F32))
    out_specs = (tok(SSD_INNER), tok(SGU_WIDTH), tok(POOL_WIDTH), seq4(seq, SSD_CONV_DIM),
                 seq4(SSD_INNER, SSD_STATE), seq4(2 * seq, POOL_WIDTH),
                 pl.BlockSpec((None, r, SGU_WIDTH), lambda i: (layer, i, 0)))
    scratch = [pltpu.VMEM((r, SSD_GROUPS * SSD_STATE), F32),
               pltpu.VMEM((r, SSD_GROUPS * SSD_STATE), F32),
               pltpu.VMEM((r, SSD_INNER), F32),
               pltpu.VMEM((r, SSD_INNER), F32),
               pltpu.VMEM((r, SSD_INNER), F32)]
    body = functools.partial(_branch_sample_kernel, nb=nb, seq=seq, start=start)
    in_specs = ([seq3(seq, D_MODEL), seq4(seq, SSD_CONV_DIM), seq4(SSD_INNER, SSD_STATE), seq4(2 * seq, POOL_WIDTH)]
                + [_layer_spec(w, layer) for w in stacked] + [_const_spec(w) for w in shared])
    args = [x, conv_pad, ssm_flat, pool_pad] + stacked + shared
    aliases = {}
    if prev_states is not None:
        body = _skip_refs(body, len(args), N_STATE_OUTPUTS)
        aliases = {len(args) + k: 3 + k for k in range(N_STATE_OUTPUTS)}
        in_specs = in_specs + [pl.BlockSpec(memory_space=pl.ANY)] * N_STATE_OUTPUTS
        args = args + list(prev_states)
    ya, yb, yc, *states = pl.pallas_call(
        body, grid=(n // nb,), in_specs=in_specs,
        out_specs=out_specs, out_shape=out_shape, scratch_shapes=scratch,
        input_output_aliases=aliases,
        compiler_params=pltpu.CompilerParams(dimension_semantics=("arbitrary",),
                                             vmem_limit_bytes=VMEM_LIMIT),
        name="branch_sample",
    )(*args)
    return ya, yb, yc, states


def _merge_kernel(x_ref, ya_ref, yb_ref, yc_ref, nm_ref, wg_ref, wa_ref, wb_ref, wc_ref, wo_ref, o_ref):
    x = x_ref[...]
    hb = _rmsnorm(x, nm_ref[...]).astype(BF16)
    m = _sigmoid(_dot(hb, wg_ref[:, 0:D_MODEL])) * _dot(ya_ref[...], wa_ref[...])
    m = m + _sigmoid(_dot(hb, wg_ref[:, D_MODEL:2 * D_MODEL])) * _dot(yb_ref[...], wb_ref[...])
    m = m + _sigmoid(_dot(hb, wg_ref[:, 2 * D_MODEL:3 * D_MODEL])) * _dot(yc_ref[...], wc_ref[...])
    o_ref[...] = x + _dot(m.astype(BF16), wo_ref[...])


def _merge(x, ya, yb, yc, sw, layer):
    t = x.shape[0]
    tm = min(TOKEN_BLOCK, t)
    assert t % tm == 0
    stacked = [sw[k] for k in ('norm_mix', 'w_gate', 'w_br_a', 'w_br_b', 'w_br_c', 'w_out')]
    tok = lambda width: pl.BlockSpec((tm, width), lambda i: (i, 0))
    return pl.pallas_call(
        _merge_kernel, grid=(t // tm,),
        in_specs=[tok(D_MODEL), tok(SSD_INNER), tok(SGU_WIDTH), tok(POOL_WIDTH)] + [_layer_spec(w, layer) for w in stacked],
        out_specs=tok(D_MODEL), out_shape=jax.ShapeDtypeStruct((t, D_MODEL), F32),
        compiler_params=pltpu.CompilerParams(dimension_semantics=("arbitrary",), vmem_limit_bytes=VMEM_LIMIT),
        name="merge",
    )(x, ya, yb, yc, *stacked)


def _ffn_kernel(x_ref, p_ref, nf_ref, wg_ref, wu_ref, wd_ref, np_ref, wpg_ref, wpu_ref, fn_ref, o_ref, *, final):
    x = x_ref[...]
    hb = _rmsnorm(x, nf_ref[...]).astype(BF16)
    acc = x
    for c in range(D_FF // FF_CHUNK):
        cols = slice(c * FF_CHUNK, (c + 1) * FF_CHUNK)
        act = (_silu(_dot(hb, wg_ref[:, cols])) * _dot(hb, wu_ref[:, cols])).astype(BF16)
        acc = acc + _dot(act, wd_ref[cols, :])
    hb = _rmsnorm(acc, np_ref[...]).astype(BF16)
    out = acc + _dot(p_ref[...].astype(BF16), wpu_ref[...]) * _sigmoid(_dot(hb, wpg_ref[...]))
    if final:
        out = _rmsnorm(out, fn_ref[...])
    o_ref[...] = out


def _ffn(x, p, sw, layer, final):
    t = x.shape[0]
    tm = min(TOKEN_BLOCK, t)
    assert t % tm == 0
    stacked = [sw[k] for k in ('norm_ffn', 'w_ffn_gate', 'w_ffn_up', 'w_down', 'norm_ple', 'w_ple_gate', 'w_ple_up')]
    tok = lambda width: pl.BlockSpec((tm, width), lambda i: (i, 0))
    return pl.pallas_call(
        functools.partial(_ffn_kernel, final=final), grid=(t // tm,),
        in_specs=([tok(D_MODEL), pl.BlockSpec((None, tm, PLE_DIM), lambda i: (layer, i, 0))]
                  + [_layer_spec(w, layer) for w in stacked] + [_const_spec(sw['final_norm'])]),
        out_specs=tok(D_MODEL), out_shape=jax.ShapeDtypeStruct((t, D_MODEL), F32),
        compiler_params=pltpu.CompilerParams(dimension_semantics=("arbitrary",), vmem_limit_bytes=VMEM_LIMIT),
        name="ffn",
    )(x, p, *stacked, sw['final_norm'])


def _head_expansion():
    rows = jnp.arange(2 * LANES)[:, None] % LANES
    cols = jnp.arange(SSD_INNER)[None, :] // SSD_HEAD_DIM
    e2 = (rows == cols).astype(BF16)
    return e2, e2.T


def _prepare_weights(norm_mix, w_in, conv_w, conv_b, dt_bias, a_log, d_skip, ssd_norm, sgu_ln_g, sgu_ln_b,
                     w_spatial, b_spatial, pool_w, pool_scale, w_br_a, w_br_b, w_br_c, w_out, norm_ffn,
                     w_gate_up, w_down, norm_ple, w_ple_gate, w_ple_up, final_norm):
    depth = w_in.shape[0]
    row = lambda v: v.reshape(depth, 1, -1).astype(F32)
    pad_heads = lambda v: jnp.pad(v.reshape(depth, 1, -1).astype(F32), ((0, 0), (0, 0), (0, LANES - SSD_HEADS)))
    seq = SUBLANES
    w8 = w_spatial[:, :, :seq, :seq]
    tt = jnp.arange(seq)
    lag = tt[None, :] - tt[:, None]
    w_lag = jnp.where(lag >= 0, w8[:, :, tt[None, :], jnp.clip(lag, 0, seq - 1)], 0.0)
    w_diag8 = jnp.repeat(jnp.transpose(w_lag, (0, 2, 3, 1)), SGU_GW, axis=3).astype(F32)
    b_full = jnp.repeat(jnp.transpose(b_spatial, (0, 2, 1)), SGU_GW, axis=2).astype(F32)
    e2, et2 = _head_expansion()
    return {
        'norm_mix': row(norm_mix),
        'w_z': w_in[:, :, O_Z:O_XBC].astype(BF16),
        'w_xbc': w_in[:, :, O_XBC:O_DT].astype(BF16),
        'w_dt': jnp.pad(w_in[:, :, O_DT:O_UV], ((0, 0), (0, 0), (0, LANES - SSD_HEADS))).astype(BF16),
        'w_uv': w_in[:, :, O_UV:O_POOL].astype(BF16),
        'w_pool': w_in[:, :, O_POOL:O_GATE].astype(BF16),
        'w_gate': w_in[:, :, O_GATE:].astype(BF16),
        'conv_w': conv_w.astype(F32), 'conv_b': row(conv_b),
        'dt_bias': pad_heads(dt_bias), 'a_log': pad_heads(a_log),
        'd_skip': row(jnp.repeat(d_skip, SSD_HEAD_DIM, axis=1)),
        'ssd_norm': row(ssd_norm),
        'ln_g': row(sgu_ln_g), 'ln_b': row(sgu_ln_b),
        'w_sp': w_spatial.astype(F32), 'b_sp': b_full,
        'w_diag8': w_diag8, 'b8': b_full[:, :seq],
        'pool_w': pool_w.astype(BF16), 'pool_scale': row(pool_scale),
        'w_br_a': w_br_a.astype(BF16), 'w_br_b': w_br_b.astype(BF16), 'w_br_c': w_br_c.astype(BF16),
        'w_out': w_out.astype(BF16),
        'norm_ffn': row(norm_ffn),
        'w_ffn_gate': w_gate_up[:, :, :D_FF].astype(BF16), 'w_ffn_up': w_gate_up[:, :, D_FF:].astype(BF16),
        'w_down': w_down.astype(BF16),
        'norm_ple': row(norm_ple),
        'w_ple_gate': w_ple_gate.astype(BF16), 'w_ple_up': w_ple_up.astype(BF16),
        'final_norm': final_norm.reshape(1, -1).astype(F32),
        'e2': e2, 'et2': et2,
    }


def kernel(x_prompt, x_sample, state_conv, state_ssm, state_pool, p_prompt, p_sample, norm_mix, w_in, conv_w, conv_b, dt_bias, a_log, d_skip, ssd_norm, sgu_ln_g, sgu_ln_b, w_spatial, b_spatial, pool_w, pool_scale, w_br_a, w_br_b, w_br_c, w_out, norm_ffn, w_gate_up, w_down, norm_ple, w_ple_gate, w_ple_up, final_norm):
    b, l, _ = x_prompt.shape
    n, s, _ = x_sample.shape
    depth = w_in.shape[0]
    sw = _prepare_weights(norm_mix, w_in, conv_w, conv_b, dt_bias, a_log, d_skip, ssd_norm, sgu_ln_g, sgu_ln_b,
                          w_spatial, b_spatial, pool_w, pool_scale, w_br_a, w_br_b, w_br_c, w_out, norm_ffn,
                          w_gate_up, w_down, norm_ple, w_ple_gate, w_ple_up, final_norm)
    conv_pad = jnp.pad(state_conv, ((0, 0), (0, 0), (SUBLANES - (SSD_CONV - 1), 0), (0, 0)))
    pool_pad = jnp.pad(state_pool, ((0, 0), (0, 0), (2 * SUBLANES - POOL_BUF, 0), (0, 0)))
    ssm_flat = state_ssm.reshape(depth, n, SSD_INNER, SSD_STATE)
    pp = p_prompt.reshape(depth, b * l, PLE_DIM)
    ps = p_sample.reshape(depth, n * s, PLE_DIM)
    xp = x_prompt
    xs = x_sample.reshape(n * s, D_MODEL)
    st_p = None
    st_s = None
    for i in range(depth):
        final = i == depth - 1
        ya, yb, yc, st_p = _branch_prompt(xp, sw, i, st_p)
        x1 = _merge(xp.reshape(b * l, D_MODEL), ya.reshape(b * l, -1), yb.reshape(b * l, -1), yc.reshape(b * l, -1), sw, i)
        xp = _ffn(x1, pp, sw, i, final).reshape(b, l, D_MODEL)
        ya, yb, yc, st_s = _branch_sample(xs.reshape(n, s, D_MODEL), conv_pad, ssm_flat, pool_pad, sw, i, st_s, PAST_LEN)
        x1 = _merge(xs, ya, yb, yc, sw, i)
        xs = _ffn(x1, ps, sw, i, final)
    conv_p, ssm_p, pool_p, v_p = st_p
    conv_s, ssm_s, pool_s, v_s = st_s
    return (xp, xs.reshape(n, s, D_MODEL),
            conv_p[:, :, SUBLANES - (SSD_CONV - 1):, :],
            ssm_p.reshape(depth, b, SSD_HEADS, SSD_HEAD_DIM, SSD_STATE),
            pool_p[:, :, 2 * SUBLANES - POOL_BUF:, :],
            v_p,
            conv_s[:, :, s - (SSD_CONV - 1):, :],
            ssm_s.reshape(depth, n, SSD_HEADS, SSD_HEAD_DIM, SSD_STATE),
            pool_s[:, :, 2 * s - POOL_BUF:, :],
            v_s.reshape(depth, n, s, SGU_WIDTH))
```

```python
import functools
import math

import jax
import jax.numpy as jnp
from jax import lax
from jax.experimental import pallas as pl
from jax.experimental.pallas import tpu as pltpu

F32 = jnp.float32
BF16 = jnp.bfloat16

D_MODEL = 1024
DEPTH = 4
PAST_LEN = 16384
SSD_HEAD_DIM = 64
SSD_HEADS = 16
SSD_INNER = 1024
SSD_GROUPS = 2
SSD_STATE = 128
SSD_CONV = 4
SSD_CHUNK = 128
SSD_CONV_DIM = 1536
SGU_WIDTH = 512
SGU_GROUPS = 4
SGU_CHUNK = 128
SGU_GW = 128
POOL_WIDTH = 512
POOL_WINDOWS = (2, 4, 8, 16)
POOL_GW = 128
POOL_BUF = 15
D_FF = 2816
PLE_DIM = 256
EPS = 1e-6
LOG2_E = 1.4426950408889634
O_Z = 0
O_XBC = 1024
O_DT = 2560
O_UV = 2576
O_POOL = 3600
O_GATE = 4112

LANES = 128
SUBLANES = 8
HEADS_PER_GROUP = SSD_HEADS // SSD_GROUPS
GROUP_INNER = SSD_INNER // SSD_GROUPS
VMEM_LIMIT = 56 * 1024 * 1024

PROMPT_BLOCK = 256
SAMPLE_SEQS = 16
TOKEN_BLOCK = 512
MXU_DEPTH = 256
FF_SPLIT = (D_FF // MXU_DEPTH + 1) // 2 * MXU_DEPTH


def _dot(a, b):
    return jnp.dot(a, b, preferred_element_type=F32)


def _dot_nt(a, b):
    return lax.dot_general(a, b, (((1,), (1,)), ((), ())), preferred_element_type=F32)


def _dot_tn(a, b):
    return lax.dot_general(a, b, (((0,), (0,)), ((), ())), preferred_element_type=F32)


def _sigmoid(x):
    return 0.5 * jnp.tanh(0.5 * x) + 0.5


def _silu(x):
    h = 0.5 * x
    return h * jnp.tanh(h) + h


def _gelu_tanh(x):
    c = math.sqrt(2.0 / math.pi)
    return 0.5 * x * (1.0 + jnp.tanh(c * (x + 0.044715 * (x * x * x))))


def _softplus(x):
    return jnp.maximum(x, 0.0) + jnp.log1p(jnp.exp(-jnp.abs(x)))


def _rmsnorm(x, g):
    ms = jnp.mean(x * x, axis=-1, keepdims=True)
    return x * lax.rsqrt(ms + EPS) * g


def _split2(v):
    hi = v.astype(BF16)
    lo = (v - hi.astype(F32)).astype(BF16)
    return jnp.concatenate([hi, lo], axis=-1)


def _expand_heads(v, e2):
    return _dot(_split2(v), e2)


def _group_rmsnorm(y, g):
    parts = []
    for k in range(SSD_GROUPS):
        yk = y[:, k * GROUP_INNER:(k + 1) * GROUP_INNER]
        ms = jnp.mean(yk * yk, axis=-1, keepdims=True)
        parts.append(yk * lax.rsqrt(ms + EPS))
    return jnp.concatenate(parts, axis=-1) * g


def _row_iota(shape):
    return lax.broadcasted_iota(jnp.int32, shape, 0)


def _lane_iota(shape):
    return lax.broadcasted_iota(jnp.int32, shape, 1)


def _ssd_chunk(xs_bf, bm, cm, dtc, a_row, ht_ref):
    q = SSD_CHUNK
    tri = _row_iota((q, q)) >= _lane_iota((q, q))
    tril_bf = jnp.where(tri, 1.0, 0.0).astype(BF16)
    da = dtc * a_row
    hi = da.astype(BF16)
    r1 = da - hi.astype(F32)
    mid = r1.astype(BF16)
    lo = (r1 - mid.astype(F32)).astype(BF16)
    cs3 = _dot(tril_bf, jnp.concatenate([hi, mid, lo], axis=1))
    a_cs = (cs3[:, :LANES] + cs3[:, LANES:2 * LANES] + cs3[:, 2 * LANES:]) * LOG2_E
    a_t = a_cs.T
    ap_t = (a_cs - jnp.log2(dtc)).T
    w_t = jnp.exp2(a_t[:, q - 1:q] - ap_t)
    c_bf = cm.astype(BF16)
    lane = _lane_iota((q, LANES))
    ys = []
    for g in range(SSD_GROUPS):
        scol = slice(g * SSD_STATE, (g + 1) * SSD_STATE)
        cg = c_bf[:, scol]
        cb = _dot_nt(cg, bm[:, scol].astype(BF16))
        bg_t = bm[:, scol].T
        ht_g = ht_ref[:, g * GROUP_INNER:(g + 1) * GROUP_INNER]
        yoff = _dot(cg, ht_g.astype(BF16))
        for pr in range(HEADS_PER_GROUP // 2):
            h0 = g * HEADS_PER_GROUP + 2 * pr
            acols, ms, bws = [], [], []
            for h in (h0, h0 + 1):
                acol = jnp.broadcast_to(a_cs[:, h:h + 1], (q, q))
                acols.append(acol)
                ms.append((jnp.where(tri, jnp.exp2(acol - ap_t[h:h + 1, :]), 0.0) * cb).astype(BF16))
                bws.append((bg_t * w_t[h:h + 1, :]).astype(BF16))
            lhs = jnp.concatenate([jnp.concatenate(ms, axis=1), jnp.concatenate(bws, axis=1)], axis=0)
            xpair = xs_bf[:, h0 * SSD_HEAD_DIM:(h0 + 2) * SSD_HEAD_DIM]
            zero = jnp.zeros_like(xpair)
            rhs = jnp.concatenate([jnp.where(lane < SSD_HEAD_DIM, xpair, zero),
                                   jnp.where(lane >= SSD_HEAD_DIM, xpair, zero)], axis=0)
            out = _dot(lhs, rhs)
            ea = jnp.exp2(jnp.where(lane < SSD_HEAD_DIM, acols[0], acols[1]))
            ys.append(out[:q] + yoff[:, pr * LANES:(pr + 1) * LANES] * ea)
            cols = slice(h0 * SSD_HEAD_DIM, (h0 + 2) * SSD_HEAD_DIM)
            ht_ref[:, cols] = ht_g[:, pr * LANES:(pr + 1) * LANES] * ea[q - 1:q, :] + out[q:]
    return jnp.concatenate(ys, axis=1)


P_XBC = 0
P_DT = P_XBC + SSD_CONV_DIM
P_Z = P_DT + LANES
P_UV = P_Z + SSD_INNER
P_POOL = P_UV + 2 * SGU_WIDTH
P_WIDTH = P_POOL + POOL_WIDTH
PROJ_PIECE = 512
PIECES_AFTER_PHASE = (1, 1, 1, 1, 1, 2, 1, 1)


def _branch_prompt_kernel(xa_ref, xn_ref, nm_ref, wz_ref, wxbc_ref, wdt_ref, wuv_ref, wpool_ref,
                          convw_ref, convb_ref, dtb_ref, alog_ref, dskip_ref, ssdn_ref,
                          lng_ref, lnb_ref, wsp_ref, bsp_ref, poolw_ref, pools_ref,
                          ya_ref, yb_ref, yc_ref, convo_ref, ssmo_ref, poolo_ref, vo_ref,
                          pa_ref, pb_ref, ht_ref, cc_ref, pc_ref, xcv_ref, y_ref, *, tb, npair):
    i = pl.program_id(0)
    j = pl.program_id(1)
    nchunk = tb // SSD_CHUNK

    def project_pieces(x, p_ref):
        hb = _rmsnorm(x, nm_ref[...]).astype(BF16)
        pieces = []
        for w_ref, base, width in ((wxbc_ref, P_XBC, SSD_CONV_DIM), (wz_ref, P_Z, SSD_INNER),
                                   (wuv_ref, P_UV, 2 * SGU_WIDTH), (wpool_ref, P_POOL, POOL_WIDTH),
                                   (wdt_ref, P_DT, LANES)):
            for c0 in range(0, width, PROJ_PIECE):
                c1 = min(c0 + PROJ_PIECE, width)

                def piece(w_ref=w_ref, base=base, c0=c0, c1=c1):
                    p_ref[:, base + c0:base + c1] = _dot(hb, w_ref[:, c0:c1])
                pieces.append(piece)
        return pieces

    def process_phases(p_ref, half):
        orow = slice(half * tb, (half + 1) * tb)
        res = {}
        phases = []

        def conv(cols):
            xp = jnp.concatenate([cc_ref[:, cols], p_ref[:, P_XBC + cols.start:P_XBC + cols.stop]], axis=0)
            acc = xp[SUBLANES:, :] * convw_ref[SSD_CONV - 1:SSD_CONV, cols]
            for k in range(SSD_CONV - 1):
                acc = acc + pltpu.roll(xp, SSD_CONV - 1 - k, 0)[SUBLANES:, :] * convw_ref[k:k + 1, cols]
            xcv_ref[:, cols] = _silu(acc + convb_ref[:, cols])
        for c0 in range(0, SSD_CONV_DIM, PROJ_PIECE):
            phases.append(functools.partial(conv, slice(c0, c0 + PROJ_PIECE)))

        def scan(c):
            rows = slice(c * SSD_CHUNK, (c + 1) * SSD_CHUNK)
            y_ref[rows, :] = _ssd_chunk(xcv_ref[rows, 0:SSD_INNER].astype(BF16),
                                        xcv_ref[rows, SSD_INNER:SSD_INNER + SSD_GROUPS * SSD_STATE],
                                        xcv_ref[rows, SSD_INNER + SSD_GROUPS * SSD_STATE:SSD_CONV_DIM],
                                        _softplus(p_ref[rows, P_DT:P_Z] + dtb_ref[...]),
                                        -jnp.exp(alog_ref[...]), ht_ref)
        for c in range(nchunk):
            phases.append(functools.partial(scan, c))

        def gate():
            res['tail'] = p_ref[tb - SUBLANES:tb, P_XBC:P_DT]
            cc_ref[...] = res['tail']
            y = (y_ref[...] + dskip_ref[...] * xcv_ref[:, 0:SSD_INNER]) * _silu(p_ref[:, P_Z:P_UV])
            ya_ref[orow, :] = _group_rmsnorm(y, ssdn_ref[...]).astype(BF16)
        phases.append(gate)

        def spatial():
            u = _gelu_tanh(p_ref[:, P_UV:P_UV + SGU_WIDTH])
            v = _gelu_tanh(p_ref[:, P_UV + SGU_WIDTH:P_POOL])
            mu = jnp.mean(v, axis=-1, keepdims=True)
            vc = v - mu
            var = jnp.mean(vc * vc, axis=-1, keepdims=True)
            vn = vc * lax.rsqrt(var + EPS) * lng_ref[...] + lnb_ref[...]
            res['vn'] = vn
            vn_bf = vn.astype(BF16)
            tri = _row_iota((SGU_CHUNK, SGU_CHUNK)) >= _lane_iota((SGU_CHUNK, SGU_CHUNK))
            s_groups = []
            for g in range(SGU_GROUPS):
                wm = jnp.where(tri, wsp_ref[g], 0.0).astype(BF16)
                vg = jnp.concatenate([vn_bf[c * SGU_CHUNK:(c + 1) * SGU_CHUNK, g * SGU_GW:(g + 1) * SGU_GW]
                                      for c in range(nchunk)], axis=1)
                s_groups.append(_dot(wm, vg))
            s = jnp.concatenate([jnp.concatenate([sg[:, c * SGU_GW:(c + 1) * SGU_GW] for sg in s_groups], axis=1)
                                 + bsp_ref[...] for c in range(nchunk)], axis=0)
            yb_ref[orow, :] = (u * s).astype(BF16)
        phases.append(spatial)

        def pool():
            pos = (2 * j + half) * tb + _row_iota((tb, POOL_GW))
            yc_cols = []
            for g, w in enumerate(POOL_WINDOWS):
                cols = slice(P_POOL + g * POOL_GW, P_POOL + (g + 1) * POOL_GW)
                cur = p_ref[:, cols]
                wsum = jnp.concatenate([pc_ref[:, g * POOL_GW:(g + 1) * POOL_GW], cur], axis=0)
                span = 1
                while span < w:
                    wsum = wsum + pltpu.roll(wsum, span, 0)
                    span *= 2
                cnt = jnp.minimum(pos + 1, w).astype(F32)
                d = wsum[2 * SUBLANES:, :] / cnt - cur
                yc_cols.append(_dot(d.astype(BF16), poolw_ref[g]))
            yc_ref[orow, :] = (jnp.concatenate(yc_cols, axis=1) * pools_ref[...]).astype(BF16)
            res['ptail'] = p_ref[tb - 2 * SUBLANES:tb, P_POOL:P_WIDTH]
            pc_ref[...] = res['ptail']
        phases.append(pool)
        return phases, res

    def run(p_cur, half, x_next, p_next):
        phases, res = process_phases(p_cur, half)
        pieces = project_pieces(x_next, p_next)
        assert len(phases) == len(PIECES_AFTER_PHASE) and len(pieces) == sum(PIECES_AFTER_PHASE)
        done = 0
        for phase, count in zip(phases, PIECES_AFTER_PHASE):
            phase()
            for piece in pieces[done:done + count]:
                piece()
            done += count
        return res

    @pl.when(jnp.logical_and(i == 0, j == 0))
    def _():
        for piece in project_pieces(xa_ref[0:tb, :], pa_ref):
            piece()

    @pl.when(j == 0)
    def _():
        ht_ref[...] = jnp.zeros_like(ht_ref)
        cc_ref[...] = jnp.zeros_like(cc_ref)
        pc_ref[...] = jnp.zeros_like(pc_ref)

    run(pa_ref, 0, xa_ref[tb:2 * tb, :], pb_ref)
    res = run(pb_ref, 1, xn_ref[...], pa_ref)

    @pl.when(j == npair - 1)
    def _():
        convo_ref[...] = res['tail']
        ssmo_ref[...] = ht_ref[...].T
        poolo_ref[...] = res['ptail']
        vo_ref[...] = res['vn'][tb - SGU_CHUNK:tb, :]


def _const_spec(arr):
    nd = arr.ndim
    return pl.BlockSpec(arr.shape, lambda *_: (0,) * nd, pipeline_mode=pl.Buffered(1))


def _layer_spec(arr, layer):
    nd = arr.ndim - 1
    return pl.BlockSpec((None,) + arr.shape[1:], lambda *_: (layer,) + (0,) * nd, pipeline_mode=pl.Buffered(1))


def _skip_refs(body, first, count):
    def wrapped(*refs):
        return body(*refs[:first], *refs[first + count:])
    return wrapped


BRANCH_WEIGHTS = ('norm_mix', 'w_z', 'w_xbc', 'w_dt', 'w_uv', 'w_pool', 'conv_w', 'conv_b', 'dt_bias', 'a_log',
                  'd_skip', 'ssd_norm', 'ln_g', 'ln_b')
N_STATE_OUTPUTS = 4


def _branch_prompt(x, sw, layer, prev_states):
    b, l, _ = x.shape
    depth = sw['w_z'].shape[0]
    tb = PROMPT_BLOCK
    npair = l // (2 * tb)
    assert l % (2 * tb) == 0 and tb % SSD_CHUNK == 0 and tb >= 2 * SUBLANES
    stacked = [sw[k] for k in BRANCH_WEIGHTS + ('w_sp', 'b_sp', 'pool_w', 'pool_scale')]
    shared = []
    tok = lambda width: pl.BlockSpec((None, 2 * tb, width), lambda i, j: (i, j, 0))

    def next_block(i, j):
        flat = jnp.minimum(i * npair + j + 1, b * npair - 1)
        return flat // npair, (flat % npair) * 2, 0

    per_seq = lambda rows, width: pl.BlockSpec((None, None, rows, width), lambda i, j: (layer, i, 0, 0))
    out_shape = (jax.ShapeDtypeStruct((b, l, SSD_INNER), BF16),
                 jax.ShapeDtypeStruct((b, l, SGU_WIDTH), BF16),
                 jax.ShapeDtypeStruct((b, l, POOL_WIDTH), BF16),
                 jax.ShapeDtypeStruct((depth, b, SUBLANES, SSD_CONV_DIM), F32),
                 jax.ShapeDtypeStruct((depth, b, SSD_INNER, SSD_STATE), F32),
                 jax.ShapeDtypeStruct((depth, b, 2 * SUBLANES, POOL_WIDTH), F32),
                 jax.ShapeDtypeStruct((depth, b, SGU_CHUNK, SGU_WIDTH), F32))
    out_specs = (tok(SSD_INNER), tok(SGU_WIDTH), tok(POOL_WIDTH),
                 per_seq(SUBLANES, SSD_CONV_DIM), per_seq(SSD_INNER, SSD_STATE),
                 per_seq(2 * SUBLANES, POOL_WIDTH), per_seq(SGU_CHUNK, SGU_WIDTH))
    scratch = [pltpu.VMEM((tb, P_WIDTH), F32),
               pltpu.VMEM((tb, P_WIDTH), F32),
               pltpu.VMEM((SSD_STATE, SSD_INNER), F32),
               pltpu.VMEM((SUBLANES, SSD_CONV_DIM), F32),
               pltpu.VMEM((2 * SUBLANES, POOL_WIDTH), F32),
               pltpu.VMEM((tb, SSD_CONV_DIM), F32),
               pltpu.VMEM((tb, SSD_INNER), F32)]
    body = functools.partial(_branch_prompt_kernel, tb=tb, npair=npair)
    in_specs = ([tok(D_MODEL), pl.BlockSpec((None, tb, D_MODEL), next_block)]
                + [_layer_spec(w, layer) for w in stacked] + [_const_spec(w) for w in shared])
    args = [x, x] + stacked + shared
    aliases = {}
    if prev_states is not None:
        body = _skip_refs(body, len(args), N_STATE_OUTPUTS)
        aliases = {len(args) + k: 3 + k for k in range(N_STATE_OUTPUTS)}
        in_specs = in_specs + [pl.BlockSpec(memory_space=pl.ANY)] * N_STATE_OUTPUTS
        args = args + list(prev_states)
    ya, yb, yc, *states = pl.pallas_call(
        body, grid=(b, npair), in_specs=in_specs,
        out_specs=out_specs, out_shape=out_shape, scratch_shapes=scratch,
        input_output_aliases=aliases,
        compiler_params=pltpu.CompilerParams(dimension_semantics=("arbitrary", "arbitrary"),
                                             vmem_limit_bytes=VMEM_LIMIT),
        name="branch_prompt",
    )(*args)
    return ya, yb, yc, states


def _branch_sample_kernel(x_ref, convs_ref, ssms_ref, pools_in_ref, nm_ref, wz_ref, wxbc_ref, wdt_ref, wuv_ref,
                          wpool_ref, convw_ref, convb_ref, dtb_ref, alog_ref, dskip_ref, ssdn_ref,
                          lng_ref, lnb_ref, wd_ref, b8_ref, poolw_ref, pools_ref, e2_ref,
                          ya_ref, yb_ref, yc_ref, convo_ref, ssmo_ref, poolo_ref, vo_ref,
                          c_ref, bm_ref, xd_ref, daug_ref, yoff_ref, *, nb, seq, start):
    r = nb * seq
    assert seq == SUBLANES
    t128 = _row_iota((r, LANES)) % seq

    def shift_rows(cur, prev, j, width):
        tt = _row_iota((r, width)) % seq
        a = pltpu.roll(cur, j, 0)
        if prev is None:
            return jnp.where(tt >= j, a, 0.0)
        return jnp.where(tt >= j, a, pltpu.roll(prev, r - seq + j, 0))

    hb = _rmsnorm(x_ref[...].reshape(r, D_MODEL), nm_ref[...]).astype(BF16)

    xbc = _dot(hb, wxbc_ref[...])
    cbuf = convs_ref[...].reshape(r, SSD_CONV_DIM)
    acc = xbc * convw_ref[SSD_CONV - 1:SSD_CONV, :]
    for k in range(SSD_CONV - 1):
        acc = acc + shift_rows(xbc, cbuf, SSD_CONV - 1 - k, SSD_CONV_DIM) * convw_ref[k:k + 1, :]
    xcv = _silu(acc + convb_ref[...])
    convo_ref[...] = xbc.reshape(nb, seq, SSD_CONV_DIM)
    xs = xcv[:, 0:SSD_INNER]
    bm = xcv[:, SSD_INNER:SSD_INNER + SSD_GROUPS * SSD_STATE]
    cm = xcv[:, SSD_INNER + SSD_GROUPS * SSD_STATE:SSD_CONV_DIM]
    dt = _softplus(_dot(hb, wdt_ref[...]) + dtb_ref[...])
    a_row = -jnp.exp(alog_ref[...])
    a_cs = dt * a_row
    for s in (1, 2, 4):
        a_cs = a_cs + jnp.where(t128 >= s, pltpu.roll(a_cs, s, 0), 0.0)
    a3 = a_cs.reshape(nb, seq, LANES)
    tot = jnp.broadcast_to(a3[:, seq - 1:seq, :], (nb, seq, LANES)).reshape(r, LANES)
    e2 = e2_ref[...]
    exp_a_e = _expand_heads(jnp.exp(a_cs), e2)
    decst_e = _expand_heads(jnp.exp(tot - a_cs), e2)
    dt_e = _expand_heads(dt, e2)
    dectot_e = _expand_heads(jnp.exp(tot), e2)
    xdt = xs * dt_e
    lane = _lane_iota((r, LANES))
    y = jnp.zeros((r, SSD_INNER), F32)
    for j in range(seq):
        if j == 0:
            lj = jnp.ones((r, LANES), F32)
            b_sh, x_sh = bm, xdt
        else:
            lj = jnp.where(t128 >= j, jnp.exp(a_cs - pltpu.roll(a_cs, j, 0)), 0.0)
            b_sh, x_sh = pltpu.roll(bm, j, 0), pltpu.roll(xdt, j, 0)
        prod = cm * b_sh
        cb0 = jnp.sum(prod[:, 0:SSD_STATE], axis=-1, keepdims=True)
        cb1 = jnp.sum(prod[:, SSD_STATE:2 * SSD_STATE], axis=-1, keepdims=True)
        mj = lj * jnp.where(lane < HEADS_PER_GROUP, cb0, cb1)
        y = y + _expand_heads(mj, e2) * x_sh
    c_ref[...] = cm
    bm_ref[...] = bm
    xd_ref[...] = xdt * decst_e
    dec_hi = dectot_e.astype(BF16).astype(F32)
    t1024 = _row_iota((r, SSD_INNER)) % seq
    daug_ref[...] = jnp.where(t1024 == 0, dec_hi, jnp.where(t1024 == 1, dectot_e - dec_hi, 0.0))
    t8 = _row_iota((seq, LANES))
    ones2 = jnp.where(t8 < 2, 1.0, 0.0)
    zeros8 = jnp.zeros((seq, LANES), F32)

    def per_seq(n, carry):
        rows = pl.ds(pl.multiple_of(n * seq, seq), seq)
        for g in range(SSD_GROUPS):
            cols = slice(g * GROUP_INNER, (g + 1) * GROUP_INNER)
            scol = slice(g * SSD_STATE, (g + 1) * SSD_STATE)
            h0 = ssms_ref[n, cols, :]
            cg = c_ref[rows, scol].astype(BF16)
            yoff_ref[rows, cols] = _dot_nt(cg, h0.astype(BF16))
            lhs = jnp.concatenate([xd_ref[rows, cols], daug_ref[rows, cols]], axis=0).astype(BF16)
            rhs = jnp.concatenate([jnp.concatenate([bm_ref[rows, scol], zeros8], axis=1),
                                   jnp.concatenate([zeros8, ones2], axis=1)], axis=0).astype(BF16)
            upd = _dot_tn(lhs, rhs)
            ssmo_ref[n, cols, :] = upd[:, SSD_STATE:] * h0 + upd[:, :SSD_STATE]
        return carry

    lax.fori_loop(0, nb, per_seq, 0)
    z = _dot(hb, wz_ref[...])
    y = (y + yoff_ref[...] * exp_a_e + dskip_ref[...] * xs) * _silu(z)
    ya_ref[...] = _group_rmsnorm(y, ssdn_ref[...]).astype(BF16)

    a = _gelu_tanh(_dot(hb, wuv_ref[...]))
    u = a[:, :SGU_WIDTH]
    v = a[:, SGU_WIDTH:]
    mu = jnp.mean(v, axis=-1, keepdims=True)
    vc = v - mu
    var = jnp.mean(vc * vc, axis=-1, keepdims=True)
    vn = vc * lax.rsqrt(var + EPS) * lng_ref[...] + lnb_ref[...]
    vo_ref[...] = vn
    s = vn.reshape(nb, seq, SGU_WIDTH) * wd_ref[0] + b8_ref[...]
    for j in range(1, seq):
        s = s + pltpu.roll(vn, j, 0).reshape(nb, seq, SGU_WIDTH) * wd_ref[j]
    yb_ref[...] = (u * s.reshape(r, SGU_WIDTH)).astype(BF16)

    xc = _dot(hb, wpool_ref[...])
    pbuf = pools_in_ref[...]
    t0 = pbuf[:, 0:seq, :].reshape(r, POOL_WIDTH)
    t1 = pbuf[:, seq:2 * seq, :].reshape(r, POOL_WIDTH)
    poolo_ref[:, 0:seq, :] = pbuf[:, seq:2 * seq, :]
    poolo_ref[:, seq:2 * seq, :] = xc.reshape(nb, seq, POOL_WIDTH)
    pos = start + (_row_iota((r, POOL_GW)) % seq)
    yc_cols = []
    for g, w in enumerate(POOL_WINDOWS):
        cols = slice(g * POOL_GW, (g + 1) * POOL_GW)
        tiles = [t0[:, cols], t1[:, cols], xc[:, cols]]
        span = 1
        while span < min(w, seq):
            prev = [None] + tiles[:-1]
            tiles = [tl + shift_rows(tl, pv, span, POOL_GW) for tl, pv in zip(tiles, prev)]
            span *= 2
        wsum = tiles[2] if w <= seq else tiles[2] + tiles[1]
        cnt = jnp.minimum(pos + 1, w).astype(F32)
        d = wsum / cnt - xc[:, cols]
        yc_cols.append(_dot(d.astype(BF16), poolw_ref[g]))
    yc_ref[...] = (jnp.concatenate(yc_cols, axis=1) * pools_ref[...]).astype(BF16)


def _branch_sample(x, conv_pad, ssm_flat, pool_pad, sw, layer, prev_states, start):
    n, seq, _ = x.shape
    depth = sw['w_z'].shape[0]
    nb = SAMPLE_SEQS
    assert n % nb == 0 and seq == SUBLANES and start >= max(POOL_WINDOWS)
    r = nb * seq
    stacked = [sw[k] for k in BRANCH_WEIGHTS + ('w_diag8', 'b8', 'pool_w', 'pool_scale')]
    shared = [sw['e2']]
    seq3 = lambda rows, width: pl.BlockSpec((nb, rows, width), lambda i: (i, 0, 0))
    seq4 = lambda rows, width: pl.BlockSpec((None, nb, rows, width), lambda i: (layer, i, 0, 0))
    tok = lambda width: pl.BlockSpec((r, width), lambda i: (i, 0))
    out_shape = (jax.ShapeDtypeStruct((n * seq, SSD_INNER), BF16),
                 jax.ShapeDtypeStruct((n * seq, SGU_WIDTH), BF16),
                 jax.ShapeDtypeStruct((n * seq, POOL_WIDTH), BF16),
                 jax.ShapeDtypeStruct((depth, n, seq, SSD_CONV_DIM), F32),
                 jax.ShapeDtypeStruct((depth, n, SSD_INNER, SSD_STATE), F32),
                 jax.ShapeDtypeStruct((depth, n, 2 * seq, POOL_WIDTH), F32),
                 jax.ShapeDtypeStruct((depth, n * seq, SGU_WIDTH), F32))
    out_specs = (tok(SSD_INNER), tok(SGU_WIDTH), tok(POOL_WIDTH), seq4(seq, SSD_CONV_DIM),
                 seq4(SSD_INNER, SSD_STATE), seq4(2 * seq, POOL_WIDTH),
                 pl.BlockSpec((None, r, SGU_WIDTH), lambda i: (layer, i, 0)))
    scratch = [pltpu.VMEM((r, SSD_GROUPS * SSD_STATE), F32),
               pltpu.VMEM((r, SSD_GROUPS * SSD_STATE), F32),
               pltpu.VMEM((r, SSD_INNER), F32),
               pltpu.VMEM((r, SSD_INNER), F32),
               pltpu.VMEM((r, SSD_INNER), F32)]
    body = functools.partial(_branch_sample_kernel, nb=nb, seq=seq, start=start)
    in_specs = ([seq3(seq, D_MODEL), seq4(seq, SSD_CONV_DIM), seq4(SSD_INNER, SSD_STATE), seq4(2 * seq, POOL_WIDTH)]
                + [_layer_spec(w, layer) for w in stacked] + [_const_spec(w) for w in shared])
    args = [x, conv_pad, ssm_flat, pool_pad] + stacked + shared
    aliases = {}
    if prev_states is not None:
        body = _skip_refs(body, len(args), N_STATE_OUTPUTS)
        aliases = {len(args) + k: 3 + k for k in range(N_STATE_OUTPUTS)}
        in_specs = in_specs + [pl.BlockSpec(memory_space=pl.ANY)] * N_STATE_OUTPUTS
        args = args + list(prev_states)
    ya, yb, yc, *states = pl.pallas_call(
        body, grid=(n // nb,), in_specs=in_specs,
        out_specs=out_specs, out_shape=out_shape, scratch_shapes=scratch,
        input_output_aliases=aliases,
        compiler_params=pltpu.CompilerParams(dimension_semantics=("arbitrary",),
                                             vmem_limit_bytes=VMEM_LIMIT),
        name="branch_sample",
    )(*args)
    return ya, yb, yc, states


def _merge_kernel(x_ref, ya_ref, yb_ref, yc_ref, nm_ref, wg_ref, wa_ref, wb_ref, wc_ref, wo_ref, o_ref):
    x = x_ref[...]
    hb = _rmsnorm(x, nm_ref[...]).astype(BF16)
    m = _sigmoid(_dot(hb, wg_ref[:, 0:D_MODEL])) * _dot(ya_ref[...], wa_ref[...])
    m = m + _sigmoid(_dot(hb, wg_ref[:, D_MODEL:2 * D_MODEL])) * _dot(yb_ref[...], wb_ref[...])
    m = m + _sigmoid(_dot(hb, wg_ref[:, 2 * D_MODEL:3 * D_MODEL])) * _dot(yc_ref[...], wc_ref[...])
    o_ref[...] = x + _dot(m.astype(BF16), wo_ref[...])


def _merge(x, ya, yb, yc, sw, layer):
    t = x.shape[0]
    tm = min(TOKEN_BLOCK, t)
    assert t % tm == 0
    stacked = [sw[k] for k in ('norm_mix', 'w_gate', 'w_br_a', 'w_br_b', 'w_br_c', 'w_out')]
    tok = lambda width: pl.BlockSpec((tm, width), lambda i: (i, 0))
    return pl.pallas_call(
        _merge_kernel, grid=(t // tm,),
        in_specs=[tok(D_MODEL), tok(SSD_INNER), tok(SGU_WIDTH), tok(POOL_WIDTH)] + [_layer_spec(w, layer) for w in stacked],
        out_specs=tok(D_MODEL), out_shape=jax.ShapeDtypeStruct((t, D_MODEL), F32),
        compiler_params=pltpu.CompilerParams(dimension_semantics=("arbitrary",), vmem_limit_bytes=VMEM_LIMIT),
        name="merge",
    )(x, ya, yb, yc, *stacked)


def _ffn_kernel(x_ref, p_ref, nf_ref, wg_ref, wu_ref, wd_ref, np_ref, wpg_ref, wpu_ref, fn_ref, o_ref, *, final):
    x = x_ref[...]
    hb = _rmsnorm(x, nf_ref[...]).astype(BF16)
    acc = x
    for cols in (slice(0, FF_SPLIT), slice(FF_SPLIT, D_FF)):
        act = (_silu(_dot(hb, wg_ref[:, cols])) * _dot(hb, wu_ref[:, cols])).astype(BF16)
        acc = acc + _dot(act, wd_ref[cols, :])
    hb = _rmsnorm(acc, np_ref[...]).astype(BF16)
    out = acc + _dot(p_ref[...].astype(BF16), wpu_ref[...]) * _sigmoid(_dot(hb, wpg_ref[...]))
    if final:
        out = _rmsnorm(out, fn_ref[...])
    o_ref[...] = out


def _ffn(x, p, sw, layer, final):
    t = x.shape[0]
    tm = min(TOKEN_BLOCK, t)
    assert t % tm == 0
    stacked = [sw[k] for k in ('norm_ffn', 'w_ffn_gate', 'w_ffn_up', 'w_down', 'norm_ple', 'w_ple_gate', 'w_ple_up')]
    tok = lambda width: pl.BlockSpec((tm, width), lambda i: (i, 0))
    return pl.pallas_call(
        functools.partial(_ffn_kernel, final=final), grid=(t // tm,),
        in_specs=([tok(D_MODEL), pl.BlockSpec((None, tm, PLE_DIM), lambda i: (layer, i, 0))]
                  + [_layer_spec(w, layer) for w in stacked] + [_const_spec(sw['final_norm'])]),
        out_specs=tok(D_MODEL), out_shape=jax.ShapeDtypeStruct((t, D_MODEL), F32),
        compiler_params=pltpu.CompilerParams(dimension_semantics=("arbitrary",), vmem_limit_bytes=VMEM_LIMIT),
        name="ffn",
    )(x, p, *stacked, sw['final_norm'])


def _head_expansion():
    rows = jnp.arange(2 * LANES)[:, None] % LANES
    cols = jnp.arange(SSD_INNER)[None, :] // SSD_HEAD_DIM
    e2 = (rows == cols).astype(BF16)
    return e2, e2.T


def _prepare_weights(norm_mix, w_in, conv_w, conv_b, dt_bias, a_log, d_skip, ssd_norm, sgu_ln_g, sgu_ln_b,
                     w_spatial, b_spatial, pool_w, pool_scale, w_br_a, w_br_b, w_br_c, w_out, norm_ffn,
                     w_gate_up, w_down, norm_ple, w_ple_gate, w_ple_up, final_norm):
    depth = w_in.shape[0]
    row = lambda v: v.reshape(depth, 1, -1).astype(F32)
    pad_heads = lambda v: jnp.pad(v.reshape(depth, 1, -1).astype(F32), ((0, 0), (0, 0), (0, LANES - SSD_HEADS)))
    seq = SUBLANES
    w8 = w_spatial[:, :, :seq, :seq]
    tt = jnp.arange(seq)
    lag = tt[None, :] - tt[:, None]
    w_lag = jnp.where(lag >= 0, w8[:, :, tt[None, :], jnp.clip(lag, 0, seq - 1)], 0.0)
    w_diag8 = jnp.repeat(jnp.transpose(w_lag, (0, 2, 3, 1)), SGU_GW, axis=3).astype(F32)
    b_full = jnp.repeat(jnp.transpose(b_spatial, (0, 2, 1)), SGU_GW, axis=2).astype(F32)
    e2, et2 = _head_expansion()
    return {
        'norm_mix': row(norm_mix),
        'w_z': w_in[:, :, O_Z:O_XBC].astype(BF16),
        'w_xbc': w_in[:, :, O_XBC:O_DT].astype(BF16),
        'w_dt': jnp.pad(w_in[:, :, O_DT:O_UV], ((0, 0), (0, 0), (0, LANES - SSD_HEADS))).astype(BF16),
        'w_uv': w_in[:, :, O_UV:O_POOL].astype(BF16),
        'w_pool': w_in[:, :, O_POOL:O_GATE].astype(BF16),
        'w_gate': w_in[:, :, O_GATE:].astype(BF16),
        'conv_w': conv_w.astype(F32), 'conv_b': row(conv_b),
        'dt_bias': pad_heads(dt_bias), 'a_log': pad_heads(a_log),
        'd_skip': row(jnp.repeat(d_skip, SSD_HEAD_DIM, axis=1)),
        'ssd_norm': row(ssd_norm),
        'ln_g': row(sgu_ln_g), 'ln_b': row(sgu_ln_b),
        'w_sp': w_spatial.astype(F32), 'b_sp': b_full,
        'w_diag8': w_diag8, 'b8': b_full[:, :seq],
        'pool_w': pool_w.astype(BF16), 'pool_scale': row(pool_scale),
        'w_br_a': w_br_a.astype(BF16), 'w_br_b': w_br_b.astype(BF16), 'w_br_c': w_br_c.astype(BF16),
        'w_out': w_out.astype(BF16),
        'norm_ffn': row(norm_ffn),
        'w_ffn_gate': w_gate_up[:, :, :D_FF].astype(BF16), 'w_ffn_up': w_gate_up[:, :, D_FF:].astype(BF16),
        'w_down': w_down.astype(BF16),
        'norm_ple': row(norm_ple),
        'w_ple_gate': w_ple_gate.astype(BF16), 'w_ple_up': w_ple_up.astype(BF16),
        'final_norm': final_norm.reshape(1, -1).astype(F32),
        'e2': e2, 'et2': et2,
    }


def kernel(x_prompt, x_sample, state_conv, state_ssm, state_pool, p_prompt, p_sample, norm_mix, w_in, conv_w, conv_b, dt_bias, a_log, d_skip, ssd_norm, sgu_ln_g, sgu_ln_b, w_spatial, b_spatial, pool_w, pool_scale, w_br_a, w_br_b, w_br_c, w_out, norm_ffn, w_gate_up, w_down, norm_ple, w_ple_gate, w_ple_up, final_norm):
    b, l, _ = x_prompt.shape
    n, s, _ = x_sample.shape
    depth = w_in.shape[0]
    sw = _prepare_weights(norm_mix, w_in, conv_w, conv_b, dt_bias, a_log, d_skip, ssd_norm, sgu_ln_g, sgu_ln_b,
                          w_spatial, b_spatial, pool_w, pool_scale, w_br_a, w_br_b, w_br_c, w_out, norm_ffn,
                          w_gate_up, w_down, norm_ple, w_ple_gate, w_ple_up, final_norm)
    conv_pad = jnp.pad(state_conv, ((0, 0), (0, 0), (SUBLANES - (SSD_CONV - 1), 0), (0, 0)))
    pool_pad = jnp.pad(state_pool, ((0, 0), (0, 0), (2 * SUBLANES - POOL_BUF, 0), (0, 0)))
    ssm_flat = state_ssm.reshape(depth, n, SSD_INNER, SSD_STATE)
    pp = p_prompt.reshape(depth, b * l, PLE_DIM)
    ps = p_sample.reshape(depth, n * s, PLE_DIM)
    xp = x_prompt
    xs = x_sample.reshape(n * s, D_MODEL)
    st_p = None
    st_s = None
    for i in range(depth):
        final = i == depth - 1
        ya, yb, yc, st_p = _branch_prompt(xp, sw, i, st_p)
        x1 = _merge(xp.reshape(b * l, D_MODEL), ya.reshape(b * l, -1), yb.reshape(b * l, -1), yc.reshape(b * l, -1), sw, i)
        xp = _ffn(x1, pp, sw, i, final).reshape(b, l, D_MODEL)
        ya, yb, yc, st_s = _branch_sample(xs.reshape(n, s, D_MODEL), conv_pad, ssm_flat, pool_pad, sw, i, st_s, PAST_LEN)
        x1 = _merge(xs, ya, yb, yc, sw, i)
        xs = _ffn(x1, ps, sw, i, final)
    conv_p, ssm_p, pool_p, v_p = st_p
    conv_s, ssm_s, pool_s, v_s = st_s
    return (xp, xs.reshape(n, s, D_MODEL),
            conv_p[:, :, SUBLANES - (SSD_CONV - 1):, :],
            ssm_p.reshape(depth, b, SSD_HEADS, SSD_HEAD_DIM, SSD_STATE),
            pool_p[:, :, 2 * SUBLANES - POOL_BUF:, :],
            v_p,
            conv_s[:, :, s - (SSD_CONV - 1):, :],
            ssm_s.reshape(depth, n, SSD_HEADS, SSD_HEAD_DIM, SSD_STATE),
            pool_s[:, :, 2 * s - POOL_BUF:, :],
            v_s.reshape(depth, n, s, SGU_WIDTH))
```

```python
import functools
import math

import jax
import jax.numpy as jnp
from jax import lax
from jax.experimental import pallas as pl
from jax.experimental.pallas import tpu as pltpu

F32 = jnp.float32
BF16 = jnp.bfloat16

D_MODEL = 1024
DEPTH = 4
PAST_LEN = 16384
SSD_HEAD_DIM = 64
SSD_HEADS = 16
SSD_INNER = 1024
SSD_GROUPS = 2
SSD_STATE = 128
SSD_CONV = 4
SSD_CHUNK = 128
SSD_CONV_DIM = 1536
SGU_WIDTH = 512
SGU_GROUPS = 4
SGU_CHUNK = 128
SGU_GW = 128
POOL_WIDTH = 512
POOL_WINDOWS = (2, 4, 8, 16)
POOL_GW = 128
POOL_BUF = 15
D_FF = 2816
PLE_DIM = 256
EPS = 1e-6
LOG2_E = 1.4426950408889634
O_Z = 0
O_XBC = 1024
O_DT = 2560
O_UV = 2576
O_POOL = 3600
O_GATE = 4112

LANES = 128
SUBLANES = 8
HEADS_PER_GROUP = SSD_HEADS // SSD_GROUPS
GROUP_INNER = SSD_INNER // SSD_GROUPS
VMEM_LIMIT = 56 * 1024 * 1024

PROMPT_BLOCK = 256
SAMPLE_SEQS = 16
TOKEN_BLOCK = 512
MXU_DEPTH = 256
FF_SPLIT = (D_FF // MXU_DEPTH + 1) // 2 * MXU_DEPTH


def _dot(a, b):
    return jnp.dot(a, b, preferred_element_type=F32)


def _dot_nt(a, b):
    return lax.dot_general(a, b, (((1,), (1,)), ((), ())), preferred_element_type=F32)


def _dot_tn(a, b):
    return lax.dot_general(a, b, (((0,), (0,)), ((), ())), preferred_element_type=F32)


def _sigmoid(x):
    return 0.5 * jnp.tanh(0.5 * x) + 0.5


def _silu(x):
    h = 0.5 * x
    return h * jnp.tanh(h) + h


def _gelu_tanh(x):
    c = math.sqrt(2.0 / math.pi)
    return 0.5 * x * (1.0 + jnp.tanh(c * (x + 0.044715 * (x * x * x))))


def _softplus(x):
    return jnp.maximum(x, 0.0) + jnp.log1p(jnp.exp(-jnp.abs(x)))


def _rmsnorm(x, g):
    ms = jnp.mean(x * x, axis=-1, keepdims=True)
    return x * lax.rsqrt(ms + EPS) * g


def _split2(v):
    hi = v.astype(BF16)
    lo = (v - hi.astype(F32)).astype(BF16)
    return jnp.concatenate([hi, lo], axis=-1)


def _expand_heads(v, e2):
    return _dot(_split2(v), e2)


def _group_rmsnorm(y, g):
    parts = []
    for k in range(SSD_GROUPS):
        yk = y[:, k * GROUP_INNER:(k + 1) * GROUP_INNER]
        ms = jnp.mean(yk * yk, axis=-1, keepdims=True)
        parts.append(yk * lax.rsqrt(ms + EPS))
    return jnp.concatenate(parts, axis=-1) * g


def _row_iota(shape):
    return lax.broadcasted_iota(jnp.int32, shape, 0)


def _lane_iota(shape):
    return lax.broadcasted_iota(jnp.int32, shape, 1)


def _ssd_chunk(xs_bf, bm, cm, dtc, a_row, ht_ref):
    q = SSD_CHUNK
    tri = _row_iota((q, q)) >= _lane_iota((q, q))
    tril_bf = jnp.where(tri, 1.0, 0.0).astype(BF16)
    da = dtc * a_row
    hi = da.astype(BF16)
    r1 = da - hi.astype(F32)
    mid = r1.astype(BF16)
    lo = (r1 - mid.astype(F32)).astype(BF16)
    cs3 = _dot(tril_bf, jnp.concatenate([hi, mid, lo], axis=1))
    a_cs = (cs3[:, :LANES] + cs3[:, LANES:2 * LANES] + cs3[:, 2 * LANES:]) * LOG2_E
    a_t = a_cs.T
    ap_t = (a_cs - jnp.log2(dtc)).T
    w_t = jnp.exp2(a_t[:, q - 1:q] - ap_t)
    c_bf = cm.astype(BF16)
    lane = _lane_iota((q, LANES))
    ys = []
    for g in range(SSD_GROUPS):
        scol = slice(g * SSD_STATE, (g + 1) * SSD_STATE)
        cg = c_bf[:, scol]
        cb = _dot_nt(cg, bm[:, scol].astype(BF16))
        bg_t = bm[:, scol].T
        ht_g = ht_ref[:, g * GROUP_INNER:(g + 1) * GROUP_INNER]
        yoff = _dot(cg, ht_g.astype(BF16))
        for pr in range(HEADS_PER_GROUP // 2):
            h0 = g * HEADS_PER_GROUP + 2 * pr
            acols, ms, bws = [], [], []
            for h in (h0, h0 + 1):
                acol = jnp.broadcast_to(a_cs[:, h:h + 1], (q, q))
                acols.append(acol)
                ms.append((jnp.where(tri, jnp.exp2(acol - ap_t[h:h + 1, :]), 0.0) * cb).astype(BF16))
                bws.append((bg_t * w_t[h:h + 1, :]).astype(BF16))
            lhs = jnp.concatenate([jnp.concatenate(ms, axis=1), jnp.concatenate(bws, axis=1)], axis=0)
            xpair = xs_bf[:, h0 * SSD_HEAD_DIM:(h0 + 2) * SSD_HEAD_DIM]
            zero = jnp.zeros_like(xpair)
            rhs = jnp.concatenate([jnp.where(lane < SSD_HEAD_DIM, xpair, zero),
                                   jnp.where(lane >= SSD_HEAD_DIM, xpair, zero)], axis=0)
            out = _dot(lhs, rhs)
            ea = jnp.exp2(jnp.where(lane < SSD_HEAD_DIM, acols[0], acols[1]))
            ys.append(out[:q] + yoff[:, pr * LANES:(pr + 1) * LANES] * ea)
            cols = slice(h0 * SSD_HEAD_DIM, (h0 + 2) * SSD_HEAD_DIM)
            ht_ref[:, cols] = ht_g[:, pr * LANES:(pr + 1) * LANES] * ea[q - 1:q, :] + out[q:]
    return jnp.concatenate(ys, axis=1)


P_XBC = 0
P_DT = P_XBC + SSD_CONV_DIM
P_UV = P_DT + LANES
P_POOL = P_UV + 2 * SGU_WIDTH
P_WIDTH = P_POOL + POOL_WIDTH
PROJ_PIECE = 512
PIECES_AFTER_PHASE = (1, 1, 1, 1, 1, 1, 1)


def _branch_prompt_kernel(xa_ref, xn_ref, nm_ref, wxbc_ref, wdt_ref, wuv_ref, wpool_ref,
                          convw_ref, convb_ref, dtb_ref, alog_ref, dskip_ref,
                          lng_ref, lnb_ref, wsp_ref, bsp_ref, poolw_ref, pools_ref,
                          ya_ref, yb_ref, yc_ref, convo_ref, ssmo_ref, poolo_ref, vo_ref,
                          pa_ref, pb_ref, ht_ref, cc_ref, pc_ref, xcv_ref, *, tb, npair):
    i = pl.program_id(0)
    j = pl.program_id(1)
    nchunk = tb // SSD_CHUNK

    def project_pieces(x, p_ref):
        hb = _rmsnorm(x, nm_ref[...]).astype(BF16)
        pieces = []
        for w_ref, base, width in ((wxbc_ref, P_XBC, SSD_CONV_DIM),
                                   (wuv_ref, P_UV, 2 * SGU_WIDTH), (wpool_ref, P_POOL, POOL_WIDTH),
                                   (wdt_ref, P_DT, LANES)):
            for c0 in range(0, width, PROJ_PIECE):
                c1 = min(c0 + PROJ_PIECE, width)

                def piece(w_ref=w_ref, base=base, c0=c0, c1=c1):
                    p_ref[:, base + c0:base + c1] = _dot(hb, w_ref[:, c0:c1])
                pieces.append(piece)
        return pieces

    def process_phases(p_ref, half):
        orow = slice(half * tb, (half + 1) * tb)
        res = {}
        phases = []

        def conv(cols):
            xp = jnp.concatenate([cc_ref[:, cols], p_ref[:, P_XBC + cols.start:P_XBC + cols.stop]], axis=0)
            acc = xp[SUBLANES:, :] * convw_ref[SSD_CONV - 1:SSD_CONV, cols]
            for k in range(SSD_CONV - 1):
                acc = acc + pltpu.roll(xp, SSD_CONV - 1 - k, 0)[SUBLANES:, :] * convw_ref[k:k + 1, cols]
            xcv_ref[:, cols] = _silu(acc + convb_ref[:, cols])
        for c0 in range(0, SSD_CONV_DIM, PROJ_PIECE):
            phases.append(functools.partial(conv, slice(c0, c0 + PROJ_PIECE)))

        def scan(c):
            rows = slice(c * SSD_CHUNK, (c + 1) * SSD_CHUNK)
            xs = xcv_ref[rows, 0:SSD_INNER]
            y = _ssd_chunk(xs.astype(BF16),
                           xcv_ref[rows, SSD_INNER:SSD_INNER + SSD_GROUPS * SSD_STATE],
                           xcv_ref[rows, SSD_INNER + SSD_GROUPS * SSD_STATE:SSD_CONV_DIM],
                           _softplus(p_ref[rows, P_DT:P_UV] + dtb_ref[...]),
                           -jnp.exp(alog_ref[...]), ht_ref)
            ya_ref[half * tb + c * SSD_CHUNK:half * tb + (c + 1) * SSD_CHUNK, :] = (
                y + dskip_ref[...] * xs).astype(BF16)
            if c == nchunk - 1:
                res['tail'] = p_ref[tb - SUBLANES:tb, P_XBC:P_DT]
                cc_ref[...] = res['tail']
        for c in range(nchunk):
            phases.append(functools.partial(scan, c))

        def spatial():
            u = _gelu_tanh(p_ref[:, P_UV:P_UV + SGU_WIDTH])
            v = _gelu_tanh(p_ref[:, P_UV + SGU_WIDTH:P_POOL])
            mu = jnp.mean(v, axis=-1, keepdims=True)
            vc = v - mu
            var = jnp.mean(vc * vc, axis=-1, keepdims=True)
            vn = vc * lax.rsqrt(var + EPS) * lng_ref[...] + lnb_ref[...]
            res['vn'] = vn
            vn_bf = vn.astype(BF16)
            tri = _row_iota((SGU_CHUNK, SGU_CHUNK)) >= _lane_iota((SGU_CHUNK, SGU_CHUNK))
            s_groups = []
            for g in range(SGU_GROUPS):
                wm = jnp.where(tri, wsp_ref[g], 0.0).astype(BF16)
                vg = jnp.concatenate([vn_bf[c * SGU_CHUNK:(c + 1) * SGU_CHUNK, g * SGU_GW:(g + 1) * SGU_GW]
                                      for c in range(nchunk)], axis=1)
                s_groups.append(_dot(wm, vg))
            s = jnp.concatenate([jnp.concatenate([sg[:, c * SGU_GW:(c + 1) * SGU_GW] for sg in s_groups], axis=1)
                                 + bsp_ref[...] for c in range(nchunk)], axis=0)
            yb_ref[orow, :] = (u * s).astype(BF16)
        phases.append(spatial)

        def pool():
            pos = (2 * j + half) * tb + _row_iota((tb, POOL_GW))
            yc_cols = []
            for g, w in enumerate(POOL_WINDOWS):
                cols = slice(P_POOL + g * POOL_GW, P_POOL + (g + 1) * POOL_GW)
                cur = p_ref[:, cols]
                wsum = jnp.concatenate([pc_ref[:, g * POOL_GW:(g + 1) * POOL_GW], cur], axis=0)
                span = 1
                while span < w:
                    wsum = wsum + pltpu.roll(wsum, span, 0)
                    span *= 2
                cnt = jnp.minimum(pos + 1, w).astype(F32)
                d = wsum[2 * SUBLANES:, :] / cnt - cur
                yc_cols.append(_dot(d.astype(BF16), poolw_ref[g]))
            yc_ref[orow, :] = (jnp.concatenate(yc_cols, axis=1) * pools_ref[...]).astype(BF16)
            res['ptail'] = p_ref[tb - 2 * SUBLANES:tb, P_POOL:P_WIDTH]
            pc_ref[...] = res['ptail']
        phases.append(pool)
        return phases, res

    def run(p_cur, half, x_next, p_next):
        phases, res = process_phases(p_cur, half)
        pieces = project_pieces(x_next, p_next)
        assert len(phases) == len(PIECES_AFTER_PHASE) and len(pieces) == sum(PIECES_AFTER_PHASE)
        done = 0
        for phase, count in zip(phases, PIECES_AFTER_PHASE):
            phase()
            for piece in pieces[done:done + count]:
                piece()
            done += count
        return res

    @pl.when(jnp.logical_and(i == 0, j == 0))
    def _():
        for piece in project_pieces(xa_ref[0:tb, :], pa_ref):
            piece()

    @pl.when(j == 0)
    def _():
        ht_ref[...] = jnp.zeros_like(ht_ref)
        cc_ref[...] = jnp.zeros_like(cc_ref)
        pc_ref[...] = jnp.zeros_like(pc_ref)

    run(pa_ref, 0, xa_ref[tb:2 * tb, :], pb_ref)
    res = run(pb_ref, 1, xn_ref[...], pa_ref)

    @pl.when(j == npair - 1)
    def _():
        convo_ref[...] = res['tail']
        ssmo_ref[...] = ht_ref[...].T
        poolo_ref[...] = res['ptail']
        vo_ref[...] = res['vn'][tb - SGU_CHUNK:tb, :]


def _const_spec(arr):
    nd = arr.ndim
    return pl.BlockSpec(arr.shape, lambda *_: (0,) * nd, pipeline_mode=pl.Buffered(1))


def _layer_spec(arr, layer):
    nd = arr.ndim - 1
    return pl.BlockSpec((None,) + arr.shape[1:], lambda *_: (layer,) + (0,) * nd, pipeline_mode=pl.Buffered(1))


def _skip_refs(body, first, count):
    def wrapped(*refs):
        return body(*refs[:first], *refs[first + count:])
    return wrapped


BRANCH_WEIGHTS = ('norm_mix', 'w_xbc', 'w_dt', 'w_uv', 'w_pool', 'conv_w', 'conv_b', 'dt_bias', 'a_log',
                  'd_skip', 'ln_g', 'ln_b')
N_STATE_OUTPUTS = 4


def _branch_prompt(x, sw, layer, prev_states):
    b, l, _ = x.shape
    depth = sw['w_xbc'].shape[0]
    tb = PROMPT_BLOCK
    npair = l // (2 * tb)
    assert l % (2 * tb) == 0 and tb % SSD_CHUNK == 0 and tb >= 2 * SUBLANES
    stacked = [sw[k] for k in BRANCH_WEIGHTS + ('w_sp', 'b_sp', 'pool_w', 'pool_scale')]
    shared = []
    tok = lambda width: pl.BlockSpec((None, 2 * tb, width), lambda i, j: (i, j, 0))

    def next_block(i, j):
        flat = jnp.minimum(i * npair + j + 1, b * npair - 1)
        return flat // npair, (flat % npair) * 2, 0

    per_seq = lambda rows, width: pl.BlockSpec((None, None, rows, width), lambda i, j: (layer, i, 0, 0))
    out_shape = (jax.ShapeDtypeStruct((b, l, SSD_INNER), BF16),
                 jax.ShapeDtypeStruct((b, l, SGU_WIDTH), BF16),
                 jax.ShapeDtypeStruct((b, l, POOL_WIDTH), BF16),
                 jax.ShapeDtypeStruct((depth, b, SUBLANES, SSD_CONV_DIM), F32),
                 jax.ShapeDtypeStruct((depth, b, SSD_INNER, SSD_STATE), F32),
                 jax.ShapeDtypeStruct((depth, b, 2 * SUBLANES, POOL_WIDTH), F32),
                 jax.ShapeDtypeStruct((depth, b, SGU_CHUNK, SGU_WIDTH), F32))
    out_specs = (tok(SSD_INNER), tok(SGU_WIDTH), tok(POOL_WIDTH),
                 per_seq(SUBLANES, SSD_CONV_DIM), per_seq(SSD_INNER, SSD_STATE),
                 per_seq(2 * SUBLANES, POOL_WIDTH), per_seq(SGU_CHUNK, SGU_WIDTH))
    scratch = [pltpu.VMEM((tb, P_WIDTH), F32),
               pltpu.VMEM((tb, P_WIDTH), F32),
               pltpu.VMEM((SSD_STATE, SSD_INNER), F32),
               pltpu.VMEM((SUBLANES, SSD_CONV_DIM), F32),
               pltpu.VMEM((2 * SUBLANES, POOL_WIDTH), F32),
               pltpu.VMEM((tb, SSD_CONV_DIM), F32)]
    body = functools.partial(_branch_prompt_kernel, tb=tb, npair=npair)
    in_specs = ([tok(D_MODEL), pl.BlockSpec((None, tb, D_MODEL), next_block)]
                + [_layer_spec(w, layer) for w in stacked] + [_const_spec(w) for w in shared])
    args = [x, x] + stacked + shared
    body = _skip_refs(body, len(args), N_STATE_OUTPUTS)
    aliases = {len(args) + k: 3 + k for k in range(N_STATE_OUTPUTS)}
    in_specs = in_specs + [pl.BlockSpec(memory_space=pl.ANY)] * N_STATE_OUTPUTS
    args = args + list(prev_states)
    ya, yb, yc, *states = pl.pallas_call(
        body, grid=(b, npair), in_specs=in_specs,
        out_specs=out_specs, out_shape=out_shape, scratch_shapes=scratch,
        input_output_aliases=aliases,
        compiler_params=pltpu.CompilerParams(dimension_semantics=("arbitrary", "arbitrary"),
                                             vmem_limit_bytes=VMEM_LIMIT),
        name="branch_prompt",
    )(*args)
    return ya, yb, yc, states


def _branch_sample_kernel(x_ref, convs_ref, ssms_ref, pools_in_ref, nm_ref, wxbc_ref, wdt_ref, wuv_ref,
                          wpool_ref, convw_ref, convb_ref, dtb_ref, alog_ref, dskip_ref,
                          lng_ref, lnb_ref, wd_ref, b8_ref, poolw_ref, pools_ref, e2_ref,
                          ya_ref, yb_ref, yc_ref, convo_ref, ssmo_ref, poolo_ref, vo_ref,
                          c_ref, bm_ref, xd_ref, daug_ref, yoff_ref, *, nb, seq, start):
    r = nb * seq
    assert seq == SUBLANES
    t128 = _row_iota((r, LANES)) % seq

    def tile_roll(v, j):
        width = v.shape[-1]
        return pltpu.roll(v.reshape(nb, seq, width), j, 1).reshape(r, width)

    def shift_rows(cur, prev, j, width):
        tt = _row_iota((r, width)) % seq
        if prev is None:
            return jnp.where(tt >= j, tile_roll(cur, j), 0.0)
        return jnp.where(tt >= j, tile_roll(cur, j), tile_roll(prev, j))

    hb = _rmsnorm(x_ref[...].reshape(r, D_MODEL), nm_ref[...]).astype(BF16)

    xbc = _dot(hb, wxbc_ref[...])
    cbuf = convs_ref[...].reshape(r, SSD_CONV_DIM)
    acc = xbc * convw_ref[SSD_CONV - 1:SSD_CONV, :]
    for k in range(SSD_CONV - 1):
        acc = acc + shift_rows(xbc, cbuf, SSD_CONV - 1 - k, SSD_CONV_DIM) * convw_ref[k:k + 1, :]
    xcv = _silu(acc + convb_ref[...])
    convo_ref[...] = xbc.reshape(nb, seq, SSD_CONV_DIM)
    xs = xcv[:, 0:SSD_INNER]
    bm = xcv[:, SSD_INNER:SSD_INNER + SSD_GROUPS * SSD_STATE]
    cm = xcv[:, SSD_INNER + SSD_GROUPS * SSD_STATE:SSD_CONV_DIM]
    dt = _softplus(_dot(hb, wdt_ref[...]) + dtb_ref[...])
    a_row = -jnp.exp(alog_ref[...])
    a_cs = dt * a_row
    for s in (1, 2, 4):
        a_cs = a_cs + jnp.where(t128 >= s, tile_roll(a_cs, s), 0.0)
    a3 = a_cs.reshape(nb, seq, LANES)
    tot = jnp.broadcast_to(a3[:, seq - 1:seq, :], (nb, seq, LANES)).reshape(r, LANES)
    e2 = e2_ref[...]
    exp_a_e = _expand_heads(jnp.exp(a_cs), e2)
    decst_e = _expand_heads(jnp.exp(tot - a_cs), e2)
    dt_e = _expand_heads(dt, e2)
    dectot_e = _expand_heads(jnp.exp(tot), e2)
    xdt = xs * dt_e
    lane = _lane_iota((r, LANES))
    y = jnp.zeros((r, SSD_INNER), F32)
    for j in range(seq):
        if j == 0:
            lj = jnp.ones((r, LANES), F32)
            b_sh, x_sh = bm, xdt
        else:
            lj = jnp.where(t128 >= j, jnp.exp(a_cs - tile_roll(a_cs, j)), 0.0)
            b_sh, x_sh = tile_roll(bm, j), tile_roll(xdt, j)
        prod = cm * b_sh
        cb0 = jnp.sum(prod[:, 0:SSD_STATE], axis=-1, keepdims=True)
        cb1 = jnp.sum(prod[:, SSD_STATE:2 * SSD_STATE], axis=-1, keepdims=True)
        mj = lj * jnp.where(lane < HEADS_PER_GROUP, cb0, cb1)
        y = y + _expand_heads(mj, e2) * x_sh
    c_ref[...] = cm
    bm_ref[...] = bm
    xd_ref[...] = xdt * decst_e
    dec_hi = dectot_e.astype(BF16).astype(F32)
    t1024 = _row_iota((r, SSD_INNER)) % seq
    daug_ref[...] = jnp.where(t1024 == 0, dec_hi, jnp.where(t1024 == 1, dectot_e - dec_hi, 0.0))
    t8 = _row_iota((seq, LANES))
    ones2 = jnp.where(t8 < 2, 1.0, 0.0)
    zeros8 = jnp.zeros((seq, LANES), F32)

    def per_seq(n, carry):
        rows = pl.ds(pl.multiple_of(n * seq, seq), seq)
        for g in range(SSD_GROUPS):
            cols = slice(g * GROUP_INNER, (g + 1) * GROUP_INNER)
            scol = slice(g * SSD_STATE, (g + 1) * SSD_STATE)
            h0 = ssms_ref[n, cols, :]
            cg = c_ref[rows, scol].astype(BF16)
            yoff_ref[rows, cols] = _dot_nt(cg, h0.astype(BF16))
            lhs = jnp.concatenate([xd_ref[rows, cols], daug_ref[rows, cols]], axis=0).astype(BF16)
            rhs = jnp.concatenate([jnp.concatenate([bm_ref[rows, scol], zeros8], axis=1),
                                   jnp.concatenate([zeros8, ones2], axis=1)], axis=0).astype(BF16)
            upd = _dot_tn(lhs, rhs)
            ssmo_ref[n, cols, :] = upd[:, SSD_STATE:] * h0 + upd[:, :SSD_STATE]
        return carry

    lax.fori_loop(0, nb, per_seq, 0)
    ya_ref[...] = (y + yoff_ref[...] * exp_a_e + dskip_ref[...] * xs).astype(BF16)

    a = _gelu_tanh(_dot(hb, wuv_ref[...]))
    u = a[:, :SGU_WIDTH]
    v = a[:, SGU_WIDTH:]
    mu = jnp.mean(v, axis=-1, keepdims=True)
    vc = v - mu
    var = jnp.mean(vc * vc, axis=-1, keepdims=True)
    vn = vc * lax.rsqrt(var + EPS) * lng_ref[...] + lnb_ref[...]
    vo_ref[...] = vn
    s = vn.reshape(nb, seq, SGU_WIDTH) * wd_ref[0] + b8_ref[...]
    for j in range(1, seq):
        s = s + tile_roll(vn, j).reshape(nb, seq, SGU_WIDTH) * wd_ref[j]
    yb_ref[...] = (u * s.reshape(r, SGU_WIDTH)).astype(BF16)

    xc = _dot(hb, wpool_ref[...])
    pbuf = pools_in_ref[...]
    t0 = pbuf[:, 0:seq, :].reshape(r, POOL_WIDTH)
    t1 = pbuf[:, seq:2 * seq, :].reshape(r, POOL_WIDTH)
    poolo_ref[:, 0:seq, :] = pbuf[:, seq:2 * seq, :]
    poolo_ref[:, seq:2 * seq, :] = xc.reshape(nb, seq, POOL_WIDTH)
    pos = start + (_row_iota((r, POOL_GW)) % seq)
    yc_cols = []
    for g, w in enumerate(POOL_WINDOWS):
        cols = slice(g * POOL_GW, (g + 1) * POOL_GW)
        tiles = [t0[:, cols], t1[:, cols], xc[:, cols]]
        span = 1
        while span < min(w, seq):
            prev = [None] + tiles[:-1]
            tiles = [tl + shift_rows(tl, pv, span, POOL_GW) for tl, pv in zip(tiles, prev)]
            span *= 2
        wsum = tiles[2] if w <= seq else tiles[2] + tiles[1]
        cnt = jnp.minimum(pos + 1, w).astype(F32)
        d = wsum / cnt - xc[:, cols]
        yc_cols.append(_dot(d.astype(BF16), poolw_ref[g]))
    yc_ref[...] = (jnp.concatenate(yc_cols, axis=1) * pools_ref[...]).astype(BF16)


def _branch_sample(x, conv_pad, ssm_flat, pool_pad, sw, layer, prev_states, start):
    n, seq, _ = x.shape
    depth = sw['w_xbc'].shape[0]
    nb = SAMPLE_SEQS
    assert n % nb == 0 and seq == SUBLANES and start >= max(POOL_WINDOWS)
    r = nb * seq
    stacked = [sw[k] for k in BRANCH_WEIGHTS + ('w_diag8', 'b8', 'pool_w', 'pool_scale')]
    shared = [sw['e2']]
    seq3 = lambda rows, width: pl.BlockSpec((nb, rows, width), lambda i: (i, 0, 0))
    seq4 = lambda rows, width: pl.BlockSpec((None, nb, rows, width), lambda i: (layer, i, 0, 0))
    tok = lambda width: pl.BlockSpec((r, width), lambda i: (i, 0))
    out_shape = (jax.ShapeDtypeStruct((n * seq, SSD_INNER), BF16),
                 jax.ShapeDtypeStruct((n * seq, SGU_WIDTH), BF16),
                 jax.ShapeDtypeStruct((n * seq, POOL_WIDTH), BF16),
                 jax.ShapeDtypeStruct((depth, n, seq, SSD_CONV_DIM), F32),
                 jax.ShapeDtypeStruct((depth, n, SSD_INNER, SSD_STATE), F32),
                 jax.ShapeDtypeStruct((depth, n, 2 * seq, POOL_WIDTH), F32),
                 jax.ShapeDtypeStruct((depth, n * seq, SGU_WIDTH), F32))
    out_specs = (tok(SSD_INNER), tok(SGU_WIDTH), tok(POOL_WIDTH), seq4(seq, SSD_CONV_DIM),
                 seq4(SSD_INNER, SSD_STATE), seq4(2 * seq, POOL_WIDTH),
                 pl.BlockSpec((None, r, SGU_WIDTH), lambda i: (layer, i, 0)))
    scratch = [pltpu.VMEM((r, SSD_GROUPS * SSD_STATE), F32),
               pltpu.VMEM((r, SSD_GROUPS * SSD_STATE), F32),
               pltpu.VMEM((r, SSD_INNER), F32),
               pltpu.VMEM((r, SSD_INNER), F32),
               pltpu.VMEM((r, SSD_INNER), F32)]
    body = functools.partial(_branch_sample_kernel, nb=nb, seq=seq, start=start)
    in_specs = ([seq3(seq, D_MODEL), seq4(seq, SSD_CONV_DIM), seq4(SSD_INNER, SSD_STATE), seq4(2 * seq, POOL_WIDTH)]
                + [_layer_spec(w, layer) for w in stacked] + [_const_spec(w) for w in shared])
    args = [x, conv_pad, ssm_flat, pool_pad] + stacked + shared
    body = _skip_refs(body, len(args), N_STATE_OUTPUTS)
    aliases = {len(args) + k: 3 + k for k in range(N_STATE_OUTPUTS)}
    in_specs = in_specs + [pl.BlockSpec(memory_space=pl.ANY)] * N_STATE_OUTPUTS
    args = args + list(prev_states)
    ya, yb, yc, *states = pl.pallas_call(
        body, grid=(n // nb,), in_specs=in_specs,
        out_specs=out_specs, out_shape=out_shape, scratch_shapes=scratch,
        input_output_aliases=aliases,
        compiler_params=pltpu.CompilerParams(dimension_semantics=("arbitrary",),
                                             vmem_limit_bytes=VMEM_LIMIT),
        name="branch_sample",
    )(*args)
    return ya, yb, yc, states


def _merge_kernel(x_ref, ya_ref, yb_ref, yc_ref, nm_ref, wz_ref, ssdn_ref, wg_ref, wa_ref, wb_ref, wc_ref, wo_ref,
                  o_ref):
    x = x_ref[...]
    hb = _rmsnorm(x, nm_ref[...]).astype(BF16)
    ya = _group_rmsnorm(ya_ref[...].astype(F32) * _silu(_dot(hb, wz_ref[...])), ssdn_ref[...]).astype(BF16)
    m = _sigmoid(_dot(hb, wg_ref[:, 0:D_MODEL])) * _dot(ya, wa_ref[...])
    m = m + _sigmoid(_dot(hb, wg_ref[:, D_MODEL:2 * D_MODEL])) * _dot(yb_ref[...], wb_ref[...])
    m = m + _sigmoid(_dot(hb, wg_ref[:, 2 * D_MODEL:3 * D_MODEL])) * _dot(yc_ref[...], wc_ref[...])
    o_ref[...] = x + _dot(m.astype(BF16), wo_ref[...])


def _merge(x, ya, yb, yc, sw, layer):
    t = x.shape[0]
    tm = min(TOKEN_BLOCK, t)
    assert t % tm == 0
    stacked = [sw[k] for k in ('norm_mix', 'w_z', 'ssd_norm', 'w_gate', 'w_br_a', 'w_br_b', 'w_br_c', 'w_out')]
    tok = lambda width: pl.BlockSpec((tm, width), lambda i: (i, 0))
    return pl.pallas_call(
        _merge_kernel, grid=(t // tm,),
        in_specs=[tok(D_MODEL), tok(SSD_INNER), tok(SGU_WIDTH), tok(POOL_WIDTH)] + [_layer_spec(w, layer) for w in stacked],
        out_specs=tok(D_MODEL), out_shape=jax.ShapeDtypeStruct((t, D_MODEL), F32),
        compiler_params=pltpu.CompilerParams(dimension_semantics=("arbitrary",), vmem_limit_bytes=VMEM_LIMIT),
        name="merge",
    )(x, ya, yb, yc, *stacked)


def _ffn_kernel(x_ref, p_ref, nf_ref, wg_ref, wu_ref, wd_ref, np_ref, wpg_ref, wpu_ref, fn_ref, o_ref, *, final):
    x = x_ref[...]
    hb = _rmsnorm(x, nf_ref[...]).astype(BF16)
    acc = x
    for cols in (slice(0, FF_SPLIT), slice(FF_SPLIT, D_FF)):
        act = (_silu(_dot(hb, wg_ref[:, cols])) * _dot(hb, wu_ref[:, cols])).astype(BF16)
        acc = acc + _dot(act, wd_ref[cols, :])
    hb = _rmsnorm(acc, np_ref[...]).astype(BF16)
    out = acc + _dot(p_ref[...].astype(BF16), wpu_ref[...]) * _sigmoid(_dot(hb, wpg_ref[...]))
    if final:
        out = _rmsnorm(out, fn_ref[...])
    o_ref[...] = out


def _ffn(x, p, sw, layer, final):
    t = x.shape[0]
    tm = min(TOKEN_BLOCK, t)
    assert t % tm == 0
    stacked = [sw[k] for k in ('norm_ffn', 'w_ffn_gate', 'w_ffn_up', 'w_down', 'norm_ple', 'w_ple_gate', 'w_ple_up')]
    tok = lambda width: pl.BlockSpec((tm, width), lambda i: (i, 0))
    return pl.pallas_call(
        functools.partial(_ffn_kernel, final=final), grid=(t // tm,),
        in_specs=([tok(D_MODEL), pl.BlockSpec((None, tm, PLE_DIM), lambda i: (layer, i, 0))]
                  + [_layer_spec(w, layer) for w in stacked] + [_const_spec(sw['final_norm'])]),
        out_specs=tok(D_MODEL), out_shape=jax.ShapeDtypeStruct((t, D_MODEL), F32),
        compiler_params=pltpu.CompilerParams(dimension_semantics=("arbitrary",), vmem_limit_bytes=VMEM_LIMIT),
        name="ffn",
    )(x, p, *stacked, sw['final_norm'])


def _head_expansion():
    rows = jnp.arange(2 * LANES)[:, None] % LANES
    cols = jnp.arange(SSD_INNER)[None, :] // SSD_HEAD_DIM
    e2 = (rows == cols).astype(BF16)
    return e2, e2.T


def _prepare_weights(norm_mix, w_in, conv_w, conv_b, dt_bias, a_log, d_skip, ssd_norm, sgu_ln_g, sgu_ln_b,
                     w_spatial, b_spatial, pool_w, pool_scale, w_br_a, w_br_b, w_br_c, w_out, norm_ffn,
                     w_gate_up, w_down, norm_ple, w_ple_gate, w_ple_up, final_norm):
    depth = w_in.shape[0]
    row = lambda v: v.reshape(depth, 1, -1).astype(F32)
    pad_heads = lambda v: jnp.pad(v.reshape(depth, 1, -1).astype(F32), ((0, 0), (0, 0), (0, LANES - SSD_HEADS)))
    seq = SUBLANES
    w8 = w_spatial[:, :, :seq, :seq]
    tt = jnp.arange(seq)
    lag = tt[None, :] - tt[:, None]
    w_lag = jnp.where(lag >= 0, w8[:, :, tt[None, :], jnp.clip(lag, 0, seq - 1)], 0.0)
    w_diag8 = jnp.repeat(jnp.transpose(w_lag, (0, 2, 3, 1)), SGU_GW, axis=3).astype(F32)
    b_full = jnp.repeat(jnp.transpose(b_spatial, (0, 2, 1)), SGU_GW, axis=2).astype(F32)
    e2, et2 = _head_expansion()
    return {
        'norm_mix': row(norm_mix),
        'w_z': w_in[:, :, O_Z:O_XBC].astype(BF16),
        'w_xbc': w_in[:, :, O_XBC:O_DT].astype(BF16),
        'w_dt': jnp.pad(w_in[:, :, O_DT:O_UV], ((0, 0), (0, 0), (0, LANES - SSD_HEADS))).astype(BF16),
        'w_uv': w_in[:, :, O_UV:O_POOL].astype(BF16),
        'w_pool': w_in[:, :, O_POOL:O_GATE].astype(BF16),
        'w_gate': w_in[:, :, O_GATE:].astype(BF16),
        'conv_w': conv_w.astype(F32), 'conv_b': row(conv_b),
        'dt_bias': pad_heads(dt_bias), 'a_log': pad_heads(a_log),
        'd_skip': row(jnp.repeat(d_skip, SSD_HEAD_DIM, axis=1)),
        'ssd_norm': row(ssd_norm),
        'ln_g': row(sgu_ln_g), 'ln_b': row(sgu_ln_b),
        'w_sp': w_spatial.astype(F32), 'b_sp': b_full,
        'w_diag8': w_diag8, 'b8': b_full[:, :seq],
        'pool_w': pool_w.astype(BF16), 'pool_scale': row(pool_scale),
        'w_br_a': w_br_a.astype(BF16), 'w_br_b': w_br_b.astype(BF16), 'w_br_c': w_br_c.astype(BF16),
        'w_out': w_out.astype(BF16),
        'norm_ffn': row(norm_ffn),
        'w_ffn_gate': w_gate_up[:, :, :D_FF].astype(BF16), 'w_ffn_up': w_gate_up[:, :, D_FF:].astype(BF16),
        'w_down': w_down.astype(BF16),
        'norm_ple': row(norm_ple),
        'w_ple_gate': w_ple_gate.astype(BF16), 'w_ple_up': w_ple_up.astype(BF16),
        'final_norm': final_norm.reshape(1, -1).astype(F32),
        'e2': e2, 'et2': et2,
    }


def kernel(x_prompt, x_sample, state_conv, state_ssm, state_pool, p_prompt, p_sample, norm_mix, w_in, conv_w, conv_b, dt_bias, a_log, d_skip, ssd_norm, sgu_ln_g, sgu_ln_b, w_spatial, b_spatial, pool_w, pool_scale, w_br_a, w_br_b, w_br_c, w_out, norm_ffn, w_gate_up, w_down, norm_ple, w_ple_gate, w_ple_up, final_norm):
    b, l, _ = x_prompt.shape
    n, s, _ = x_sample.shape
    depth = w_in.shape[0]
    sw = _prepare_weights(norm_mix, w_in, conv_w, conv_b, dt_bias, a_log, d_skip, ssd_norm, sgu_ln_g, sgu_ln_b,
                          w_spatial, b_spatial, pool_w, pool_scale, w_br_a, w_br_b, w_br_c, w_out, norm_ffn,
                          w_gate_up, w_down, norm_ple, w_ple_gate, w_ple_up, final_norm)
    conv_pad = jnp.pad(state_conv, ((0, 0), (0, 0), (SUBLANES - (SSD_CONV - 1), 0), (0, 0)))
    pool_pad = jnp.pad(state_pool, ((0, 0), (0, 0), (2 * SUBLANES - POOL_BUF, 0), (0, 0)))
    ssm_flat = state_ssm.reshape(depth, n, SSD_INNER, SSD_STATE)
    pp = p_prompt.reshape(depth, b * l, PLE_DIM)
    ps = p_sample.reshape(depth, n * s, PLE_DIM)
    xp = x_prompt
    xs = x_sample.reshape(n * s, D_MODEL)
    st_p = tuple(lax.empty(shape, F32) for shape in (
        (depth, b, SUBLANES, SSD_CONV_DIM), (depth, b, SSD_INNER, SSD_STATE),
        (depth, b, 2 * SUBLANES, POOL_WIDTH), (depth, b, SGU_CHUNK, SGU_WIDTH)))
    st_s = tuple(lax.empty(shape, F32) for shape in (
        (depth, n, s, SSD_CONV_DIM), (depth, n, SSD_INNER, SSD_STATE),
        (depth, n, 2 * s, POOL_WIDTH), (depth, n * s, SGU_WIDTH)))
    for i in range(depth):
        final = i == depth - 1
        ya, yb, yc, st_p = _branch_prompt(xp, sw, i, st_p)
        x1 = _merge(xp.reshape(b * l, D_MODEL), ya.reshape(b * l, -1), yb.reshape(b * l, -1), yc.reshape(b * l, -1), sw, i)
        xp = _ffn(x1, pp, sw, i, final).reshape(b, l, D_MODEL)
        ya, yb, yc, st_s = _branch_sample(xs.reshape(n, s, D_MODEL), conv_pad, ssm_flat, pool_pad, sw, i, st_s, PAST_LEN)
        x1 = _merge(xs, ya, yb, yc, sw, i)
        xs = _ffn(x1, ps, sw, i, final)
    conv_p, ssm_p, pool_p, v_p = st_p
    conv_s, ssm_s, pool_s, v_s = st_s
    return (xp, xs.reshape(n, s, D_MODEL),
            conv_p[:, :, SUBLANES - (SSD_CONV - 1):, :],
            ssm_p.reshape(depth, b, SSD_HEADS, SSD_HEAD_DIM, SSD_STATE),
            pool_p[:, :, 2 * SUBLANES - POOL_BUF:, :],
            v_p,
            conv_s[:, :, s - (SSD_CONV - 1):, :],
            ssm_s.reshape(depth, n, SSD_HEADS, SSD_HEAD_DIM, SSD_STATE),
            pool_s[:, :, 2 * s - POOL_BUF:, :],
            v_s.reshape(depth, n, s, SGU_WIDTH))
```

```python
import functools
import math

import jax
import jax.numpy as jnp
from jax import lax
from jax.experimental import pallas as pl
from jax.experimental.pallas import tpu as pltpu

F32 = jnp.float32
BF16 = jnp.bfloat16

D_MODEL = 1024
DEPTH = 4
PAST_LEN = 16384
SSD_HEAD_DIM = 64
SSD_HEADS = 16
SSD_INNER = 1024
SSD_GROUPS = 2
SSD_STATE = 128
SSD_CONV = 4
SSD_CHUNK = 128
SSD_CONV_DIM = 1536
SGU_WIDTH = 512
SGU_GROUPS = 4
SGU_CHUNK = 128
SGU_GW = 128
POOL_WIDTH = 512
POOL_WINDOWS = (2, 4, 8, 16)
POOL_GW = 128
POOL_BUF = 15
D_FF = 2816
PLE_DIM = 256
EPS = 1e-6
LOG2_E = 1.4426950408889634
O_Z = 0
O_XBC = 1024
O_DT = 2560
O_UV = 2576
O_POOL = 3600
O_GATE = 4112

LANES = 128
SUBLANES = 8
HEADS_PER_GROUP = SSD_HEADS // SSD_GROUPS
GROUP_INNER = SSD_INNER // SSD_GROUPS
VMEM_LIMIT = 56 * 1024 * 1024

PROMPT_BLOCK = 256
SAMPLE_SEQS = 16
SAMPLE_UNROLL = 8
TOKEN_BLOCK = 512
MXU_DEPTH = 256
FF_SPLIT = (D_FF // MXU_DEPTH + 1) // 2 * MXU_DEPTH


def _dot(a, b):
    return jnp.dot(a, b, preferred_element_type=F32)


def _dot_nt(a, b):
    return lax.dot_general(a, b, (((1,), (1,)), ((), ())), preferred_element_type=F32)


def _dot_tn(a, b):
    return lax.dot_general(a, b, (((0,), (0,)), ((), ())), preferred_element_type=F32)


def _sigmoid(x):
    return 0.5 * jnp.tanh(0.5 * x) + 0.5


def _silu(x):
    h = 0.5 * x
    return h * jnp.tanh(h) + h


def _gelu_tanh(x):
    c = math.sqrt(2.0 / math.pi)
    return 0.5 * x * (1.0 + jnp.tanh(c * (x + 0.044715 * (x * x * x))))


def _softplus(x):
    return jnp.maximum(x, 0.0) + jnp.log1p(jnp.exp(-jnp.abs(x)))


def _rmsnorm(x, g):
    ms = jnp.mean(x * x, axis=-1, keepdims=True)
    return x * lax.rsqrt(ms + EPS) * g


def _split2(v):
    hi = v.astype(BF16)
    lo = (v - hi.astype(F32)).astype(BF16)
    return jnp.concatenate([hi, lo], axis=-1)


def _expand_heads(v, e2):
    return _dot(_split2(v), e2)


def _group_rmsnorm(y, g):
    parts = []
    for k in range(SSD_GROUPS):
        yk = y[:, k * GROUP_INNER:(k + 1) * GROUP_INNER]
        ms = jnp.mean(yk * yk, axis=-1, keepdims=True)
        parts.append(yk * lax.rsqrt(ms + EPS))
    return jnp.concatenate(parts, axis=-1) * g


def _row_iota(shape):
    return lax.broadcasted_iota(jnp.int32, shape, 0)


def _lane_iota(shape):
    return lax.broadcasted_iota(jnp.int32, shape, 1)


def _ssd_chunk(xs_bf, bm, cm, dtc, a_row, ht_ref):
    q = SSD_CHUNK
    tri = _row_iota((q, q)) >= _lane_iota((q, q))
    tril_bf = jnp.where(tri, 1.0, 0.0).astype(BF16)
    da = dtc * a_row
    hi = da.astype(BF16)
    r1 = da - hi.astype(F32)
    mid = r1.astype(BF16)
    lo = (r1 - mid.astype(F32)).astype(BF16)
    cs3 = _dot(tril_bf, jnp.concatenate([hi, mid, lo], axis=1))
    a_cs = (cs3[:, :LANES] + cs3[:, LANES:2 * LANES] + cs3[:, 2 * LANES:]) * LOG2_E
    a_t = a_cs.T
    ap_t = (a_cs - jnp.log2(dtc)).T
    w_t = jnp.exp2(a_t[:, q - 1:q] - ap_t)
    c_bf = cm.astype(BF16)
    lane = _lane_iota((q, LANES))
    ys = []
    for g in range(SSD_GROUPS):
        scol = slice(g * SSD_STATE, (g + 1) * SSD_STATE)
        cg = c_bf[:, scol]
        cb = _dot_nt(cg, bm[:, scol].astype(BF16))
        bg_t = bm[:, scol].T
        ht_g = ht_ref[:, g * GROUP_INNER:(g + 1) * GROUP_INNER]
        yoff = _dot(cg, ht_g.astype(BF16))
        for pr in range(HEADS_PER_GROUP // 2):
            h0 = g * HEADS_PER_GROUP + 2 * pr
            acols, ms, bws = [], [], []
            for h in (h0, h0 + 1):
                acol = jnp.broadcast_to(a_cs[:, h:h + 1], (q, q))
                acols.append(acol)
                ms.append((jnp.where(tri, jnp.exp2(acol - ap_t[h:h + 1, :]), 0.0) * cb).astype(BF16))
                bws.append((bg_t * w_t[h:h + 1, :]).astype(BF16))
            lhs = jnp.concatenate([jnp.concatenate(ms, axis=1), jnp.concatenate(bws, axis=1)], axis=0)
            xpair = xs_bf[:, h0 * SSD_HEAD_DIM:(h0 + 2) * SSD_HEAD_DIM]
            zero = jnp.zeros_like(xpair)
            rhs = jnp.concatenate([jnp.where(lane < SSD_HEAD_DIM, xpair, zero),
                                   jnp.where(lane >= SSD_HEAD_DIM, xpair, zero)], axis=0)
            out = _dot(lhs, rhs)
            ea = jnp.exp2(jnp.where(lane < SSD_HEAD_DIM, acols[0], acols[1]))
            ys.append(out[:q] + yoff[:, pr * LANES:(pr + 1) * LANES] * ea)
            cols = slice(h0 * SSD_HEAD_DIM, (h0 + 2) * SSD_HEAD_DIM)
            ht_ref[:, cols] = ht_g[:, pr * LANES:(pr + 1) * LANES] * ea[q - 1:q, :] + out[q:]
    return jnp.concatenate(ys, axis=1)


P_XBC = 0
P_DT = P_XBC + SSD_CONV_DIM
P_UV = P_DT + LANES
P_POOL = P_UV + 2 * SGU_WIDTH
P_WIDTH = P_POOL + POOL_WIDTH
PROJ_PIECE = 512
PIECES_AFTER_PHASE = (1, 1, 1, 1, 1, 1, 1)


def _branch_prompt_kernel(xa_ref, xn_ref, nm_ref, wxbc_ref, wdt_ref, wuv_ref, wpool_ref,
                          convw_ref, convb_ref, dtb_ref, alog_ref, dskip_ref,
                          lng_ref, lnb_ref, wsp_ref, bsp_ref, poolw_ref, pools_ref,
                          ya_ref, yb_ref, yc_ref, convo_ref, ssmo_ref, poolo_ref, vo_ref,
                          pa_ref, pb_ref, ht_ref, cc_ref, pc_ref, xcv_ref, *, tb, npair):
    i = pl.program_id(0)
    j = pl.program_id(1)
    nchunk = tb // SSD_CHUNK

    def project_pieces(x, p_ref):
        hb = _rmsnorm(x, nm_ref[...]).astype(BF16)
        pieces = []
        for w_ref, base, width in ((wxbc_ref, P_XBC, SSD_CONV_DIM),
                                   (wuv_ref, P_UV, 2 * SGU_WIDTH), (wpool_ref, P_POOL, POOL_WIDTH),
                                   (wdt_ref, P_DT, LANES)):
            for c0 in range(0, width, PROJ_PIECE):
                c1 = min(c0 + PROJ_PIECE, width)

                def piece(w_ref=w_ref, base=base, c0=c0, c1=c1):
                    p_ref[:, base + c0:base + c1] = _dot(hb, w_ref[:, c0:c1])
                pieces.append(piece)
        return pieces

    def process_phases(p_ref, half):
        orow = slice(half * tb, (half + 1) * tb)
        res = {}
        phases = []

        def conv(cols):
            xp = jnp.concatenate([cc_ref[:, cols], p_ref[:, P_XBC + cols.start:P_XBC + cols.stop]], axis=0)
            acc = xp[SUBLANES:, :] * convw_ref[SSD_CONV - 1:SSD_CONV, cols]
            for k in range(SSD_CONV - 1):
                acc = acc + pltpu.roll(xp, SSD_CONV - 1 - k, 0)[SUBLANES:, :] * convw_ref[k:k + 1, cols]
            xcv_ref[:, cols] = _silu(acc + convb_ref[:, cols])
        for c0 in range(0, SSD_CONV_DIM, PROJ_PIECE):
            phases.append(functools.partial(conv, slice(c0, c0 + PROJ_PIECE)))

        def scan(c):
            rows = slice(c * SSD_CHUNK, (c + 1) * SSD_CHUNK)
            xs = xcv_ref[rows, 0:SSD_INNER]
            y = _ssd_chunk(xs.astype(BF16),
                           xcv_ref[rows, SSD_INNER:SSD_INNER + SSD_GROUPS * SSD_STATE],
                           xcv_ref[rows, SSD_INNER + SSD_GROUPS * SSD_STATE:SSD_CONV_DIM],
                           _softplus(p_ref[rows, P_DT:P_UV] + dtb_ref[...]),
                           -jnp.exp(alog_ref[...]), ht_ref)
            ya_ref[half * tb + c * SSD_CHUNK:half * tb + (c + 1) * SSD_CHUNK, :] = (
                y + dskip_ref[...] * xs).astype(BF16)
            if c == nchunk - 1:
                res['tail'] = p_ref[tb - SUBLANES:tb, P_XBC:P_DT]
                cc_ref[...] = res['tail']
        for c in range(nchunk):
            phases.append(functools.partial(scan, c))

        def spatial():
            u = _gelu_tanh(p_ref[:, P_UV:P_UV + SGU_WIDTH])
            v = _gelu_tanh(p_ref[:, P_UV + SGU_WIDTH:P_POOL])
            mu = jnp.mean(v, axis=-1, keepdims=True)
            vc = v - mu
            var = jnp.mean(vc * vc, axis=-1, keepdims=True)
            vn = vc * lax.rsqrt(var + EPS) * lng_ref[...] + lnb_ref[...]
            res['vn'] = vn
            vn_bf = vn.astype(BF16)
            tri = _row_iota((SGU_CHUNK, SGU_CHUNK)) >= _lane_iota((SGU_CHUNK, SGU_CHUNK))
            s_groups = []
            for g in range(SGU_GROUPS):
                wm = jnp.where(tri, wsp_ref[g], 0.0).astype(BF16)
                vg = jnp.concatenate([vn_bf[c * SGU_CHUNK:(c + 1) * SGU_CHUNK, g * SGU_GW:(g + 1) * SGU_GW]
                                      for c in range(nchunk)], axis=1)
                s_groups.append(_dot(wm, vg))
            s = jnp.concatenate([jnp.concatenate([sg[:, c * SGU_GW:(c + 1) * SGU_GW] for sg in s_groups], axis=1)
                                 + bsp_ref[...] for c in range(nchunk)], axis=0)
            yb_ref[orow, :] = (u * s).astype(BF16)
        phases.append(spatial)

        def pool():
            pos = (2 * j + half) * tb + _row_iota((tb, POOL_GW))
            yc_cols = []
            for g, w in enumerate(POOL_WINDOWS):
                cols = slice(P_POOL + g * POOL_GW, P_POOL + (g + 1) * POOL_GW)
                cur = p_ref[:, cols]
                wsum = jnp.concatenate([pc_ref[:, g * POOL_GW:(g + 1) * POOL_GW], cur], axis=0)
                span = 1
                while span < w:
                    wsum = wsum + pltpu.roll(wsum, span, 0)
                    span *= 2
                cnt = jnp.minimum(pos + 1, w).astype(F32)
                d = wsum[2 * SUBLANES:, :] / cnt - cur
                yc_cols.append(_dot(d.astype(BF16), poolw_ref[g]))
            yc_ref[orow, :] = (jnp.concatenate(yc_cols, axis=1) * pools_ref[...]).astype(BF16)
            res['ptail'] = p_ref[tb - 2 * SUBLANES:tb, P_POOL:P_WIDTH]
            pc_ref[...] = res['ptail']
        phases.append(pool)
        return phases, res

    def run(p_cur, half, x_next, p_next):
        phases, res = process_phases(p_cur, half)
        pieces = project_pieces(x_next, p_next)
        assert len(phases) == len(PIECES_AFTER_PHASE) and len(pieces) == sum(PIECES_AFTER_PHASE)
        done = 0
        for phase, count in zip(phases, PIECES_AFTER_PHASE):
            phase()
            for piece in pieces[done:done + count]:
                piece()
            done += count
        return res

    @pl.when(jnp.logical_and(i == 0, j == 0))
    def _():
        for piece in project_pieces(xa_ref[0:tb, :], pa_ref):
            piece()

    @pl.when(j == 0)
    def _():
        ht_ref[...] = jnp.zeros_like(ht_ref)
        cc_ref[...] = jnp.zeros_like(cc_ref)
        pc_ref[...] = jnp.zeros_like(pc_ref)

    run(pa_ref, 0, xa_ref[tb:2 * tb, :], pb_ref)
    res = run(pb_ref, 1, xn_ref[...], pa_ref)

    @pl.when(j == npair - 1)
    def _():
        convo_ref[...] = res['tail']
        ssmo_ref[...] = ht_ref[...].T
        poolo_ref[...] = res['ptail']
        vo_ref[...] = res['vn'][tb - SGU_CHUNK:tb, :]


def _const_spec(arr):
    nd = arr.ndim
    return pl.BlockSpec(arr.shape, lambda *_: (0,) * nd, pipeline_mode=pl.Buffered(1))


def _layer_spec(arr, layer):
    nd = arr.ndim - 1
    return pl.BlockSpec((None,) + arr.shape[1:], lambda *_: (layer,) + (0,) * nd, pipeline_mode=pl.Buffered(1))


def _skip_refs(body, first, count):
    def wrapped(*refs):
        return body(*refs[:first], *refs[first + count:])
    return wrapped


BRANCH_WEIGHTS = ('norm_mix', 'w_xbc', 'w_dt', 'w_uv', 'w_pool', 'conv_w', 'conv_b', 'dt_bias', 'a_log',
                  'd_skip', 'ln_g', 'ln_b')
N_STATE_OUTPUTS = 4


def _branch_prompt(x, sw, layer, prev_states):
    b, l, _ = x.shape
    depth = sw['w_xbc'].shape[0]
    tb = PROMPT_BLOCK
    npair = l // (2 * tb)
    assert l % (2 * tb) == 0 and tb % SSD_CHUNK == 0 and tb >= 2 * SUBLANES
    stacked = [sw[k] for k in BRANCH_WEIGHTS + ('w_sp', 'b_sp', 'pool_w', 'pool_scale')]
    shared = []
    tok = lambda width: pl.BlockSpec((None, 2 * tb, width), lambda i, j: (i, j, 0))

    def next_block(i, j):
        flat = jnp.minimum(i * npair + j + 1, b * npair - 1)
        return flat // npair, (flat % npair) * 2, 0

    per_seq = lambda rows, width: pl.BlockSpec((None, None, rows, width), lambda i, j: (layer, i, 0, 0))
    out_shape = (jax.ShapeDtypeStruct((b, l, SSD_INNER), BF16),
                 jax.ShapeDtypeStruct((b, l, SGU_WIDTH), BF16),
                 jax.ShapeDtypeStruct((b, l, POOL_WIDTH), BF16),
                 jax.ShapeDtypeStruct((depth, b, SUBLANES, SSD_CONV_DIM), F32),
                 jax.ShapeDtypeStruct((depth, b, SSD_INNER, SSD_STATE), F32),
                 jax.ShapeDtypeStruct((depth, b, 2 * SUBLANES, POOL_WIDTH), F32),
                 jax.ShapeDtypeStruct((depth, b, SGU_CHUNK, SGU_WIDTH), F32))
    out_specs = (tok(SSD_INNER), tok(SGU_WIDTH), tok(POOL_WIDTH),
                 per_seq(SUBLANES, SSD_CONV_DIM), per_seq(SSD_INNER, SSD_STATE),
                 per_seq(2 * SUBLANES, POOL_WIDTH), per_seq(SGU_CHUNK, SGU_WIDTH))
    scratch = [pltpu.VMEM((tb, P_WIDTH), F32),
               pltpu.VMEM((tb, P_WIDTH), F32),
               pltpu.VMEM((SSD_STATE, SSD_INNER), F32),
               pltpu.VMEM((SUBLANES, SSD_CONV_DIM), F32),
               pltpu.VMEM((2 * SUBLANES, POOL_WIDTH), F32),
               pltpu.VMEM((tb, SSD_CONV_DIM), F32)]
    body = functools.partial(_branch_prompt_kernel, tb=tb, npair=npair)
    in_specs = ([tok(D_MODEL), pl.BlockSpec((None, tb, D_MODEL), next_block)]
                + [_layer_spec(w, layer) for w in stacked] + [_const_spec(w) for w in shared])
    args = [x, x] + stacked + shared
    aliases = {}
    if prev_states is not None:
        body = _skip_refs(body, len(args), N_STATE_OUTPUTS)
        aliases = {len(args) + k: 3 + k for k in range(N_STATE_OUTPUTS)}
        in_specs = in_specs + [pl.BlockSpec(memory_space=pl.ANY)] * N_STATE_OUTPUTS
        args = args + list(prev_states)
    ya, yb, yc, *states = pl.pallas_call(
        body, grid=(b, npair), in_specs=in_specs,
        out_specs=out_specs, out_shape=out_shape, scratch_shapes=scratch,
        input_output_aliases=aliases,
        compiler_params=pltpu.CompilerParams(dimension_semantics=("arbitrary", "arbitrary"),
                                             vmem_limit_bytes=VMEM_LIMIT),
        name="branch_prompt",
    )(*args)
    return ya, yb, yc, states


def _branch_sample_kernel(x_ref, convs_ref, ssms_ref, pools_in_ref, nm_ref, wxbc_ref, wdt_ref, wuv_ref,
                          wpool_ref, convw_ref, convb_ref, dtb_ref, alog_ref, dskip_ref,
                          lng_ref, lnb_ref, wd_ref, b8_ref, poolw_ref, pools_ref, e2_ref,
                          ya_ref, yb_ref, yc_ref, convo_ref, ssmo_ref, poolo_ref, vo_ref,
                          c_ref, bm_ref, xd_ref, daug_ref, yoff_ref, *, nb, seq, start):
    r = nb * seq
    assert seq == SUBLANES
    t128 = _row_iota((r, LANES)) % seq

    def tile_roll(v, j):
        width = v.shape[-1]
        return pltpu.roll(v.reshape(nb, seq, width), j, 1).reshape(r, width)

    def shift_rows(cur, prev, j, width):
        tt = _row_iota((r, width)) % seq
        if prev is None:
            return jnp.where(tt >= j, tile_roll(cur, j), 0.0)
        return jnp.where(tt >= j, tile_roll(cur, j), tile_roll(prev, j))

    hb = _rmsnorm(x_ref[...].reshape(r, D_MODEL), nm_ref[...]).astype(BF16)

    xbc = _dot(hb, wxbc_ref[...])
    cbuf = convs_ref[...].reshape(r, SSD_CONV_DIM)
    acc = xbc * convw_ref[SSD_CONV - 1:SSD_CONV, :]
    for k in range(SSD_CONV - 1):
        acc = acc + shift_rows(xbc, cbuf, SSD_CONV - 1 - k, SSD_CONV_DIM) * convw_ref[k:k + 1, :]
    xcv = _silu(acc + convb_ref[...])
    convo_ref[...] = xbc.reshape(nb, seq, SSD_CONV_DIM)
    xs = xcv[:, 0:SSD_INNER]
    bm = xcv[:, SSD_INNER:SSD_INNER + SSD_GROUPS * SSD_STATE]
    cm = xcv[:, SSD_INNER + SSD_GROUPS * SSD_STATE:SSD_CONV_DIM]
    dt = _softplus(_dot(hb, wdt_ref[...]) + dtb_ref[...])
    a_row = -jnp.exp(alog_ref[...])
    a_cs = dt * a_row
    for s in (1, 2, 4):
        a_cs = a_cs + jnp.where(t128 >= s, tile_roll(a_cs, s), 0.0)
    a3 = a_cs.reshape(nb, seq, LANES)
    tot = jnp.broadcast_to(a3[:, seq - 1:seq, :], (nb, seq, LANES)).reshape(r, LANES)
    e2 = e2_ref[...]
    exp_a_e = _expand_heads(jnp.exp(a_cs), e2)
    decst_e = _expand_heads(jnp.exp(tot - a_cs), e2)
    dt_e = _expand_heads(dt, e2)
    dectot_e = _expand_heads(jnp.exp(tot), e2)
    xdt = xs * dt_e
    lane = _lane_iota((r, LANES))
    y = jnp.zeros((r, SSD_INNER), F32)
    for j in range(seq):
        if j == 0:
            lj = jnp.ones((r, LANES), F32)
            b_sh, x_sh = bm, xdt
        else:
            lj = jnp.where(t128 >= j, jnp.exp(a_cs - tile_roll(a_cs, j)), 0.0)
            b_sh, x_sh = tile_roll(bm, j), tile_roll(xdt, j)
        prod = cm * b_sh
        cb0 = jnp.sum(prod[:, 0:SSD_STATE], axis=-1, keepdims=True)
        cb1 = jnp.sum(prod[:, SSD_STATE:2 * SSD_STATE], axis=-1, keepdims=True)
        mj = lj * jnp.where(lane < HEADS_PER_GROUP, cb0, cb1)
        y = y + _expand_heads(mj, e2) * x_sh
    c_ref[...] = cm
    bm_ref[...] = bm
    xd_ref[...] = xdt * decst_e
    dec_hi = dectot_e.astype(BF16).astype(F32)
    t1024 = _row_iota((r, SSD_INNER)) % seq
    daug_ref[...] = jnp.where(t1024 == 0, dec_hi, jnp.where(t1024 == 1, dectot_e - dec_hi, 0.0))
    t8 = _row_iota((seq, LANES))
    ones2 = jnp.where(t8 < 2, 1.0, 0.0)
    zeros8 = jnp.zeros((seq, LANES), F32)

    def per_seq(n, carry):
        rows = pl.ds(pl.multiple_of(n * seq, seq), seq)
        for g in range(SSD_GROUPS):
            cols = slice(g * GROUP_INNER, (g + 1) * GROUP_INNER)
            scol = slice(g * SSD_STATE, (g + 1) * SSD_STATE)
            h0 = ssms_ref[n, cols, :]
            cg = c_ref[rows, scol].astype(BF16)
            yoff_ref[rows, cols] = _dot_nt(cg, h0.astype(BF16))
            lhs = jnp.concatenate([xd_ref[rows, cols], daug_ref[rows, cols]], axis=0).astype(BF16)
            rhs = jnp.concatenate([jnp.concatenate([bm_ref[rows, scol], zeros8], axis=1),
                                   jnp.concatenate([zeros8, ones2], axis=1)], axis=0).astype(BF16)
            upd = _dot_tn(lhs, rhs)
            ssmo_ref[n, cols, :] = upd[:, SSD_STATE:] * h0 + upd[:, :SSD_STATE]
        return carry

    lax.fori_loop(0, nb, per_seq, 0, unroll=SAMPLE_UNROLL)
    ya_ref[...] = (y + yoff_ref[...] * exp_a_e + dskip_ref[...] * xs).astype(BF16)

    a = _gelu_tanh(_dot(hb, wuv_ref[...]))
    u = a[:, :SGU_WIDTH]
    v = a[:, SGU_WIDTH:]
    mu = jnp.mean(v, axis=-1, keepdims=True)
    vc = v - mu
    var = jnp.mean(vc * vc, axis=-1, keepdims=True)
    vn = vc * lax.rsqrt(var + EPS) * lng_ref[...] + lnb_ref[...]
    vo_ref[...] = vn
    s = vn.reshape(nb, seq, SGU_WIDTH) * wd_ref[0] + b8_ref[...]
    for j in range(1, seq):
        s = s + tile_roll(vn, j).reshape(nb, seq, SGU_WIDTH) * wd_ref[j]
    yb_ref[...] = (u * s.reshape(r, SGU_WIDTH)).astype(BF16)

    xc = _dot(hb, wpool_ref[...])
    pbuf = pools_in_ref[...]
    t0 = pbuf[:, 0:seq, :].reshape(r, POOL_WIDTH)
    t1 = pbuf[:, seq:2 * seq, :].reshape(r, POOL_WIDTH)
    poolo_ref[:, 0:seq, :] = pbuf[:, seq:2 * seq, :]
    poolo_ref[:, seq:2 * seq, :] = xc.reshape(nb, seq, POOL_WIDTH)
    pos = start + (_row_iota((r, POOL_GW)) % seq)
    yc_cols = []
    for g, w in enumerate(POOL_WINDOWS):
        cols = slice(g * POOL_GW, (g + 1) * POOL_GW)
        tiles = [t0[:, cols], t1[:, cols], xc[:, cols]]
        span = 1
        while span < min(w, seq):
            prev = [None] + tiles[:-1]
            tiles = [tl + shift_rows(tl, pv, span, POOL_GW) for tl, pv in zip(tiles, prev)]
            span *= 2
        wsum = tiles[2] if w <= seq else tiles[2] + tiles[1]
        cnt = jnp.minimum(pos + 1, w).astype(F32)
        d = wsum / cnt - xc[:, cols]
        yc_cols.append(_dot(d.astype(BF16), poolw_ref[g]))
    yc_ref[...] = (jnp.concatenate(yc_cols, axis=1) * pools_ref[...]).astype(BF16)


def _branch_sample(x, conv_pad, ssm_flat, pool_pad, sw, layer, prev_states, start):
    n, seq, _ = x.shape
    depth = sw['w_xbc'].shape[0]
    nb = SAMPLE_SEQS
    assert n % nb == 0 and seq == SUBLANES and start >= max(POOL_WINDOWS)
    r = nb * seq
    stacked = [sw[k] for k in BRANCH_WEIGHTS + ('w_diag8', 'b8', 'pool_w', 'pool_scale')]
    shared = [sw['e2']]
    seq3 = lambda rows, width: pl.BlockSpec((nb, rows, width), lambda i: (i, 0, 0))
    seq4 = lambda rows, width: pl.BlockSpec((None, nb, rows, width), lambda i: (layer, i, 0, 0))
    tok = lambda width: pl.BlockSpec((r, width), lambda i: (i, 0))
    out_shape = (jax.ShapeDtypeStruct((n * seq, SSD_INNER), BF16),
                 jax.ShapeDtypeStruct((n * seq, SGU_WIDTH), BF16),
                 jax.ShapeDtypeStruct((n * seq, POOL_WIDTH), BF16),
                 jax.ShapeDtypeStruct((depth, n, seq, SSD_CONV_DIM), F32),
                 jax.ShapeDtypeStruct((depth, n, SSD_INNER, SSD_STATE), F32),
                 jax.ShapeDtypeStruct((depth, n, 2 * seq, POOL_WIDTH), F32),
                 jax.ShapeDtypeStruct((depth, n * seq, SGU_WIDTH), F32))
    out_specs = (tok(SSD_INNER), tok(SGU_WIDTH), tok(POOL_WIDTH), seq4(seq, SSD_CONV_DIM),
                 seq4(SSD_INNER, SSD_STATE), seq4(2 * seq, POOL_WIDTH),
                 pl.BlockSpec((None, r, SGU_WIDTH), lambda i: (layer, i, 0)))
    scratch = [pltpu.VMEM((r, SSD_GROUPS * SSD_STATE), F32),
               pltpu.VMEM((r, SSD_GROUPS * SSD_STATE), F32),
               pltpu.VMEM((r, SSD_INNER), F32),
               pltpu.VMEM((r, SSD_INNER), F32),
               pltpu.VMEM((r, SSD_INNER), F32)]
    body = functools.partial(_branch_sample_kernel, nb=nb, seq=seq, start=start)
    in_specs = ([seq3(seq, D_MODEL), seq4(seq, SSD_CONV_DIM), seq4(SSD_INNER, SSD_STATE), seq4(2 * seq, POOL_WIDTH)]
                + [_layer_spec(w, layer) for w in stacked] + [_const_spec(w) for w in shared])
    args = [x, conv_pad, ssm_flat, pool_pad] + stacked + shared
    aliases = {}
    if prev_states is not None:
        body = _skip_refs(body, len(args), N_STATE_OUTPUTS)
        aliases = {len(args) + k: 3 + k for k in range(N_STATE_OUTPUTS)}
        in_specs = in_specs + [pl.BlockSpec(memory_space=pl.ANY)] * N_STATE_OUTPUTS
        args = args + list(prev_states)
    ya, yb, yc, *states = pl.pallas_call(
        body, grid=(n // nb,), in_specs=in_specs,
        out_specs=out_specs, out_shape=out_shape, scratch_shapes=scratch,
        input_output_aliases=aliases,
        compiler_params=pltpu.CompilerParams(dimension_semantics=("arbitrary",),
                                             vmem_limit_bytes=VMEM_LIMIT),
        name="branch_sample",
    )(*args)
    return ya, yb, yc, states


def _merge_kernel(x_ref, ya_ref, yb_ref, yc_ref, nm_ref, wz_ref, ssdn_ref, wg_ref, wa_ref, wb_ref, wc_ref, wo_ref,
                  o_ref):
    x = x_ref[...]
    hb = _rmsnorm(x, nm_ref[...]).astype(BF16)
    ya = _group_rmsnorm(ya_ref[...].astype(F32) * _silu(_dot(hb, wz_ref[...])), ssdn_ref[...]).astype(BF16)
    m = _sigmoid(_dot(hb, wg_ref[:, 0:D_MODEL])) * _dot(ya, wa_ref[...])
    m = m + _sigmoid(_dot(hb, wg_ref[:, D_MODEL:2 * D_MODEL])) * _dot(yb_ref[...], wb_ref[...])
    m = m + _sigmoid(_dot(hb, wg_ref[:, 2 * D_MODEL:3 * D_MODEL])) * _dot(yc_ref[...], wc_ref[...])
    o_ref[...] = x + _dot(m.astype(BF16), wo_ref[...])


def _merge(x, ya, yb, yc, sw, layer):
    t = x.shape[0]
    tm = min(TOKEN_BLOCK, t)
    assert t % tm == 0
    stacked = [sw[k] for k in ('norm_mix', 'w_z', 'ssd_norm', 'w_gate', 'w_br_a', 'w_br_b', 'w_br_c', 'w_out')]
    tok = lambda width: pl.BlockSpec((tm, width), lambda i: (i, 0))
    return pl.pallas_call(
        _merge_kernel, grid=(t // tm,),
        in_specs=[tok(D_MODEL), tok(SSD_INNER), tok(SGU_WIDTH), tok(POOL_WIDTH)] + [_layer_spec(w, layer) for w in stacked],
        out_specs=tok(D_MODEL), out_shape=jax.ShapeDtypeStruct((t, D_MODEL), F32),
        compiler_params=pltpu.CompilerParams(dimension_semantics=("arbitrary",), vmem_limit_bytes=VMEM_LIMIT),
        name="merge",
    )(x, ya, yb, yc, *stacked)


def _ffn_kernel(x_ref, p_ref, nf_ref, wgu_ref, wd_ref, np_ref, wpg_ref, wpu_ref, fn_ref, o_ref, *, final):
    x = x_ref[...]
    hb = _rmsnorm(x, nf_ref[...]).astype(BF16)
    acc = x
    for cols in (slice(0, FF_SPLIT), slice(FF_SPLIT, D_FF)):
        up_cols = slice(D_FF + cols.start, D_FF + cols.stop)
        act = (_silu(_dot(hb, wgu_ref[:, cols])) * _dot(hb, wgu_ref[:, up_cols])).astype(BF16)
        acc = acc + _dot(act, wd_ref[cols, :])
    hb = _rmsnorm(acc, np_ref[...]).astype(BF16)
    out = acc + _dot(p_ref[...].astype(BF16), wpu_ref[...]) * _sigmoid(_dot(hb, wpg_ref[...]))
    if final:
        out = _rmsnorm(out, fn_ref[...])
    o_ref[...] = out


def _ffn(x, p, sw, layer, final):
    t = x.shape[0]
    tm = min(TOKEN_BLOCK, t)
    assert t % tm == 0
    stacked = [sw[k] for k in ('norm_ffn', 'w_gate_up', 'w_down', 'norm_ple', 'w_ple_gate', 'w_ple_up')]
    tok = lambda width: pl.BlockSpec((tm, width), lambda i: (i, 0))
    return pl.pallas_call(
        functools.partial(_ffn_kernel, final=final), grid=(t // tm,),
        in_specs=([tok(D_MODEL), pl.BlockSpec((None, tm, PLE_DIM), lambda i: (layer, i, 0))]
                  + [_layer_spec(w, layer) for w in stacked] + [_const_spec(sw['final_norm'])]),
        out_specs=tok(D_MODEL), out_shape=jax.ShapeDtypeStruct((t, D_MODEL), F32),
        compiler_params=pltpu.CompilerParams(dimension_semantics=("arbitrary",), vmem_limit_bytes=VMEM_LIMIT),
        name="ffn",
    )(x, p, *stacked, sw['final_norm'])


IN_DIM = O_GATE + 3 * D_MODEL
W_IN_ROWS = 256


def _split_w_in_kernel(w_ref, z_ref, xbc_ref, dt_ref, uv_ref, pool_ref, gate_ref):
    z_ref[...] = w_ref[:, O_Z:O_XBC].astype(BF16)
    xbc_ref[...] = w_ref[:, O_XBC:O_DT].astype(BF16)
    dt = w_ref[:, O_DT:O_DT + LANES]
    dt_ref[...] = jnp.where(_lane_iota(dt.shape) < SSD_HEADS, dt, 0.0).astype(BF16)
    uv_ref[...] = w_ref[:, O_UV:O_POOL].astype(BF16)
    pool_ref[...] = w_ref[:, O_POOL:O_GATE].astype(BF16)
    gate_ref[...] = w_ref[:, O_GATE:IN_DIM].astype(BF16)


def _split_w_in(w_in):
    depth, d, width = w_in.shape
    assert width == IN_DIM and d % W_IN_ROWS == 0
    widths = (O_XBC - O_Z, O_DT - O_XBC, LANES, O_POOL - O_UV, O_GATE - O_POOL, IN_DIM - O_GATE)
    spec = lambda w: pl.BlockSpec((None, W_IN_ROWS, w), lambda i, j: (i, j, 0))
    return pl.pallas_call(
        _split_w_in_kernel, grid=(depth, d // W_IN_ROWS),
        in_specs=[spec(IN_DIM)], out_specs=tuple(spec(w) for w in widths),
        out_shape=tuple(jax.ShapeDtypeStruct((depth, d, w), BF16) for w in widths),
        compiler_params=pltpu.CompilerParams(dimension_semantics=("arbitrary", "arbitrary"),
                                             vmem_limit_bytes=VMEM_LIMIT),
        name="split_w_in",
    )(w_in)


def _head_expansion():
    rows = jnp.arange(2 * LANES)[:, None] % LANES
    cols = jnp.arange(SSD_INNER)[None, :] // SSD_HEAD_DIM
    e2 = (rows == cols).astype(BF16)
    return e2, e2.T


def _prepare_weights(norm_mix, w_in, conv_w, conv_b, dt_bias, a_log, d_skip, ssd_norm, sgu_ln_g, sgu_ln_b,
                     w_spatial, b_spatial, pool_w, pool_scale, w_br_a, w_br_b, w_br_c, w_out, norm_ffn,
                     w_gate_up, w_down, norm_ple, w_ple_gate, w_ple_up, final_norm):
    depth = w_in.shape[0]
    row = lambda v: v.reshape(depth, 1, -1).astype(F32)
    pad_heads = lambda v: jnp.pad(v.reshape(depth, 1, -1).astype(F32), ((0, 0), (0, 0), (0, LANES - SSD_HEADS)))
    seq = SUBLANES
    w8 = w_spatial[:, :, :seq, :seq]
    tt = jnp.arange(seq)
    lag = tt[None, :] - tt[:, None]
    w_lag = jnp.where(lag >= 0, w8[:, :, tt[None, :], jnp.clip(lag, 0, seq - 1)], 0.0)
    w_diag8 = jnp.repeat(jnp.transpose(w_lag, (0, 2, 3, 1)), SGU_GW, axis=3).astype(F32)
    b_full = jnp.repeat(jnp.transpose(b_spatial, (0, 2, 1)), SGU_GW, axis=2).astype(F32)
    e2, et2 = _head_expansion()
    w_z, w_xbc, w_dt, w_uv, w_pool, w_gate = _split_w_in(w_in)
    return {
        'norm_mix': row(norm_mix),
        'w_z': w_z, 'w_xbc': w_xbc, 'w_dt': w_dt, 'w_uv': w_uv, 'w_pool': w_pool, 'w_gate': w_gate,
        'conv_w': conv_w.astype(F32), 'conv_b': row(conv_b),
        'dt_bias': pad_heads(dt_bias), 'a_log': pad_heads(a_log),
        'd_skip': row(jnp.repeat(d_skip, SSD_HEAD_DIM, axis=1)),
        'ssd_norm': row(ssd_norm),
        'ln_g': row(sgu_ln_g), 'ln_b': row(sgu_ln_b),
        'w_sp': w_spatial.astype(F32), 'b_sp': b_full,
        'w_diag8': w_diag8, 'b8': b_full[:, :seq],
        'pool_w': pool_w.astype(BF16), 'pool_scale': row(pool_scale),
        'w_br_a': w_br_a.astype(BF16), 'w_br_b': w_br_b.astype(BF16), 'w_br_c': w_br_c.astype(BF16),
        'w_out': w_out.astype(BF16),
        'norm_ffn': row(norm_ffn),
        'w_gate_up': w_gate_up.astype(BF16),
        'w_down': w_down.astype(BF16),
        'norm_ple': row(norm_ple),
        'w_ple_gate': w_ple_gate.astype(BF16), 'w_ple_up': w_ple_up.astype(BF16),
        'final_norm': final_norm.reshape(1, -1).astype(F32),
        'e2': e2, 'et2': et2,
    }


def kernel(x_prompt, x_sample, state_conv, state_ssm, state_pool, p_prompt, p_sample, norm_mix, w_in, conv_w, conv_b, dt_bias, a_log, d_skip, ssd_norm, sgu_ln_g, sgu_ln_b, w_spatial, b_spatial, pool_w, pool_scale, w_br_a, w_br_b, w_br_c, w_out, norm_ffn, w_gate_up, w_down, norm_ple, w_ple_gate, w_ple_up, final_norm):
    b, l, _ = x_prompt.shape
    n, s, _ = x_sample.shape
    depth = w_in.shape[0]
    sw = _prepare_weights(norm_mix, w_in, conv_w, conv_b, dt_bias, a_log, d_skip, ssd_norm, sgu_ln_g, sgu_ln_b,
                          w_spatial, b_spatial, pool_w, pool_scale, w_br_a, w_br_b, w_br_c, w_out, norm_ffn,
                          w_gate_up, w_down, norm_ple, w_ple_gate, w_ple_up, final_norm)
    conv_pad = jnp.pad(state_conv, ((0, 0), (0, 0), (SUBLANES - (SSD_CONV - 1), 0), (0, 0)))
    pool_pad = jnp.pad(state_pool, ((0, 0), (0, 0), (2 * SUBLANES - POOL_BUF, 0), (0, 0)))
    ssm_flat = state_ssm.reshape(depth, n, SSD_INNER, SSD_STATE)
    pp = p_prompt.reshape(depth, b * l, PLE_DIM)
    ps = p_sample.reshape(depth, n * s, PLE_DIM)
    xp = x_prompt
    xs = x_sample.reshape(n * s, D_MODEL)
    st_p = None
    st_s = None
    for i in range(depth):
        final = i == depth - 1
        ya, yb, yc, st_p = _branch_prompt(xp, sw, i, st_p)
        x1 = _merge(xp.reshape(b * l, D_MODEL), ya.reshape(b * l, -1), yb.reshape(b * l, -1), yc.reshape(b * l, -1), sw, i)
        xp = _ffn(x1, pp, sw, i, final).reshape(b, l, D_MODEL)
        ya, yb, yc, st_s = _branch_sample(xs.reshape(n, s, D_MODEL), conv_pad, ssm_flat, pool_pad, sw, i, st_s, PAST_LEN)
        x1 = _merge(xs, ya, yb, yc, sw, i)
        xs = _ffn(x1, ps, sw, i, final)
    conv_p, ssm_p, pool_p, v_p = st_p
    conv_s, ssm_s, pool_s, v_s = st_s
    return (xp, xs.reshape(n, s, D_MODEL),
            conv_p[:, :, SUBLANES - (SSD_CONV - 1):, :],
            ssm_p.reshape(depth, b, SSD_HEADS, SSD_HEAD_DIM, SSD_STATE),
            pool_p[:, :, 2 * SUBLANES - POOL_BUF:, :],
            v_p,
            conv_s[:, :, s - (SSD_CONV - 1):, :],
            ssm_s.reshape(depth, n, SSD_HEADS, SSD_HEAD_DIM, SSD_STATE),
            pool_s[:, :, 2 * s - POOL_BUF:, :],
            v_s.reshape(depth, n, s, SGU_WIDTH))
```

```python
import functools
import math

import jax
import jax.numpy as jnp
from jax import lax
from jax.experimental import pallas as pl
from jax.experimental.pallas import tpu as pltpu

F32 = jnp.float32
BF16 = jnp.bfloat16

D_MODEL = 1024
DEPTH = 4
PAST_LEN = 16384
SSD_HEAD_DIM = 64
SSD_HEADS = 16
SSD_INNER = 1024
SSD_GROUPS = 2
SSD_STATE = 128
SSD_CONV = 4
SSD_CHUNK = 128
SSD_CONV_DIM = 1536
SGU_WIDTH = 512
SGU_GROUPS = 4
SGU_CHUNK = 128
SGU_GW = 128
POOL_WIDTH = 512
POOL_WINDOWS = (2, 4, 8, 16)
POOL_GW = 128
POOL_BUF = 15
D_FF = 2816
PLE_DIM = 256
EPS = 1e-6
LOG2_E = 1.4426950408889634
O_Z = 0
O_XBC = 1024
O_DT = 2560
O_UV = 2576
O_POOL = 3600
O_GATE = 4112

LANES = 128
SUBLANES = 8
HEADS_PER_GROUP = SSD_HEADS // SSD_GROUPS
GROUP_INNER = SSD_INNER // SSD_GROUPS
VMEM_LIMIT = 56 * 1024 * 1024

PROMPT_BLOCK = 256
SAMPLE_SEQS = 16
SAMPLE_UNROLL = 8
TOKEN_BLOCK = 512
FFN_STEP_ROWS = 1024
MXU_DEPTH = 256
FF_SPLIT = (D_FF // MXU_DEPTH + 1) // 2 * MXU_DEPTH


def _dot(a, b):
    return jnp.dot(a, b, preferred_element_type=F32)


def _dot_nt(a, b):
    return lax.dot_general(a, b, (((1,), (1,)), ((), ())), preferred_element_type=F32)


def _dot_tn(a, b):
    return lax.dot_general(a, b, (((0,), (0,)), ((), ())), preferred_element_type=F32)


def _sigmoid(x):
    return 0.5 * jnp.tanh(0.5 * x) + 0.5


def _silu(x):
    h = 0.5 * x
    return h * jnp.tanh(h) + h


def _gelu_tanh(x):
    c = math.sqrt(2.0 / math.pi)
    return 0.5 * x * (1.0 + jnp.tanh(c * (x + 0.044715 * (x * x * x))))


def _softplus(x):
    return jnp.maximum(x, 0.0) + jnp.log1p(jnp.exp(-jnp.abs(x)))


def _rmsnorm(x, g):
    ms = jnp.mean(x * x, axis=-1, keepdims=True)
    return x * lax.rsqrt(ms + EPS) * g


def _split2(v):
    hi = v.astype(BF16)
    lo = (v - hi.astype(F32)).astype(BF16)
    return jnp.concatenate([hi, lo], axis=-1)


def _expand_heads(v, e2):
    return _dot(_split2(v), e2)


def _group_rmsnorm(y, g):
    parts = []
    for k in range(SSD_GROUPS):
        yk = y[:, k * GROUP_INNER:(k + 1) * GROUP_INNER]
        ms = jnp.mean(yk * yk, axis=-1, keepdims=True)
        parts.append(yk * lax.rsqrt(ms + EPS))
    return jnp.concatenate(parts, axis=-1) * g


def _row_iota(shape):
    return lax.broadcasted_iota(jnp.int32, shape, 0)


def _lane_iota(shape):
    return lax.broadcasted_iota(jnp.int32, shape, 1)


def _ssd_chunk(xs_bf, bm, cm, dtc, a_row, ht_ref):
    q = SSD_CHUNK
    tri = _row_iota((q, q)) >= _lane_iota((q, q))
    tril_bf = jnp.where(tri, 1.0, 0.0).astype(BF16)
    da = dtc * a_row
    hi = da.astype(BF16)
    r1 = da - hi.astype(F32)
    mid = r1.astype(BF16)
    lo = (r1 - mid.astype(F32)).astype(BF16)
    cs3 = _dot(tril_bf, jnp.concatenate([hi, mid, lo], axis=1))
    a_cs = (cs3[:, :LANES] + cs3[:, LANES:2 * LANES] + cs3[:, 2 * LANES:]) * LOG2_E
    a_t = a_cs.T
    ap_t = (a_cs - jnp.log2(dtc)).T
    w_t = jnp.exp2(a_t[:, q - 1:q] - ap_t)
    c_bf = cm.astype(BF16)
    lane = _lane_iota((q, LANES))
    ys = []
    for g in range(SSD_GROUPS):
        scol = slice(g * SSD_STATE, (g + 1) * SSD_STATE)
        cg = c_bf[:, scol]
        cb = _dot_nt(cg, bm[:, scol].astype(BF16))
        bg_t = bm[:, scol].T
        ht_g = ht_ref[:, g * GROUP_INNER:(g + 1) * GROUP_INNER]
        yoff = _dot(cg, ht_g.astype(BF16))
        for pr in range(HEADS_PER_GROUP // 2):
            h0 = g * HEADS_PER_GROUP + 2 * pr
            acols, ms, bws = [], [], []
            for h in (h0, h0 + 1):
                acol = jnp.broadcast_to(a_cs[:, h:h + 1], (q, q))
                acols.append(acol)
                ms.append((jnp.where(tri, jnp.exp2(acol - ap_t[h:h + 1, :]), 0.0) * cb).astype(BF16))
                bws.append((bg_t * w_t[h:h + 1, :]).astype(BF16))
            lhs = jnp.concatenate([jnp.concatenate(ms, axis=1), jnp.concatenate(bws, axis=1)], axis=0)
            xpair = xs_bf[:, h0 * SSD_HEAD_DIM:(h0 + 2) * SSD_HEAD_DIM]
            zero = jnp.zeros_like(xpair)
            rhs = jnp.concatenate([jnp.where(lane < SSD_HEAD_DIM, xpair, zero),
                                   jnp.where(lane >= SSD_HEAD_DIM, xpair, zero)], axis=0)
            out = _dot(lhs, rhs)
            ea = jnp.exp2(jnp.where(lane < SSD_HEAD_DIM, acols[0], acols[1]))
            ys.append(out[:q] + yoff[:, pr * LANES:(pr + 1) * LANES] * ea)
            cols = slice(h0 * SSD_HEAD_DIM, (h0 + 2) * SSD_HEAD_DIM)
            ht_ref[:, cols] = ht_g[:, pr * LANES:(pr + 1) * LANES] * ea[q - 1:q, :] + out[q:]
    return jnp.concatenate(ys, axis=1)


P_XBC = 0
P_DT = P_XBC + SSD_CONV_DIM
P_UV = P_DT + LANES
P_POOL = P_UV + 2 * SGU_WIDTH
P_WIDTH = P_POOL + POOL_WIDTH
PROJ_PIECE = 512
PIECES_AFTER_PHASE = (1, 1, 1, 1, 1, 1, 1)


def _branch_prompt_kernel(xa_ref, xn_ref, nm_ref, wxbc_ref, wdt_ref, wuv_ref, wpool_ref,
                          convw_ref, convb_ref, dtb_ref, alog_ref, dskip_ref,
                          lng_ref, lnb_ref, wsp_ref, bsp_ref, poolw_ref, pools_ref,
                          ya_ref, yb_ref, yc_ref, convo_ref, ssmo_ref, poolo_ref, vo_ref,
                          pa_ref, pb_ref, ht_ref, cc_ref, pc_ref, xcv_ref, *, tb, npair):
    i = pl.program_id(0)
    j = pl.program_id(1)
    nchunk = tb // SSD_CHUNK

    def project_pieces(x, p_ref):
        hb = _rmsnorm(x, nm_ref[...]).astype(BF16)
        pieces = []
        for w_ref, base, width in ((wxbc_ref, P_XBC, SSD_CONV_DIM),
                                   (wuv_ref, P_UV, 2 * SGU_WIDTH), (wpool_ref, P_POOL, POOL_WIDTH),
                                   (wdt_ref, P_DT, LANES)):
            for c0 in range(0, width, PROJ_PIECE):
                c1 = min(c0 + PROJ_PIECE, width)

                def piece(w_ref=w_ref, base=base, c0=c0, c1=c1):
                    p_ref[:, base + c0:base + c1] = _dot(hb, w_ref[:, c0:c1])
                pieces.append(piece)
        return pieces

    def process_phases(p_ref, half):
        orow = slice(half * tb, (half + 1) * tb)
        res = {}
        phases = []

        def conv(cols):
            xp = jnp.concatenate([cc_ref[:, cols], p_ref[:, P_XBC + cols.start:P_XBC + cols.stop]], axis=0)
            acc = xp[SUBLANES:, :] * convw_ref[SSD_CONV - 1:SSD_CONV, cols]
            for k in range(SSD_CONV - 1):
                acc = acc + pltpu.roll(xp, SSD_CONV - 1 - k, 0)[SUBLANES:, :] * convw_ref[k:k + 1, cols]
            xcv_ref[:, cols] = _silu(acc + convb_ref[:, cols])
        for c0 in range(0, SSD_CONV_DIM, PROJ_PIECE):
            phases.append(functools.partial(conv, slice(c0, c0 + PROJ_PIECE)))

        def scan(c):
            rows = slice(c * SSD_CHUNK, (c + 1) * SSD_CHUNK)
            xs = xcv_ref[rows, 0:SSD_INNER]
            y = _ssd_chunk(xs.astype(BF16),
                           xcv_ref[rows, SSD_INNER:SSD_INNER + SSD_GROUPS * SSD_STATE],
                           xcv_ref[rows, SSD_INNER + SSD_GROUPS * SSD_STATE:SSD_CONV_DIM],
                           _softplus(p_ref[rows, P_DT:P_UV] + dtb_ref[...]),
                           -jnp.exp(alog_ref[...]), ht_ref)
            ya_ref[half * tb + c * SSD_CHUNK:half * tb + (c + 1) * SSD_CHUNK, :] = (
                y + dskip_ref[...] * xs).astype(BF16)
            if c == nchunk - 1:
                res['tail'] = p_ref[tb - SUBLANES:tb, P_XBC:P_DT]
                cc_ref[...] = res['tail']
        for c in range(nchunk):
            phases.append(functools.partial(scan, c))

        def spatial():
            u = _gelu_tanh(p_ref[:, P_UV:P_UV + SGU_WIDTH])
            v = _gelu_tanh(p_ref[:, P_UV + SGU_WIDTH:P_POOL])
            mu = jnp.mean(v, axis=-1, keepdims=True)
            vc = v - mu
            var = jnp.mean(vc * vc, axis=-1, keepdims=True)
            vn = vc * lax.rsqrt(var + EPS) * lng_ref[...] + lnb_ref[...]
            res['vn'] = vn
            vn_bf = vn.astype(BF16)
            tri = _row_iota((SGU_CHUNK, SGU_CHUNK)) >= _lane_iota((SGU_CHUNK, SGU_CHUNK))
            s_groups = []
            for g in range(SGU_GROUPS):
                wm = jnp.where(tri, wsp_ref[g], 0.0).astype(BF16)
                vg = jnp.concatenate([vn_bf[c * SGU_CHUNK:(c + 1) * SGU_CHUNK, g * SGU_GW:(g + 1) * SGU_GW]
                                      for c in range(nchunk)], axis=1)
                s_groups.append(_dot(wm, vg))
            s = jnp.concatenate([jnp.concatenate([sg[:, c * SGU_GW:(c + 1) * SGU_GW] for sg in s_groups], axis=1)
                                 + bsp_ref[...] for c in range(nchunk)], axis=0)
            yb_ref[orow, :] = (u * s).astype(BF16)
        phases.append(spatial)

        def pool():
            pos = (2 * j + half) * tb + _row_iota((tb, POOL_GW))
            yc_cols = []
            for g, w in enumerate(POOL_WINDOWS):
                cols = slice(P_POOL + g * POOL_GW, P_POOL + (g + 1) * POOL_GW)
                cur = p_ref[:, cols]
                wsum = jnp.concatenate([pc_ref[:, g * POOL_GW:(g + 1) * POOL_GW], cur], axis=0)
                span = 1
                while span < w:
                    wsum = wsum + pltpu.roll(wsum, span, 0)
                    span *= 2
                cnt = jnp.minimum(pos + 1, w).astype(F32)
                d = wsum[2 * SUBLANES:, :] / cnt - cur
                yc_cols.append(_dot(d.astype(BF16), poolw_ref[g]))
            yc_ref[orow, :] = (jnp.concatenate(yc_cols, axis=1) * pools_ref[...]).astype(BF16)
            res['ptail'] = p_ref[tb - 2 * SUBLANES:tb, P_POOL:P_WIDTH]
            pc_ref[...] = res['ptail']
        phases.append(pool)
        return phases, res

    def run(p_cur, half, x_next, p_next):
        phases, res = process_phases(p_cur, half)
        pieces = project_pieces(x_next, p_next)
        assert len(phases) == len(PIECES_AFTER_PHASE) and len(pieces) == sum(PIECES_AFTER_PHASE)
        done = 0
        for phase, count in zip(phases, PIECES_AFTER_PHASE):
            phase()
            for piece in pieces[done:done + count]:
                piece()
            done += count
        return res

    @pl.when(jnp.logical_and(i == 0, j == 0))
    def _():
        for piece in project_pieces(xa_ref[0:tb, :], pa_ref):
            piece()

    @pl.when(j == 0)
    def _():
        ht_ref[...] = jnp.zeros_like(ht_ref)
        cc_ref[...] = jnp.zeros_like(cc_ref)
        pc_ref[...] = jnp.zeros_like(pc_ref)

    run(pa_ref, 0, xa_ref[tb:2 * tb, :], pb_ref)
    res = run(pb_ref, 1, xn_ref[...], pa_ref)

    @pl.when(j == npair - 1)
    def _():
        convo_ref[...] = res['tail']
        ssmo_ref[...] = ht_ref[...].T
        poolo_ref[...] = res['ptail']
        vo_ref[...] = res['vn'][tb - SGU_CHUNK:tb, :]


def _const_spec(arr):
    nd = arr.ndim
    return pl.BlockSpec(arr.shape, lambda *_: (0,) * nd, pipeline_mode=pl.Buffered(1))


def _layer_spec(arr, layer):
    nd = arr.ndim - 1
    return pl.BlockSpec((None,) + arr.shape[1:], lambda *_: (layer,) + (0,) * nd, pipeline_mode=pl.Buffered(1))


def _skip_refs(body, first, count):
    def wrapped(*refs):
        return body(*refs[:first], *refs[first + count:])
    return wrapped


BRANCH_WEIGHTS = ('norm_mix', 'w_xbc', 'w_dt', 'w_uv', 'w_pool', 'conv_w', 'conv_b', 'dt_bias', 'a_log',
                  'd_skip', 'ln_g', 'ln_b')
N_STATE_OUTPUTS = 4


def _branch_prompt(x, sw, layer, prev_states):
    b, l, _ = x.shape
    depth = sw['w_xbc'].shape[0]
    tb = PROMPT_BLOCK
    npair = l // (2 * tb)
    assert l % (2 * tb) == 0 and tb % SSD_CHUNK == 0 and tb >= 2 * SUBLANES
    stacked = [sw[k] for k in BRANCH_WEIGHTS + ('w_sp', 'b_sp', 'pool_w', 'pool_scale')]
    shared = []
    tok = lambda width: pl.BlockSpec((None, 2 * tb, width), lambda i, j: (i, j, 0))

    def next_block(i, j):
        flat = jnp.minimum(i * npair + j + 1, b * npair - 1)
        return flat // npair, (flat % npair) * 2, 0

    per_seq = lambda rows, width: pl.BlockSpec((None, None, rows, width), lambda i, j: (layer, i, 0, 0))
    out_shape = (jax.ShapeDtypeStruct((b, l, SSD_INNER), BF16),
                 jax.ShapeDtypeStruct((b, l, SGU_WIDTH), BF16),
                 jax.ShapeDtypeStruct((b, l, POOL_WIDTH), BF16),
                 jax.ShapeDtypeStruct((depth, b, SUBLANES, SSD_CONV_DIM), F32),
                 jax.ShapeDtypeStruct((depth, b, SSD_INNER, SSD_STATE), F32),
                 jax.ShapeDtypeStruct((depth, b, 2 * SUBLANES, POOL_WIDTH), F32),
                 jax.ShapeDtypeStruct((depth, b, SGU_CHUNK, SGU_WIDTH), F32))
    out_specs = (tok(SSD_INNER), tok(SGU_WIDTH), tok(POOL_WIDTH),
                 per_seq(SUBLANES, SSD_CONV_DIM), per_seq(SSD_INNER, SSD_STATE),
                 per_seq(2 * SUBLANES, POOL_WIDTH), per_seq(SGU_CHUNK, SGU_WIDTH))
    scratch = [pltpu.VMEM((tb, P_WIDTH), F32),
               pltpu.VMEM((tb, P_WIDTH), F32),
               pltpu.VMEM((SSD_STATE, SSD_INNER), F32),
               pltpu.VMEM((SUBLANES, SSD_CONV_DIM), F32),
               pltpu.VMEM((2 * SUBLANES, POOL_WIDTH), F32),
               pltpu.VMEM((tb, SSD_CONV_DIM), F32)]
    body = functools.partial(_branch_prompt_kernel, tb=tb, npair=npair)
    in_specs = ([tok(D_MODEL), pl.BlockSpec((None, tb, D_MODEL), next_block)]
                + [_layer_spec(w, layer) for w in stacked] + [_const_spec(w) for w in shared])
    args = [x, x] + stacked + shared
    aliases = {}
    if prev_states is not None:
        body = _skip_refs(body, len(args), N_STATE_OUTPUTS)
        aliases = {len(args) + k: 3 + k for k in range(N_STATE_OUTPUTS)}
        in_specs = in_specs + [pl.BlockSpec(memory_space=pl.ANY)] * N_STATE_OUTPUTS
        args = args + list(prev_states)
    ya, yb, yc, *states = pl.pallas_call(
        body, grid=(b, npair), in_specs=in_specs,
        out_specs=out_specs, out_shape=out_shape, scratch_shapes=scratch,
        input_output_aliases=aliases,
        compiler_params=pltpu.CompilerParams(dimension_semantics=("arbitrary", "arbitrary"),
                                             vmem_limit_bytes=VMEM_LIMIT),
        name="branch_prompt",
    )(*args)
    return ya, yb, yc, states


def _branch_sample_kernel(x_ref, convs_ref, ssms_ref, pools_in_ref, nm_ref, wxbc_ref, wdt_ref, wuv_ref,
                          wpool_ref, convw_ref, convb_ref, dtb_ref, alog_ref, dskip_ref,
                          lng_ref, lnb_ref, wd_ref, b8_ref, poolw_ref, pools_ref, e2_ref,
                          ya_ref, yb_ref, yc_ref, convo_ref, ssmo_ref, poolo_ref, vo_ref,
                          c_ref, bm_ref, xd_ref, daug_ref, yoff_ref, *, nb, seq, start):
    r = nb * seq
    assert seq == SUBLANES
    t128 = _row_iota((r, LANES)) % seq

    def tile_roll(v, j):
        width = v.shape[-1]
        return pltpu.roll(v.reshape(nb, seq, width), j, 1).reshape(r, width)

    def shift_rows(cur, prev, j, width):
        tt = _row_iota((r, width)) % seq
        if prev is None:
            return jnp.where(tt >= j, tile_roll(cur, j), 0.0)
        return jnp.where(tt >= j, tile_roll(cur, j), tile_roll(prev, j))

    hb = _rmsnorm(x_ref[...].reshape(r, D_MODEL), nm_ref[...]).astype(BF16)

    xbc = _dot(hb, wxbc_ref[...])
    cbuf = convs_ref[...].reshape(r, SSD_CONV_DIM)
    acc = xbc * convw_ref[SSD_CONV - 1:SSD_CONV, :]
    for k in range(SSD_CONV - 1):
        acc = acc + shift_rows(xbc, cbuf, SSD_CONV - 1 - k, SSD_CONV_DIM) * convw_ref[k:k + 1, :]
    xcv = _silu(acc + convb_ref[...])
    convo_ref[...] = xbc.reshape(nb, seq, SSD_CONV_DIM)
    xs = xcv[:, 0:SSD_INNER]
    bm = xcv[:, SSD_INNER:SSD_INNER + SSD_GROUPS * SSD_STATE]
    cm = xcv[:, SSD_INNER + SSD_GROUPS * SSD_STATE:SSD_CONV_DIM]
    dt = _softplus(_dot(hb, wdt_ref[...]) + dtb_ref[...])
    a_row = -jnp.exp(alog_ref[...])
    a_cs = dt * a_row
    for s in (1, 2, 4):
        a_cs = a_cs + jnp.where(t128 >= s, tile_roll(a_cs, s), 0.0)
    a3 = a_cs.reshape(nb, seq, LANES)
    tot = jnp.broadcast_to(a3[:, seq - 1:seq, :], (nb, seq, LANES)).reshape(r, LANES)
    e2 = e2_ref[...]
    exp_a_e = _expand_heads(jnp.exp(a_cs), e2)
    decst_e = _expand_heads(jnp.exp(tot - a_cs), e2)
    dt_e = _expand_heads(dt, e2)
    dectot_e = _expand_heads(jnp.exp(tot), e2)
    xdt = xs * dt_e
    lane = _lane_iota((r, LANES))
    y = jnp.zeros((r, SSD_INNER), F32)
    for j in range(seq):
        if j == 0:
            lj = jnp.ones((r, LANES), F32)
            b_sh, x_sh = bm, xdt
        else:
            lj = jnp.where(t128 >= j, jnp.exp(a_cs - tile_roll(a_cs, j)), 0.0)
            b_sh, x_sh = tile_roll(bm, j), tile_roll(xdt, j)
        prod = cm * b_sh
        cb0 = jnp.sum(prod[:, 0:SSD_STATE], axis=-1, keepdims=True)
        cb1 = jnp.sum(prod[:, SSD_STATE:2 * SSD_STATE], axis=-1, keepdims=True)
        mj = lj * jnp.where(lane < HEADS_PER_GROUP, cb0, cb1)
        y = y + _expand_heads(mj, e2) * x_sh
    c_ref[...] = cm
    bm_ref[...] = bm
    xd_ref[...] = xdt * decst_e
    dec_hi = dectot_e.astype(BF16).astype(F32)
    t1024 = _row_iota((r, SSD_INNER)) % seq
    daug_ref[...] = jnp.where(t1024 == 0, dec_hi, jnp.where(t1024 == 1, dectot_e - dec_hi, 0.0))
    t8 = _row_iota((seq, LANES))
    ones2 = jnp.where(t8 < 2, 1.0, 0.0)
    zeros8 = jnp.zeros((seq, LANES), F32)

    def per_seq(n, carry):
        rows = pl.ds(pl.multiple_of(n * seq, seq), seq)
        for g in range(SSD_GROUPS):
            cols = slice(g * GROUP_INNER, (g + 1) * GROUP_INNER)
            scol = slice(g * SSD_STATE, (g + 1) * SSD_STATE)
            h0 = ssms_ref[n, cols, :]
            cg = c_ref[rows, scol].astype(BF16)
            yoff_ref[rows, cols] = _dot_nt(cg, h0.astype(BF16))
            lhs = jnp.concatenate([xd_ref[rows, cols], daug_ref[rows, cols]], axis=0).astype(BF16)
            rhs = jnp.concatenate([jnp.concatenate([bm_ref[rows, scol], zeros8], axis=1),
                                   jnp.concatenate([zeros8, ones2], axis=1)], axis=0).astype(BF16)
            upd = _dot_tn(lhs, rhs)
            ssmo_ref[n, cols, :] = upd[:, SSD_STATE:] * h0 + upd[:, :SSD_STATE]
        return carry

    lax.fori_loop(0, nb, per_seq, 0, unroll=SAMPLE_UNROLL)
    ya_ref[...] = (y + yoff_ref[...] * exp_a_e + dskip_ref[...] * xs).astype(BF16)

    a = _gelu_tanh(_dot(hb, wuv_ref[...]))
    u = a[:, :SGU_WIDTH]
    v = a[:, SGU_WIDTH:]
    mu = jnp.mean(v, axis=-1, keepdims=True)
    vc = v - mu
    var = jnp.mean(vc * vc, axis=-1, keepdims=True)
    vn = vc * lax.rsqrt(var + EPS) * lng_ref[...] + lnb_ref[...]
    vo_ref[...] = vn
    s = vn.reshape(nb, seq, SGU_WIDTH) * wd_ref[0] + b8_ref[...]
    for j in range(1, seq):
        s = s + tile_roll(vn, j).reshape(nb, seq, SGU_WIDTH) * wd_ref[j]
    yb_ref[...] = (u * s.reshape(r, SGU_WIDTH)).astype(BF16)

    xc = _dot(hb, wpool_ref[...])
    pbuf = pools_in_ref[...]
    t0 = pbuf[:, 0:seq, :].reshape(r, POOL_WIDTH)
    t1 = pbuf[:, seq:2 * seq, :].reshape(r, POOL_WIDTH)
    poolo_ref[:, 0:seq, :] = pbuf[:, seq:2 * seq, :]
    poolo_ref[:, seq:2 * seq, :] = xc.reshape(nb, seq, POOL_WIDTH)
    pos = start + (_row_iota((r, POOL_GW)) % seq)
    yc_cols = []
    for g, w in enumerate(POOL_WINDOWS):
        cols = slice(g * POOL_GW, (g + 1) * POOL_GW)
        tiles = [t0[:, cols], t1[:, cols], xc[:, cols]]
        span = 1
        while span < min(w, seq):
            prev = [None] + tiles[:-1]
            tiles = [tl + shift_rows(tl, pv, span, POOL_GW) for tl, pv in zip(tiles, prev)]
            span *= 2
        wsum = tiles[2] if w <= seq else tiles[2] + tiles[1]
        cnt = jnp.minimum(pos + 1, w).astype(F32)
        d = wsum / cnt - xc[:, cols]
        yc_cols.append(_dot(d.astype(BF16), poolw_ref[g]))
    yc_ref[...] = (jnp.concatenate(yc_cols, axis=1) * pools_ref[...]).astype(BF16)


def _branch_sample(x, conv_pad, ssm_flat, pool_pad, sw, layer, prev_states, start):
    n, seq, _ = x.shape
    depth = sw['w_xbc'].shape[0]
    nb = SAMPLE_SEQS
    assert n % nb == 0 and seq == SUBLANES and start >= max(POOL_WINDOWS)
    r = nb * seq
    stacked = [sw[k] for k in BRANCH_WEIGHTS + ('w_diag8', 'b8', 'pool_w', 'pool_scale')]
    shared = [sw['e2']]
    seq3 = lambda rows, width: pl.BlockSpec((nb, rows, width), lambda i: (i, 0, 0))
    seq4 = lambda rows, width: pl.BlockSpec((None, nb, rows, width), lambda i: (layer, i, 0, 0))
    tok = lambda width: pl.BlockSpec((r, width), lambda i: (i, 0))
    out_shape = (jax.ShapeDtypeStruct((n * seq, SSD_INNER), BF16),
                 jax.ShapeDtypeStruct((n * seq, SGU_WIDTH), BF16),
                 jax.ShapeDtypeStruct((n * seq, POOL_WIDTH), BF16),
                 jax.ShapeDtypeStruct((depth, n, seq, SSD_CONV_DIM), F32),
                 jax.ShapeDtypeStruct((depth, n, SSD_INNER, SSD_STATE), F32),
                 jax.ShapeDtypeStruct((depth, n, 2 * seq, POOL_WIDTH), F32),
                 jax.ShapeDtypeStruct((depth, n * seq, SGU_WIDTH), F32))
    out_specs = (tok(SSD_INNER), tok(SGU_WIDTH), tok(POOL_WIDTH), seq4(seq, SSD_CONV_DIM),
                 seq4(SSD_INNER, SSD_STATE), seq4(2 * seq, POOL_WIDTH),
                 pl.BlockSpec((None, r, SGU_WIDTH), lambda i: (layer, i, 0)))
    scratch = [pltpu.VMEM((r, SSD_GROUPS * SSD_STATE), F32),
               pltpu.VMEM((r, SSD_GROUPS * SSD_STATE), F32),
               pltpu.VMEM((r, SSD_INNER), F32),
               pltpu.VMEM((r, SSD_INNER), F32),
               pltpu.VMEM((r, SSD_INNER), F32)]
    body = functools.partial(_branch_sample_kernel, nb=nb, seq=seq, start=start)
    in_specs = ([seq3(seq, D_MODEL), seq4(seq, SSD_CONV_DIM), seq4(SSD_INNER, SSD_STATE), seq4(2 * seq, POOL_WIDTH)]
                + [_layer_spec(w, layer) for w in stacked] + [_const_spec(w) for w in shared])
    args = [x, conv_pad, ssm_flat, pool_pad] + stacked + shared
    aliases = {}
    if prev_states is not None:
        body = _skip_refs(body, len(args), N_STATE_OUTPUTS)
        aliases = {len(args) + k: 3 + k for k in range(N_STATE_OUTPUTS)}
        in_specs = in_specs + [pl.BlockSpec(memory_space=pl.ANY)] * N_STATE_OUTPUTS
        args = args + list(prev_states)
    ya, yb, yc, *states = pl.pallas_call(
        body, grid=(n // nb,), in_specs=in_specs,
        out_specs=out_specs, out_shape=out_shape, scratch_shapes=scratch,
        input_output_aliases=aliases,
        compiler_params=pltpu.CompilerParams(dimension_semantics=("arbitrary",),
                                             vmem_limit_bytes=VMEM_LIMIT),
        name="branch_sample",
    )(*args)
    return ya, yb, yc, states


def _merge_kernel(x_ref, ya_ref, yb_ref, yc_ref, nm_ref, wz_ref, ssdn_ref, wg_ref, wa_ref, wb_ref, wc_ref, wo_ref,
                  o_ref):
    x = x_ref[...]
    hb = _rmsnorm(x, nm_ref[...]).astype(BF16)
    ya = _group_rmsnorm(ya_ref[...].astype(F32) * _silu(_dot(hb, wz_ref[...])), ssdn_ref[...]).astype(BF16)
    m = _sigmoid(_dot(hb, wg_ref[:, 0:D_MODEL])) * _dot(ya, wa_ref[...])
    m = m + _sigmoid(_dot(hb, wg_ref[:, D_MODEL:2 * D_MODEL])) * _dot(yb_ref[...], wb_ref[...])
    m = m + _sigmoid(_dot(hb, wg_ref[:, 2 * D_MODEL:3 * D_MODEL])) * _dot(yc_ref[...], wc_ref[...])
    o_ref[...] = x + _dot(m.astype(BF16), wo_ref[...])


def _merge(x, ya, yb, yc, sw, layer):
    t = x.shape[0]
    tm = min(TOKEN_BLOCK, t)
    assert t % tm == 0
    stacked = [sw[k] for k in ('norm_mix', 'w_z', 'ssd_norm', 'w_gate', 'w_br_a', 'w_br_b', 'w_br_c', 'w_out')]
    tok = lambda width: pl.BlockSpec((tm, width), lambda i: (i, 0))
    return pl.pallas_call(
        _merge_kernel, grid=(t // tm,),
        in_specs=[tok(D_MODEL), tok(SSD_INNER), tok(SGU_WIDTH), tok(POOL_WIDTH)] + [_layer_spec(w, layer) for w in stacked],
        out_specs=tok(D_MODEL), out_shape=jax.ShapeDtypeStruct((t, D_MODEL), F32),
        compiler_params=pltpu.CompilerParams(dimension_semantics=("arbitrary",), vmem_limit_bytes=VMEM_LIMIT),
        name="merge",
    )(x, ya, yb, yc, *stacked)


def _ffn_kernel(x_ref, p_ref, nf_ref, wgu_ref, wd_ref, np_ref, wpg_ref, wpu_ref, fn_ref, o_ref, *, final):
    for r0 in range(0, x_ref.shape[0], TOKEN_BLOCK):
        rows = slice(r0, r0 + TOKEN_BLOCK)
        x = x_ref[rows, :]
        hb = _rmsnorm(x, nf_ref[...]).astype(BF16)
        acc = x
        for cols in (slice(0, FF_SPLIT), slice(FF_SPLIT, D_FF)):
            up_cols = slice(D_FF + cols.start, D_FF + cols.stop)
            act = (_silu(_dot(hb, wgu_ref[:, cols])) * _dot(hb, wgu_ref[:, up_cols])).astype(BF16)
            acc = acc + _dot(act, wd_ref[cols, :])
        hb = _rmsnorm(acc, np_ref[...]).astype(BF16)
        out = acc + _dot(p_ref[rows, :].astype(BF16), wpu_ref[...]) * _sigmoid(_dot(hb, wpg_ref[...]))
        if final:
            out = _rmsnorm(out, fn_ref[...])
        o_ref[rows, :] = out


def _ffn(x, p, sw, layer, final):
    t = x.shape[0]
    tm = min(FFN_STEP_ROWS, t)
    assert t % tm == 0 and tm % TOKEN_BLOCK == 0
    stacked = [sw[k] for k in ('norm_ffn', 'w_gate_up', 'w_down', 'norm_ple', 'w_ple_gate', 'w_ple_up')]
    tok = lambda width: pl.BlockSpec((tm, width), lambda i: (i, 0))
    return pl.pallas_call(
        functools.partial(_ffn_kernel, final=final), grid=(t // tm,),
        in_specs=([tok(D_MODEL), pl.BlockSpec((None, tm, PLE_DIM), lambda i: (layer, i, 0))]
                  + [_layer_spec(w, layer) for w in stacked] + [_const_spec(sw['final_norm'])]),
        out_specs=tok(D_MODEL), out_shape=jax.ShapeDtypeStruct((t, D_MODEL), F32),
        compiler_params=pltpu.CompilerParams(dimension_semantics=("arbitrary",), vmem_limit_bytes=VMEM_LIMIT),
        name="ffn",
    )(x, p, *stacked, sw['final_norm'])


IN_DIM = O_GATE + 3 * D_MODEL
W_IN_ROWS = 256


def _split_w_in_kernel(wt_ref, z_ref, xbc_ref, dt_ref, uv_ref, pool_ref, gate_ref):
    z_ref[...] = wt_ref[O_Z:O_XBC, :].T.astype(BF16)
    xbc_ref[...] = wt_ref[O_XBC:O_DT, :].T.astype(BF16)
    dt = wt_ref[O_DT:O_DT + LANES, :].T
    dt_ref[...] = jnp.where(_lane_iota(dt.shape) < SSD_HEADS, dt, 0.0).astype(BF16)
    uv_ref[...] = wt_ref[O_UV:O_POOL, :].T.astype(BF16)
    pool_ref[...] = wt_ref[O_POOL:O_GATE, :].T.astype(BF16)
    gate_ref[...] = wt_ref[O_GATE:IN_DIM, :].T.astype(BF16)


def _split_w_in(w_in):
    depth, d, width = w_in.shape
    assert width == IN_DIM and d % W_IN_ROWS == 0
    widths = (O_XBC - O_Z, O_DT - O_XBC, LANES, O_POOL - O_UV, O_GATE - O_POOL, IN_DIM - O_GATE)
    spec = lambda w: pl.BlockSpec((None, W_IN_ROWS, w), lambda i, j: (i, j, 0))
    return pl.pallas_call(
        _split_w_in_kernel, grid=(depth, d // W_IN_ROWS),
        in_specs=[pl.BlockSpec((None, IN_DIM, W_IN_ROWS), lambda i, j: (i, 0, j))],
        out_specs=tuple(spec(w) for w in widths),
        out_shape=tuple(jax.ShapeDtypeStruct((depth, d, w), BF16) for w in widths),
        compiler_params=pltpu.CompilerParams(dimension_semantics=("arbitrary", "arbitrary"),
                                             vmem_limit_bytes=VMEM_LIMIT),
        name="split_w_in",
    )(jnp.swapaxes(w_in, 1, 2))


def _head_expansion():
    rows = jnp.arange(2 * LANES)[:, None] % LANES
    cols = jnp.arange(SSD_INNER)[None, :] // SSD_HEAD_DIM
    e2 = (rows == cols).astype(BF16)
    return e2, e2.T


def _prepare_weights(norm_mix, w_in, conv_w, conv_b, dt_bias, a_log, d_skip, ssd_norm, sgu_ln_g, sgu_ln_b,
                     w_spatial, b_spatial, pool_w, pool_scale, w_br_a, w_br_b, w_br_c, w_out, norm_ffn,
                     w_gate_up, w_down, norm_ple, w_ple_gate, w_ple_up, final_norm):
    depth = w_in.shape[0]
    row = lambda v: v.reshape(depth, 1, -1).astype(F32)
    pad_heads = lambda v: jnp.pad(v.reshape(depth, 1, -1).astype(F32), ((0, 0), (0, 0), (0, LANES - SSD_HEADS)))
    seq = SUBLANES
    w8 = w_spatial[:, :, :seq, :seq]
    tt = jnp.arange(seq)
    lag = tt[None, :] - tt[:, None]
    w_lag = jnp.where(lag >= 0, w8[:, :, tt[None, :], jnp.clip(lag, 0, seq - 1)], 0.0)
    w_diag8 = jnp.repeat(jnp.transpose(w_lag, (0, 2, 3, 1)), SGU_GW, axis=3).astype(F32)
    b_full = jnp.repeat(jnp.transpose(b_spatial, (0, 2, 1)), SGU_GW, axis=2).astype(F32)
    e2, et2 = _head_expansion()
    w_z, w_xbc, w_dt, w_uv, w_pool, w_gate = _split_w_in(w_in)
    return {
        'norm_mix': row(norm_mix),
        'w_z': w_z, 'w_xbc': w_xbc, 'w_dt': w_dt, 'w_uv': w_uv, 'w_pool': w_pool, 'w_gate': w_gate,
        'conv_w': conv_w.astype(F32), 'conv_b': row(conv_b),
        'dt_bias': pad_heads(dt_bias), 'a_log': pad_heads(a_log),
        'd_skip': row(jnp.repeat(d_skip, SSD_HEAD_DIM, axis=1)),
        'ssd_norm': row(ssd_norm),
        'ln_g': row(sgu_ln_g), 'ln_b': row(sgu_ln_b),
        'w_sp': w_spatial.astype(F32), 'b_sp': b_full,
        'w_diag8': w_diag8, 'b8': b_full[:, :seq],
        'pool_w': pool_w.astype(BF16), 'pool_scale': row(pool_scale),
        'w_br_a': w_br_a.astype(BF16), 'w_br_b': w_br_b.astype(BF16), 'w_br_c': w_br_c.astype(BF16),
        'w_out': w_out.astype(BF16),
        'norm_ffn': row(norm_ffn),
        'w_gate_up': w_gate_up.astype(BF16),
        'w_down': w_down.astype(BF16),
        'norm_ple': row(norm_ple),
        'w_ple_gate': w_ple_gate.astype(BF16), 'w_ple_up': w_ple_up.astype(BF16),
        'final_norm': final_norm.reshape(1, -1).astype(F32),
        'e2': e2, 'et2': et2,
    }


def kernel(x_prompt, x_sample, state_conv, state_ssm, state_pool, p_prompt, p_sample, norm_mix, w_in, conv_w, conv_b, dt_bias, a_log, d_skip, ssd_norm, sgu_ln_g, sgu_ln_b, w_spatial, b_spatial, pool_w, pool_scale, w_br_a, w_br_b, w_br_c, w_out, norm_ffn, w_gate_up, w_down, norm_ple, w_ple_gate, w_ple_up, final_norm):
    b, l, _ = x_prompt.shape
    n, s, _ = x_sample.shape
    depth = w_in.shape[0]
    sw = _prepare_weights(norm_mix, w_in, conv_w, conv_b, dt_bias, a_log, d_skip, ssd_norm, sgu_ln_g, sgu_ln_b,
                          w_spatial, b_spatial, pool_w, pool_scale, w_br_a, w_br_b, w_br_c, w_out, norm_ffn,
                          w_gate_up, w_down, norm_ple, w_ple_gate, w_ple_up, final_norm)
    conv_pad = jnp.pad(state_conv, ((0, 0), (0, 0), (SUBLANES - (SSD_CONV - 1), 0), (0, 0)))
    pool_pad = jnp.pad(state_pool, ((0, 0), (0, 0), (2 * SUBLANES - POOL_BUF, 0), (0, 0)))
    ssm_flat = state_ssm.reshape(depth, n, SSD_INNER, SSD_STATE)
    pp = p_prompt.reshape(depth, b * l, PLE_DIM)
    ps = p_sample.reshape(depth, n * s, PLE_DIM)
    xp = x_prompt
    xs = x_sample.reshape(n * s, D_MODEL)
    st_p = None
    st_s = None
    for i in range(depth):
        final = i == depth - 1
        ya, yb, yc, st_p = _branch_prompt(xp, sw, i, st_p)
        x1 = _merge(xp.reshape(b * l, D_MODEL), ya.reshape(b * l, -1), yb.reshape(b * l, -1), yc.reshape(b * l, -1), sw, i)
        xp = _ffn(x1, pp, sw, i, final).reshape(b, l, D_MODEL)
        ya, yb, yc, st_s = _branch_sample(xs.reshape(n, s, D_MODEL), conv_pad, ssm_flat, pool_pad, sw, i, st_s, PAST_LEN)
        x1 = _merge(xs, ya, yb, yc, sw, i)
        xs = _ffn(x1, ps, sw, i, final)
    conv_p, ssm_p, pool_p, v_p = st_p
    conv_s, ssm_s, pool_s, v_s = st_s
    return (xp, xs.reshape(n, s, D_MODEL),
            conv_p[:, :, SUBLANES - (SSD_CONV - 1):, :],
            ssm_p.reshape(depth, b, SSD_HEADS, SSD_HEAD_DIM, SSD_STATE),
            pool_p[:, :, 2 * SUBLANES - POOL_BUF:, :],
            v_p,
            conv_s[:, :, s - (SSD_CONV - 1):, :],
            ssm_s.reshape(depth, n, SSD_HEADS, SSD_HEAD_DIM, SSD_STATE),
            pool_s[:, :, 2 * s - POOL_BUF:, :],
            v_s.reshape(depth, n, s, SGU_WIDTH))
```

```python
import functools
import math

import jax
import jax.numpy as jnp
from jax import lax
from jax.experimental import pallas as pl
from jax.experimental.pallas import tpu as pltpu

F32 = jnp.float32
BF16 = jnp.bfloat16

D_MODEL = 1024
DEPTH = 4
PAST_LEN = 16384
SSD_HEAD_DIM = 64
SSD_HEADS = 16
SSD_INNER = 1024
SSD_GROUPS = 2
SSD_STATE = 128
SSD_CONV = 4
SSD_CHUNK = 128
SSD_CONV_DIM = 1536
SGU_WIDTH = 512
SGU_GROUPS = 4
SGU_CHUNK = 128
SGU_GW = 128
POOL_WIDTH = 512
POOL_WINDOWS = (2, 4, 8, 16)
POOL_GW = 128
POOL_BUF = 15
D_FF = 2816
PLE_DIM = 256
EPS = 1e-6
LOG2_E = 1.4426950408889634
O_Z = 0
O_XBC = 1024
O_DT = 2560
O_UV = 2576
O_POOL = 3600
O_GATE = 4112

LANES = 128
SUBLANES = 8
HEADS_PER_GROUP = SSD_HEADS // SSD_GROUPS
GROUP_INNER = SSD_INNER // SSD_GROUPS
VMEM_LIMIT = 56 * 1024 * 1024

PROMPT_BLOCK = 512
SAMPLE_SEQS = 16
SAMPLE_UNROLL = 8
TOKEN_BLOCK = 512
MXU_DEPTH = 256
FF_SPLIT = (D_FF // MXU_DEPTH + 1) // 2 * MXU_DEPTH


def _dot(a, b):
    return jnp.dot(a, b, preferred_element_type=F32)


def _dot_nt(a, b):
    return lax.dot_general(a, b, (((1,), (1,)), ((), ())), preferred_element_type=F32)


def _dot_tn(a, b):
    return lax.dot_general(a, b, (((0,), (0,)), ((), ())), preferred_element_type=F32)


def _sigmoid(x):
    return 0.5 * jnp.tanh(0.5 * x) + 0.5


def _silu(x):
    h = 0.5 * x
    return h * jnp.tanh(h) + h


def _gelu_tanh(x):
    c = math.sqrt(2.0 / math.pi)
    return 0.5 * x * (1.0 + jnp.tanh(c * (x + 0.044715 * (x * x * x))))


def _softplus(x):
    return jnp.maximum(x, 0.0) + jnp.log1p(jnp.exp(-jnp.abs(x)))


def _rmsnorm(x, g):
    ms = jnp.mean(x * x, axis=-1, keepdims=True)
    return x * lax.rsqrt(ms + EPS) * g


def _split2(v):
    hi = v.astype(BF16)
    lo = (v - hi.astype(F32)).astype(BF16)
    return jnp.concatenate([hi, lo], axis=-1)


def _expand_heads(v, e2):
    return _dot(_split2(v), e2)


def _group_rmsnorm(y, g):
    parts = []
    for k in range(SSD_GROUPS):
        yk = y[:, k * GROUP_INNER:(k + 1) * GROUP_INNER]
        ms = jnp.mean(yk * yk, axis=-1, keepdims=True)
        parts.append(yk * lax.rsqrt(ms + EPS))
    return jnp.concatenate(parts, axis=-1) * g


def _row_iota(shape):
    return lax.broadcasted_iota(jnp.int32, shape, 0)


def _lane_iota(shape):
    return lax.broadcasted_iota(jnp.int32, shape, 1)


def _ssd_chunk(xs_bf, bm, cm, dtc, a_row, ht_ref):
    q = SSD_CHUNK
    tri = _row_iota((q, q)) >= _lane_iota((q, q))
    tril_bf = jnp.where(tri, 1.0, 0.0).astype(BF16)
    da = dtc * a_row
    hi = da.astype(BF16)
    r1 = da - hi.astype(F32)
    mid = r1.astype(BF16)
    lo = (r1 - mid.astype(F32)).astype(BF16)
    cs3 = _dot(tril_bf, jnp.concatenate([hi, mid, lo], axis=1))
    a_cs = (cs3[:, :LANES] + cs3[:, LANES:2 * LANES] + cs3[:, 2 * LANES:]) * LOG2_E
    a_t = a_cs.T
    ap_t = (a_cs - jnp.log2(dtc)).T
    w_t = jnp.exp2(a_t[:, q - 1:q] - ap_t)
    c_bf = cm.astype(BF16)
    lane = _lane_iota((q, LANES))
    ys = []
    for g in range(SSD_GROUPS):
        scol = slice(g * SSD_STATE, (g + 1) * SSD_STATE)
        cg = c_bf[:, scol]
        cb = _dot_nt(cg, bm[:, scol].astype(BF16))
        bg_t = bm[:, scol].T
        ht_g = ht_ref[:, g * GROUP_INNER:(g + 1) * GROUP_INNER]
        yoff = _dot(cg, ht_g.astype(BF16))
        for pr in range(HEADS_PER_GROUP // 2):
            h0 = g * HEADS_PER_GROUP + 2 * pr
            acols, ms, bws = [], [], []
            for h in (h0, h0 + 1):
                acol = jnp.broadcast_to(a_cs[:, h:h + 1], (q, q))
                acols.append(acol)
                ms.append((jnp.where(tri, jnp.exp2(acol - ap_t[h:h + 1, :]), 0.0) * cb).astype(BF16))
                bws.append((bg_t * w_t[h:h + 1, :]).astype(BF16))
            lhs = jnp.concatenate([jnp.concatenate(ms, axis=1), jnp.concatenate(bws, axis=1)], axis=0)
            xpair = xs_bf[:, h0 * SSD_HEAD_DIM:(h0 + 2) * SSD_HEAD_DIM]
            zero = jnp.zeros_like(xpair)
            rhs = jnp.concatenate([jnp.where(lane < SSD_HEAD_DIM, xpair, zero),
                                   jnp.where(lane >= SSD_HEAD_DIM, xpair, zero)], axis=0)
            out = _dot(lhs, rhs)
            ea = jnp.exp2(jnp.where(lane < SSD_HEAD_DIM, acols[0], acols[1]))
            ys.append(out[:q] + yoff[:, pr * LANES:(pr + 1) * LANES] * ea)
            cols = slice(h0 * SSD_HEAD_DIM, (h0 + 2) * SSD_HEAD_DIM)
            ht_ref[:, cols] = ht_g[:, pr * LANES:(pr + 1) * LANES] * ea[q - 1:q, :] + out[q:]
    return jnp.concatenate(ys, axis=1)


P_XBC = 0
P_DT = P_XBC + SSD_CONV_DIM
P_UV = P_DT + LANES
P_POOL = P_UV + 2 * SGU_WIDTH
P_WIDTH = P_POOL + POOL_WIDTH
PROJ_PIECE = 512


def _branch_prompt_kernel(xa_ref, xn_ref, nm_ref, wxbc_ref, wdt_ref, wuv_ref, wpool_ref,
                          convw_ref, convb_ref, dtb_ref, alog_ref, dskip_ref,
                          lng_ref, lnb_ref, wsp_ref, bsp_ref, poolw_ref, pools_ref,
                          ya_ref, yb_ref, yc_ref, convo_ref, ssmo_ref, poolo_ref, vo_ref,
                          pa_ref, pb_ref, ht_ref, cc_ref, pc_ref, xcv_ref, *, tb, npair):
    i = pl.program_id(0)
    j = pl.program_id(1)
    nchunk = tb // SSD_CHUNK

    def project_pieces(x, p_ref):
        hb = _rmsnorm(x, nm_ref[...]).astype(BF16)
        pieces = []
        for w_ref, base, width in ((wxbc_ref, P_XBC, SSD_CONV_DIM),
                                   (wuv_ref, P_UV, 2 * SGU_WIDTH), (wpool_ref, P_POOL, POOL_WIDTH),
                                   (wdt_ref, P_DT, LANES)):
            for c0 in range(0, width, PROJ_PIECE):
                c1 = min(c0 + PROJ_PIECE, width)

                def piece(w_ref=w_ref, base=base, c0=c0, c1=c1):
                    p_ref[:, base + c0:base + c1] = _dot(hb, w_ref[:, c0:c1])
                pieces.append(piece)
        return pieces

    def process_phases(p_ref, half):
        orow = slice(half * tb, (half + 1) * tb)
        res = {}
        phases = []

        def conv(cols):
            xp = jnp.concatenate([cc_ref[:, cols], p_ref[:, P_XBC + cols.start:P_XBC + cols.stop]], axis=0)
            acc = xp[SUBLANES:, :] * convw_ref[SSD_CONV - 1:SSD_CONV, cols]
            for k in range(SSD_CONV - 1):
                acc = acc + pltpu.roll(xp, SSD_CONV - 1 - k, 0)[SUBLANES:, :] * convw_ref[k:k + 1, cols]
            xcv_ref[:, cols] = _silu(acc + convb_ref[:, cols])
        for c0 in range(0, SSD_CONV_DIM, PROJ_PIECE):
            phases.append(functools.partial(conv, slice(c0, c0 + PROJ_PIECE)))

        def scan(c):
            rows = slice(c * SSD_CHUNK, (c + 1) * SSD_CHUNK)
            xs = xcv_ref[rows, 0:SSD_INNER]
            y = _ssd_chunk(xs.astype(BF16),
                           xcv_ref[rows, SSD_INNER:SSD_INNER + SSD_GROUPS * SSD_STATE],
                           xcv_ref[rows, SSD_INNER + SSD_GROUPS * SSD_STATE:SSD_CONV_DIM],
                           _softplus(p_ref[rows, P_DT:P_UV] + dtb_ref[...]),
                           -jnp.exp(alog_ref[...]), ht_ref)
            ya_ref[half * tb + c * SSD_CHUNK:half * tb + (c + 1) * SSD_CHUNK, :] = (
                y + dskip_ref[...] * xs).astype(BF16)
            if c == nchunk - 1:
                res['tail'] = p_ref[tb - SUBLANES:tb, P_XBC:P_DT]
                cc_ref[...] = res['tail']
        for c in range(nchunk):
            phases.append(functools.partial(scan, c))

        def spatial():
            u = _gelu_tanh(p_ref[:, P_UV:P_UV + SGU_WIDTH])
            v = _gelu_tanh(p_ref[:, P_UV + SGU_WIDTH:P_POOL])
            mu = jnp.mean(v, axis=-1, keepdims=True)
            vc = v - mu
            var = jnp.mean(vc * vc, axis=-1, keepdims=True)
            vn = vc * lax.rsqrt(var + EPS) * lng_ref[...] + lnb_ref[...]
            res['vn'] = vn
            vn_bf = vn.astype(BF16)
            tri = _row_iota((SGU_CHUNK, SGU_CHUNK)) >= _lane_iota((SGU_CHUNK, SGU_CHUNK))
            s_groups = []
            for g in range(SGU_GROUPS):
                wm = jnp.where(tri, wsp_ref[g], 0.0).astype(BF16)
                vg = jnp.concatenate([vn_bf[c * SGU_CHUNK:(c + 1) * SGU_CHUNK, g * SGU_GW:(g + 1) * SGU_GW]
                                      for c in range(nchunk)], axis=1)
                s_groups.append(_dot(wm, vg))
            s = jnp.concatenate([jnp.concatenate([sg[:, c * SGU_GW:(c + 1) * SGU_GW] for sg in s_groups], axis=1)
                                 + bsp_ref[...] for c in range(nchunk)], axis=0)
            yb_ref[orow, :] = (u * s).astype(BF16)
        phases.append(spatial)

        def pool():
            pos = (2 * j + half) * tb + _row_iota((tb, POOL_GW))
            yc_cols = []
            for g, w in enumerate(POOL_WINDOWS):
                cols = slice(P_POOL + g * POOL_GW, P_POOL + (g + 1) * POOL_GW)
                cur = p_ref[:, cols]
                wsum = jnp.concatenate([pc_ref[:, g * POOL_GW:(g + 1) * POOL_GW], cur], axis=0)
                span = 1
                while span < w:
                    wsum = wsum + pltpu.roll(wsum, span, 0)
                    span *= 2
                cnt = jnp.minimum(pos + 1, w).astype(F32)
                d = wsum[2 * SUBLANES:, :] / cnt - cur
                yc_cols.append(_dot(d.astype(BF16), poolw_ref[g]))
            yc_ref[orow, :] = (jnp.concatenate(yc_cols, axis=1) * pools_ref[...]).astype(BF16)
            res['ptail'] = p_ref[tb - 2 * SUBLANES:tb, P_POOL:P_WIDTH]
            pc_ref[...] = res['ptail']
        phases.append(pool)
        return phases, res

    def run(p_cur, half, x_next, p_next):
        phases, res = process_phases(p_cur, half)
        pieces = project_pieces(x_next, p_next)
        done = 0
        for k, phase in enumerate(phases):
            phase()
            upto = (k + 1) * len(pieces) // len(phases)
            for piece in pieces[done:upto]:
                piece()
            done = upto
        return res

    @pl.when(jnp.logical_and(i == 0, j == 0))
    def _():
        for piece in project_pieces(xa_ref[0:tb, :], pa_ref):
            piece()

    @pl.when(j == 0)
    def _():
        ht_ref[...] = jnp.zeros_like(ht_ref)
        cc_ref[...] = jnp.zeros_like(cc_ref)
        pc_ref[...] = jnp.zeros_like(pc_ref)

    run(pa_ref, 0, xa_ref[tb:2 * tb, :], pb_ref)
    res = run(pb_ref, 1, xn_ref[...], pa_ref)

    @pl.when(j == npair - 1)
    def _():
        convo_ref[...] = res['tail']
        ssmo_ref[...] = ht_ref[...].T
        poolo_ref[...] = res['ptail']
        vo_ref[...] = res['vn'][tb - SGU_CHUNK:tb, :]


def _const_spec(arr):
    nd = arr.ndim
    return pl.BlockSpec(arr.shape, lambda *_: (0,) * nd, pipeline_mode=pl.Buffered(1))


def _layer_spec(arr, layer):
    nd = arr.ndim - 1
    return pl.BlockSpec((None,) + arr.shape[1:], lambda *_: (layer,) + (0,) * nd, pipeline_mode=pl.Buffered(1))


def _skip_refs(body, first, count):
    def wrapped(*refs):
        return body(*refs[:first], *refs[first + count:])
    return wrapped


BRANCH_WEIGHTS = ('norm_mix', 'w_xbc', 'w_dt', 'w_uv', 'w_pool', 'conv_w', 'conv_b', 'dt_bias', 'a_log',
                  'd_skip', 'ln_g', 'ln_b')
N_STATE_OUTPUTS = 4


def _branch_prompt(x, sw, layer, prev_states):
    b, l, _ = x.shape
    depth = sw['w_xbc'].shape[0]
    tb = PROMPT_BLOCK
    npair = l // (2 * tb)
    assert l % (2 * tb) == 0 and tb % SSD_CHUNK == 0 and tb >= 2 * SUBLANES
    stacked = [sw[k] for k in BRANCH_WEIGHTS + ('w_sp', 'b_sp', 'pool_w', 'pool_scale')]
    shared = []
    tok = lambda width: pl.BlockSpec((None, 2 * tb, width), lambda i, j: (i, j, 0))

    def next_block(i, j):
        flat = jnp.minimum(i * npair + j + 1, b * npair - 1)
        return flat // npair, (flat % npair) * 2, 0

    per_seq = lambda rows, width: pl.BlockSpec((None, None, rows, width), lambda i, j: (layer, i, 0, 0))
    out_shape = (jax.ShapeDtypeStruct((b, l, SSD_INNER), BF16),
                 jax.ShapeDtypeStruct((b, l, SGU_WIDTH), BF16),
                 jax.ShapeDtypeStruct((b, l, POOL_WIDTH), BF16),
                 jax.ShapeDtypeStruct((depth, b, SUBLANES, SSD_CONV_DIM), F32),
                 jax.ShapeDtypeStruct((depth, b, SSD_INNER, SSD_STATE), F32),
                 jax.ShapeDtypeStruct((depth, b, 2 * SUBLANES, POOL_WIDTH), F32),
                 jax.ShapeDtypeStruct((depth, b, SGU_CHUNK, SGU_WIDTH), F32))
    out_specs = (tok(SSD_INNER), tok(SGU_WIDTH), tok(POOL_WIDTH),
                 per_seq(SUBLANES, SSD_CONV_DIM), per_seq(SSD_INNER, SSD_STATE),
                 per_seq(2 * SUBLANES, POOL_WIDTH), per_seq(SGU_CHUNK, SGU_WIDTH))
    scratch = [pltpu.VMEM((tb, P_WIDTH), F32),
               pltpu.VMEM((tb, P_WIDTH), F32),
               pltpu.VMEM((SSD_STATE, SSD_INNER), F32),
               pltpu.VMEM((SUBLANES, SSD_CONV_DIM), F32),
               pltpu.VMEM((2 * SUBLANES, POOL_WIDTH), F32),
               pltpu.VMEM((tb, SSD_CONV_DIM), F32)]
    body = functools.partial(_branch_prompt_kernel, tb=tb, npair=npair)
    in_specs = ([tok(D_MODEL), pl.BlockSpec((None, tb, D_MODEL), next_block)]
                + [_layer_spec(w, layer) for w in stacked] + [_const_spec(w) for w in shared])
    args = [x, x] + stacked + shared
    aliases = {}
    if prev_states is not None:
        body = _skip_refs(body, len(args), N_STATE_OUTPUTS)
        aliases = {len(args) + k: 3 + k for k in range(N_STATE_OUTPUTS)}
        in_specs = in_specs + [pl.BlockSpec(memory_space=pl.ANY)] * N_STATE_OUTPUTS
        args = args + list(prev_states)
    ya, yb, yc, *states = pl.pallas_call(
        body, grid=(b, npair), in_specs=in_specs,
        out_specs=out_specs, out_shape=out_shape, scratch_shapes=scratch,
        input_output_aliases=aliases,
        compiler_params=pltpu.CompilerParams(dimension_semantics=("arbitrary", "arbitrary"),
                                             vmem_limit_bytes=VMEM_LIMIT),
        name="branch_prompt",
    )(*args)
    return ya, yb, yc, states


def _branch_sample_kernel(x_ref, convs_ref, ssms_ref, pools_in_ref, nm_ref, wxbc_ref, wdt_ref, wuv_ref,
                          wpool_ref, convw_ref, convb_ref, dtb_ref, alog_ref, dskip_ref,
                          lng_ref, lnb_ref, wd_ref, b8_ref, poolw_ref, pools_ref, e2_ref,
                          ya_ref, yb_ref, yc_ref, convo_ref, ssmo_ref, poolo_ref, vo_ref,
                          c_ref, bm_ref, xd_ref, daug_ref, yoff_ref, *, nb, seq, start):
    r = nb * seq
    assert seq == SUBLANES
    t128 = _row_iota((r, LANES)) % seq

    def tile_roll(v, j):
        width = v.shape[-1]
        return pltpu.roll(v.reshape(nb, seq, width), j, 1).reshape(r, width)

    def shift_rows(cur, prev, j, width):
        tt = _row_iota((r, width)) % seq
        if prev is None:
            return jnp.where(tt >= j, tile_roll(cur, j), 0.0)
        return jnp.where(tt >= j, tile_roll(cur, j), tile_roll(prev, j))

    hb = _rmsnorm(x_ref[...].reshape(r, D_MODEL), nm_ref[...]).astype(BF16)

    xbc = _dot(hb, wxbc_ref[...])
    cbuf = convs_ref[...].reshape(r, SSD_CONV_DIM)
    acc = xbc * convw_ref[SSD_CONV - 1:SSD_CONV, :]
    for k in range(SSD_CONV - 1):
        acc = acc + shift_rows(xbc, cbuf, SSD_CONV - 1 - k, SSD_CONV_DIM) * convw_ref[k:k + 1, :]
    xcv = _silu(acc + convb_ref[...])
    convo_ref[...] = xbc.reshape(nb, seq, SSD_CONV_DIM)
    xs = xcv[:, 0:SSD_INNER]
    bm = xcv[:, SSD_INNER:SSD_INNER + SSD_GROUPS * SSD_STATE]
    cm = xcv[:, SSD_INNER + SSD_GROUPS * SSD_STATE:SSD_CONV_DIM]
    dt = _softplus(_dot(hb, wdt_ref[...]) + dtb_ref[...])
    a_row = -jnp.exp(alog_ref[...])
    a_cs = dt * a_row
    for s in (1, 2, 4):
        a_cs = a_cs + jnp.where(t128 >= s, tile_roll(a_cs, s), 0.0)
    a3 = a_cs.reshape(nb, seq, LANES)
    tot = jnp.broadcast_to(a3[:, seq - 1:seq, :], (nb, seq, LANES)).reshape(r, LANES)
    e2 = e2_ref[...]
    exp_a_e = _expand_heads(jnp.exp(a_cs), e2)
    decst_e = _expand_heads(jnp.exp(tot - a_cs), e2)
    dt_e = _expand_heads(dt, e2)
    dectot_e = _expand_heads(jnp.exp(tot), e2)
    xdt = xs * dt_e
    lane = _lane_iota((r, LANES))
    y = jnp.zeros((r, SSD_INNER), F32)
    for j in range(seq):
        if j == 0:
            lj = jnp.ones((r, LANES), F32)
            b_sh, x_sh = bm, xdt
        else:
            lj = jnp.where(t128 >= j, jnp.exp(a_cs - tile_roll(a_cs, j)), 0.0)
            b_sh, x_sh = tile_roll(bm, j), tile_roll(xdt, j)
        prod = cm * b_sh
        cb0 = jnp.sum(prod[:, 0:SSD_STATE], axis=-1, keepdims=True)
        cb1 = jnp.sum(prod[:, SSD_STATE:2 * SSD_STATE], axis=-1, keepdims=True)
        mj = lj * jnp.where(lane < HEADS_PER_GROUP, cb0, cb1)
        y = y + _expand_heads(mj, e2) * x_sh
    c_ref[...] = cm
    bm_ref[...] = bm
    xd_ref[...] = xdt * decst_e
    dec_hi = dectot_e.astype(BF16).astype(F32)
    t1024 = _row_iota((r, SSD_INNER)) % seq
    daug_ref[...] = jnp.where(t1024 == 0, dec_hi, jnp.where(t1024 == 1, dectot_e - dec_hi, 0.0))
    t8 = _row_iota((seq, LANES))
    ones2 = jnp.where(t8 < 2, 1.0, 0.0)
    zeros8 = jnp.zeros((seq, LANES), F32)

    def per_seq(n, carry):
        rows = pl.ds(pl.multiple_of(n * seq, seq), seq)
        for g in range(SSD_GROUPS):
            cols = slice(g * GROUP_INNER, (g + 1) * GROUP_INNER)
            scol = slice(g * SSD_STATE, (g + 1) * SSD_STATE)
            h0 = ssms_ref[n, cols, :]
            cg = c_ref[rows, scol].astype(BF16)
            yoff_ref[rows, cols] = _dot_nt(cg, h0.astype(BF16))
            lhs = jnp.concatenate([xd_ref[rows, cols], daug_ref[rows, cols]], axis=0).astype(BF16)
            rhs = jnp.concatenate([jnp.concatenate([bm_ref[rows, scol], zeros8], axis=1),
                                   jnp.concatenate([zeros8, ones2], axis=1)], axis=0).astype(BF16)
            upd = _dot_tn(lhs, rhs)
            ssmo_ref[n, cols, :] = upd[:, SSD_STATE:] * h0 + upd[:, :SSD_STATE]
        return carry

    lax.fori_loop(0, nb, per_seq, 0, unroll=SAMPLE_UNROLL)
    ya_ref[...] = (y + yoff_ref[...] * exp_a_e + dskip_ref[...] * xs).astype(BF16)

    a = _gelu_tanh(_dot(hb, wuv_ref[...]))
    u = a[:, :SGU_WIDTH]
    v = a[:, SGU_WIDTH:]
    mu = jnp.mean(v, axis=-1, keepdims=True)
    vc = v - mu
    var = jnp.mean(vc * vc, axis=-1, keepdims=True)
    vn = vc * lax.rsqrt(var + EPS) * lng_ref[...] + lnb_ref[...]
    vo_ref[...] = vn
    s = vn.reshape(nb, seq, SGU_WIDTH) * wd_ref[0] + b8_ref[...]
    for j in range(1, seq):
        s = s + tile_roll(vn, j).reshape(nb, seq, SGU_WIDTH) * wd_ref[j]
    yb_ref[...] = (u * s.reshape(r, SGU_WIDTH)).astype(BF16)

    xc = _dot(hb, wpool_ref[...])
    pbuf = pools_in_ref[...]
    t0 = pbuf[:, 0:seq, :].reshape(r, POOL_WIDTH)
    t1 = pbuf[:, seq:2 * seq, :].reshape(r, POOL_WIDTH)
    poolo_ref[:, 0:seq, :] = pbuf[:, seq:2 * seq, :]
    poolo_ref[:, seq:2 * seq, :] = xc.reshape(nb, seq, POOL_WIDTH)
    pos = start + (_row_iota((r, POOL_GW)) % seq)
    yc_cols = []
    for g, w in enumerate(POOL_WINDOWS):
        cols = slice(g * POOL_GW, (g + 1) * POOL_GW)
        tiles = [t0[:, cols], t1[:, cols], xc[:, cols]]
        span = 1
        while span < min(w, seq):
            prev = [None] + tiles[:-1]
            tiles = [tl + shift_rows(tl, pv, span, POOL_GW) for tl, pv in zip(tiles, prev)]
            span *= 2
        wsum = tiles[2] if w <= seq else tiles[2] + tiles[1]
        cnt = jnp.minimum(pos + 1, w).astype(F32)
        d = wsum / cnt - xc[:, cols]
        yc_cols.append(_dot(d.astype(BF16), poolw_ref[g]))
    yc_ref[...] = (jnp.concatenate(yc_cols, axis=1) * pools_ref[...]).astype(BF16)


def _branch_sample(x, conv_pad, ssm_flat, pool_pad, sw, layer, prev_states, start):
    n, seq, _ = x.shape
    depth = sw['w_xbc'].shape[0]
    nb = SAMPLE_SEQS
    assert n % nb == 0 and seq == SUBLANES and start >= max(POOL_WINDOWS)
    r = nb * seq
    stacked = [sw[k] for k in BRANCH_WEIGHTS + ('w_diag8', 'b8', 'pool_w', 'pool_scale')]
    shared = [sw['e2']]
    seq3 = lambda rows, width: pl.BlockSpec((nb, rows, width), lambda i: (i, 0, 0))
    seq4 = lambda rows, width: pl.BlockSpec((None, nb, rows, width), lambda i: (layer, i, 0, 0))
    tok = lambda width: pl.BlockSpec((r, width), lambda i: (i, 0))
    out_shape = (jax.ShapeDtypeStruct((n * seq, SSD_INNER), BF16),
                 jax.ShapeDtypeStruct((n * seq, SGU_WIDTH), BF16),
                 jax.ShapeDtypeStruct((n * seq, POOL_WIDTH), BF16),
                 jax.ShapeDtypeStruct((depth, n, seq, SSD_CONV_DIM), F32),
                 jax.ShapeDtypeStruct((depth, n, SSD_INNER, SSD_STATE), F32),
                 jax.ShapeDtypeStruct((depth, n, 2 * seq, POOL_WIDTH), F32),
                 jax.ShapeDtypeStruct((depth, n * seq, SGU_WIDTH), F32))
    out_specs = (tok(SSD_INNER), tok(SGU_WIDTH), tok(POOL_WIDTH), seq4(seq, SSD_CONV_DIM),
                 seq4(SSD_INNER, SSD_STATE), seq4(2 * seq, POOL_WIDTH),
                 pl.BlockSpec((None, r, SGU_WIDTH), lambda i: (layer, i, 0)))
    scratch = [pltpu.VMEM((r, SSD_GROUPS * SSD_STATE), F32),
               pltpu.VMEM((r, SSD_GROUPS * SSD_STATE), F32),
               pltpu.VMEM((r, SSD_INNER), F32),
               pltpu.VMEM((r, SSD_INNER), F32),
               pltpu.VMEM((r, SSD_INNER), F32)]
    body = functools.partial(_branch_sample_kernel, nb=nb, seq=seq, start=start)
    in_specs = ([seq3(seq, D_MODEL), seq4(seq, SSD_CONV_DIM), seq4(SSD_INNER, SSD_STATE), seq4(2 * seq, POOL_WIDTH)]
                + [_layer_spec(w, layer) for w in stacked] + [_const_spec(w) for w in shared])
    args = [x, conv_pad, ssm_flat, pool_pad] + stacked + shared
    aliases = {}
    if prev_states is not None:
        body = _skip_refs(body, len(args), N_STATE_OUTPUTS)
        aliases = {len(args) + k: 3 + k for k in range(N_STATE_OUTPUTS)}
        in_specs = in_specs + [pl.BlockSpec(memory_space=pl.ANY)] * N_STATE_OUTPUTS
        args = args + list(prev_states)
    ya, yb, yc, *states = pl.pallas_call(
        body, grid=(n // nb,), in_specs=in_specs,
        out_specs=out_specs, out_shape=out_shape, scratch_shapes=scratch,
        input_output_aliases=aliases,
        compiler_params=pltpu.CompilerParams(dimension_semantics=("arbitrary",),
                                             vmem_limit_bytes=VMEM_LIMIT),
        name="branch_sample",
    )(*args)
    return ya, yb, yc, states


def _group_specs(tm, steps_p, lead=()):
    head = (None,) * len(lead)
    prompt = lambda width: pl.BlockSpec(head + (tm, width), lambda i: lead + (jnp.minimum(i, steps_p - 1), 0))
    sample = lambda width: pl.BlockSpec(head + (tm, width), lambda i: lead + (jnp.maximum(i - steps_p, 0), 0))
    return prompt, sample


def _store_group(is_sample, out, op_ref, os_ref):
    @pl.when(jnp.logical_not(is_sample))
    def _():
        op_ref[...] = out

    @pl.when(is_sample)
    def _():
        os_ref[...] = out


def _merge_kernel(xp_ref, yap_ref, ybp_ref, ycp_ref, xs_ref, yas_ref, ybs_ref, ycs_ref,
                  nm_ref, wz_ref, ssdn_ref, wg_ref, wa_ref, wb_ref, wc_ref, wo_ref, op_ref, os_ref, *, steps_p):
    is_sample = pl.program_id(0) >= steps_p
    pick = lambda p_ref, s_ref: jnp.where(is_sample, s_ref[...], p_ref[...])
    x = pick(xp_ref, xs_ref)
    hb = _rmsnorm(x, nm_ref[...]).astype(BF16)
    ya = pick(yap_ref, yas_ref).astype(F32) * _silu(_dot(hb, wz_ref[...]))
    ya = _group_rmsnorm(ya, ssdn_ref[...]).astype(BF16)
    m = _sigmoid(_dot(hb, wg_ref[:, 0:D_MODEL])) * _dot(ya, wa_ref[...])
    m = m + _sigmoid(_dot(hb, wg_ref[:, D_MODEL:2 * D_MODEL])) * _dot(pick(ybp_ref, ybs_ref), wb_ref[...])
    m = m + _sigmoid(_dot(hb, wg_ref[:, 2 * D_MODEL:3 * D_MODEL])) * _dot(pick(ycp_ref, ycs_ref), wc_ref[...])
    _store_group(is_sample, x + _dot(m.astype(BF16), wo_ref[...]), op_ref, os_ref)


def _merge(xp, yp, xs, ys, sw, layer):
    tm = TOKEN_BLOCK
    tp, ts = xp.shape[0], xs.shape[0]
    assert tp % tm == 0 and ts % tm == 0
    steps_p = tp // tm
    stacked = [sw[k] for k in ('norm_mix', 'w_z', 'ssd_norm', 'w_gate', 'w_br_a', 'w_br_b', 'w_br_c', 'w_out')]
    prompt, sample = _group_specs(tm, steps_p)
    widths = (D_MODEL, SSD_INNER, SGU_WIDTH, POOL_WIDTH)
    return pl.pallas_call(
        functools.partial(_merge_kernel, steps_p=steps_p), grid=(steps_p + ts // tm,),
        in_specs=([prompt(w) for w in widths] + [sample(w) for w in widths]
                  + [_layer_spec(w, layer) for w in stacked]),
        out_specs=(prompt(D_MODEL), sample(D_MODEL)),
        out_shape=(jax.ShapeDtypeStruct((tp, D_MODEL), F32), jax.ShapeDtypeStruct((ts, D_MODEL), F32)),
        compiler_params=pltpu.CompilerParams(dimension_semantics=("arbitrary",), vmem_limit_bytes=VMEM_LIMIT),
        name="merge",
    )(xp, *yp, xs, *ys, *stacked)


def _ffn_kernel(xp_ref, pp_ref, xs_ref, ps_ref, nf_ref, wgu_ref, wd_ref, np_ref, wpg_ref, wpu_ref, fn_ref,
                op_ref, os_ref, *, final, steps_p):
    is_sample = pl.program_id(0) >= steps_p
    x = jnp.where(is_sample, xs_ref[...], xp_ref[...])
    p = jnp.where(is_sample, ps_ref[...], pp_ref[...])
    hb = _rmsnorm(x, nf_ref[...]).astype(BF16)
    acc = x
    for cols in (slice(0, FF_SPLIT), slice(FF_SPLIT, D_FF)):
        up_cols = slice(D_FF + cols.start, D_FF + cols.stop)
        act = (_silu(_dot(hb, wgu_ref[:, cols])) * _dot(hb, wgu_ref[:, up_cols])).astype(BF16)
        acc = acc + _dot(act, wd_ref[cols, :])
    hb = _rmsnorm(acc, np_ref[...]).astype(BF16)
    out = acc + _dot(p.astype(BF16), wpu_ref[...]) * _sigmoid(_dot(hb, wpg_ref[...]))
    if final:
        out = _rmsnorm(out, fn_ref[...])
    _store_group(is_sample, out, op_ref, os_ref)


def _ffn(xp, pp, xs, ps, sw, layer, final):
    tm = TOKEN_BLOCK
    tp, ts = xp.shape[0], xs.shape[0]
    assert tp % tm == 0 and ts % tm == 0
    steps_p = tp // tm
    stacked = [sw[k] for k in ('norm_ffn', 'w_gate_up', 'w_down', 'norm_ple', 'w_ple_gate', 'w_ple_up')]
    prompt, sample = _group_specs(tm, steps_p)
    prompt_l, sample_l = _group_specs(tm, steps_p, lead=(layer,))
    return pl.pallas_call(
        functools.partial(_ffn_kernel, final=final, steps_p=steps_p), grid=(steps_p + ts // tm,),
        in_specs=([prompt(D_MODEL), prompt_l(PLE_DIM), sample(D_MODEL), sample_l(PLE_DIM)]
                  + [_layer_spec(w, layer) for w in stacked] + [_const_spec(sw['final_norm'])]),
        out_specs=(prompt(D_MODEL), sample(D_MODEL)),
        out_shape=(jax.ShapeDtypeStruct((tp, D_MODEL), F32), jax.ShapeDtypeStruct((ts, D_MODEL), F32)),
        compiler_params=pltpu.CompilerParams(dimension_semantics=("arbitrary",), vmem_limit_bytes=VMEM_LIMIT),
        name="ffn",
    )(xp, pp, xs, ps, *stacked, sw['final_norm'])


IN_DIM = O_GATE + 3 * D_MODEL
W_IN_ROWS = 256


def _split_w_in_kernel(wt_ref, z_ref, xbc_ref, dt_ref, uv_ref, pool_ref, gate_ref):
    z_ref[...] = wt_ref[O_Z:O_XBC, :].T.astype(BF16)
    xbc_ref[...] = wt_ref[O_XBC:O_DT, :].T.astype(BF16)
    dt = wt_ref[O_DT:O_DT + LANES, :].T
    dt_ref[...] = jnp.where(_lane_iota(dt.shape) < SSD_HEADS, dt, 0.0).astype(BF16)
    uv_ref[...] = wt_ref[O_UV:O_POOL, :].T.astype(BF16)
    pool_ref[...] = wt_ref[O_POOL:O_GATE, :].T.astype(BF16)
    gate_ref[...] = wt_ref[O_GATE:IN_DIM, :].T.astype(BF16)


def _split_w_in(w_in):
    depth, d, width = w_in.shape
    assert width == IN_DIM and d % W_IN_ROWS == 0
    widths = (O_XBC - O_Z, O_DT - O_XBC, LANES, O_POOL - O_UV, O_GATE - O_POOL, IN_DIM - O_GATE)
    spec = lambda w: pl.BlockSpec((None, W_IN_ROWS, w), lambda i, j: (i, j, 0))
    return pl.pallas_call(
        _split_w_in_kernel, grid=(depth, d // W_IN_ROWS),
        in_specs=[pl.BlockSpec((None, IN_DIM, W_IN_ROWS), lambda i, j: (i, 0, j))],
        out_specs=tuple(spec(w) for w in widths),
        out_shape=tuple(jax.ShapeDtypeStruct((depth, d, w), BF16) for w in widths),
        compiler_params=pltpu.CompilerParams(dimension_semantics=("arbitrary", "arbitrary"),
                                             vmem_limit_bytes=VMEM_LIMIT),
        name="split_w_in",
    )(jnp.swapaxes(w_in, 1, 2))


def _head_expansion():
    rows = jnp.arange(2 * LANES)[:, None] % LANES
    cols = jnp.arange(SSD_INNER)[None, :] // SSD_HEAD_DIM
    e2 = (rows == cols).astype(BF16)
    return e2, e2.T


def _prepare_weights(norm_mix, w_in, conv_w, conv_b, dt_bias, a_log, d_skip, ssd_norm, sgu_ln_g, sgu_ln_b,
                     w_spatial, b_spatial, pool_w, pool_scale, w_br_a, w_br_b, w_br_c, w_out, norm_ffn,
                     w_gate_up, w_down, norm_ple, w_ple_gate, w_ple_up, final_norm):
    depth = w_in.shape[0]
    row = lambda v: v.reshape(depth, 1, -1).astype(F32)
    pad_heads = lambda v: jnp.pad(v.reshape(depth, 1, -1).astype(F32), ((0, 0), (0, 0), (0, LANES - SSD_HEADS)))
    seq = SUBLANES
    w8 = w_spatial[:, :, :seq, :seq]
    tt = jnp.arange(seq)
    lag = tt[None, :] - tt[:, None]
    w_lag = jnp.where(lag >= 0, w8[:, :, tt[None, :], jnp.clip(lag, 0, seq - 1)], 0.0)
    w_diag8 = jnp.repeat(jnp.transpose(w_lag, (0, 2, 3, 1)), SGU_GW, axis=3).astype(F32)
    b_full = jnp.repeat(jnp.transpose(b_spatial, (0, 2, 1)), SGU_GW, axis=2).astype(F32)
    e2, et2 = _head_expansion()
    w_z, w_xbc, w_dt, w_uv, w_pool, w_gate = _split_w_in(w_in)
    return {
        'norm_mix': row(norm_mix),
        'w_z': w_z, 'w_xbc': w_xbc, 'w_dt': w_dt, 'w_uv': w_uv, 'w_pool': w_pool, 'w_gate': w_gate,
        'conv_w': conv_w.astype(F32), 'conv_b': row(conv_b),
        'dt_bias': pad_heads(dt_bias), 'a_log': pad_heads(a_log),
        'd_skip': row(jnp.repeat(d_skip, SSD_HEAD_DIM, axis=1)),
        'ssd_norm': row(ssd_norm),
        'ln_g': row(sgu_ln_g), 'ln_b': row(sgu_ln_b),
        'w_sp': w_spatial.astype(F32), 'b_sp': b_full,
        'w_diag8': w_diag8, 'b8': b_full[:, :seq],
        'pool_w': pool_w.astype(BF16), 'pool_scale': row(pool_scale),
        'w_br_a': w_br_a.astype(BF16), 'w_br_b': w_br_b.astype(BF16), 'w_br_c': w_br_c.astype(BF16),
        'w_out': w_out.astype(BF16),
        'norm_ffn': row(norm_ffn),
        'w_gate_up': w_gate_up.astype(BF16),
        'w_down': w_down.astype(BF16),
        'norm_ple': row(norm_ple),
        'w_ple_gate': w_ple_gate.astype(BF16), 'w_ple_up': w_ple_up.astype(BF16),
        'final_norm': final_norm.reshape(1, -1).astype(F32),
        'e2': e2, 'et2': et2,
    }


def kernel(x_prompt, x_sample, state_conv, state_ssm, state_pool, p_prompt, p_sample, norm_mix, w_in, conv_w, conv_b, dt_bias, a_log, d_skip, ssd_norm, sgu_ln_g, sgu_ln_b, w_spatial, b_spatial, pool_w, pool_scale, w_br_a, w_br_b, w_br_c, w_out, norm_ffn, w_gate_up, w_down, norm_ple, w_ple_gate, w_ple_up, final_norm):
    b, l, _ = x_prompt.shape
    n, s, _ = x_sample.shape
    depth = w_in.shape[0]
    sw = _prepare_weights(norm_mix, w_in, conv_w, conv_b, dt_bias, a_log, d_skip, ssd_norm, sgu_ln_g, sgu_ln_b,
                          w_spatial, b_spatial, pool_w, pool_scale, w_br_a, w_br_b, w_br_c, w_out, norm_ffn,
                          w_gate_up, w_down, norm_ple, w_ple_gate, w_ple_up, final_norm)
    conv_pad = jnp.pad(state_conv, ((0, 0), (0, 0), (SUBLANES - (SSD_CONV - 1), 0), (0, 0)))
    pool_pad = jnp.pad(state_pool, ((0, 0), (0, 0), (2 * SUBLANES - POOL_BUF, 0), (0, 0)))
    ssm_flat = state_ssm.reshape(depth, n, SSD_INNER, SSD_STATE)
    pp = p_prompt.reshape(depth, b * l, PLE_DIM)
    ps = p_sample.reshape(depth, n * s, PLE_DIM)
    xp = x_prompt.reshape(b * l, D_MODEL)
    xs = x_sample.reshape(n * s, D_MODEL)
    st_p = None
    st_s = None
    for i in range(depth):
        *yp, st_p = _branch_prompt(xp.reshape(b, l, D_MODEL), sw, i, st_p)
        *ys, st_s = _branch_sample(xs.reshape(n, s, D_MODEL), conv_pad, ssm_flat, pool_pad, sw, i, st_s, PAST_LEN)
        yp = [y.reshape(b * l, -1) for y in yp]
        xp, xs = _merge(xp, yp, xs, ys, sw, i)
        xp, xs = _ffn(xp, pp, xs, ps, sw, i, final=i == depth - 1)
    conv_p, ssm_p, pool_p, v_p = st_p
    conv_s, ssm_s, pool_s, v_s = st_s
    return (xp.reshape(b, l, D_MODEL), xs.reshape(n, s, D_MODEL),
            conv_p[:, :, SUBLANES - (SSD_CONV - 1):, :],
            ssm_p.reshape(depth, b, SSD_HEADS, SSD_HEAD_DIM, SSD_STATE),
            pool_p[:, :, 2 * SUBLANES - POOL_BUF:, :],
            v_p,
            conv_s[:, :, s - (SSD_CONV - 1):, :],
            ssm_s.reshape(depth, n, SSD_HEADS, SSD_HEAD_DIM, SSD_STATE),
            pool_s[:, :, 2 * s - POOL_BUF:, :],
            v_s.reshape(depth, n, s, SGU_WIDTH))
```

```python
import functools
import math

import jax
import jax.numpy as jnp
from jax import lax
from jax.experimental import pallas as pl
from jax.experimental.pallas import tpu as pltpu

F32 = jnp.float32
BF16 = jnp.bfloat16

D_MODEL = 1024
DEPTH = 4
PAST_LEN = 16384
SSD_HEAD_DIM = 64
SSD_HEADS = 16
SSD_INNER = 1024
SSD_GROUPS = 2
SSD_STATE = 128
SSD_CONV = 4
SSD_CHUNK = 128
SSD_CONV_DIM = 1536
SGU_WIDTH = 512
SGU_GROUPS = 4
SGU_CHUNK = 128
SGU_GW = 128
POOL_WIDTH = 512
POOL_WINDOWS = (2, 4, 8, 16)
POOL_GW = 128
POOL_BUF = 15
D_FF = 2816
PLE_DIM = 256
EPS = 1e-6
LOG2_E = 1.4426950408889634
O_Z = 0
O_XBC = 1024
O_DT = 2560
O_UV = 2576
O_POOL = 3600
O_GATE = 4112

LANES = 128
SUBLANES = 8
HEADS_PER_GROUP = SSD_HEADS // SSD_GROUPS
GROUP_INNER = SSD_INNER // SSD_GROUPS
VMEM_LIMIT = 56 * 1024 * 1024

PROMPT_BLOCK = 256
SAMPLE_SEQS = 16
SAMPLE_UNROLL = 8
TOKEN_BLOCK = 512
MXU_DEPTH = 256
FF_SPLIT = (D_FF // MXU_DEPTH + 1) // 2 * MXU_DEPTH


def _dot(a, b):
    return jnp.dot(a, b, preferred_element_type=F32)


def _dot_nt(a, b):
    return lax.dot_general(a, b, (((1,), (1,)), ((), ())), preferred_element_type=F32)


def _dot_tn(a, b):
    return lax.dot_general(a, b, (((0,), (0,)), ((), ())), preferred_element_type=F32)


def _sigmoid(x):
    return 0.5 * jnp.tanh(0.5 * x) + 0.5


def _silu(x):
    h = 0.5 * x
    return h * jnp.tanh(h) + h


def _gelu_tanh(x):
    c = math.sqrt(2.0 / math.pi)
    return 0.5 * x * (1.0 + jnp.tanh(c * (x + 0.044715 * (x * x * x))))


def _softplus(x):
    return jnp.maximum(x, 0.0) + jnp.log1p(jnp.exp(-jnp.abs(x)))


def _rmsnorm(x, g):
    ms = jnp.mean(x * x, axis=-1, keepdims=True)
    return x * lax.rsqrt(ms + EPS) * g


def _split2(v):
    hi = v.astype(BF16)
    lo = (v - hi.astype(F32)).astype(BF16)
    return jnp.concatenate([hi, lo], axis=-1)


def _expand_heads(v, e2):
    return _dot(_split2(v), e2)


def _group_rmsnorm(y, g):
    parts = []
    for k in range(SSD_GROUPS):
        yk = y[:, k * GROUP_INNER:(k + 1) * GROUP_INNER]
        ms = jnp.mean(yk * yk, axis=-1, keepdims=True)
        parts.append(yk * lax.rsqrt(ms + EPS))
    return jnp.concatenate(parts, axis=-1) * g


def _row_iota(shape):
    return lax.broadcasted_iota(jnp.int32, shape, 0)


def _lane_iota(shape):
    return lax.broadcasted_iota(jnp.int32, shape, 1)


def _ssd_chunk(xs_bf, bm, cm, dtc, a_row, ht_ref):
    q = SSD_CHUNK
    tri = _row_iota((q, q)) >= _lane_iota((q, q))
    tril_bf = jnp.where(tri, 1.0, 0.0).astype(BF16)
    da = dtc * a_row
    hi = da.astype(BF16)
    r1 = da - hi.astype(F32)
    mid = r1.astype(BF16)
    lo = (r1 - mid.astype(F32)).astype(BF16)
    cs3 = _dot(tril_bf, jnp.concatenate([hi, mid, lo], axis=1))
    a_cs = (cs3[:, :LANES] + cs3[:, LANES:2 * LANES] + cs3[:, 2 * LANES:]) * LOG2_E
    a_t = a_cs.T
    ap_t = (a_cs - jnp.log2(dtc)).T
    w_t = jnp.exp2(a_t[:, q - 1:q] - ap_t)
    c_bf = cm.astype(BF16)
    lane = _lane_iota((q, LANES))
    ys = []
    for g in range(SSD_GROUPS):
        scol = slice(g * SSD_STATE, (g + 1) * SSD_STATE)
        cg = c_bf[:, scol]
        cb = _dot_nt(cg, bm[:, scol].astype(BF16))
        bg_t = bm[:, scol].T
        ht_g = ht_ref[:, g * GROUP_INNER:(g + 1) * GROUP_INNER]
        yoff = _dot(cg, ht_g.astype(BF16))
        for pr in range(HEADS_PER_GROUP // 2):
            h0 = g * HEADS_PER_GROUP + 2 * pr
            acols, ms, bws = [], [], []
            for h in (h0, h0 + 1):
                acol = jnp.broadcast_to(a_cs[:, h:h + 1], (q, q))
                acols.append(acol)
                ms.append((jnp.where(tri, jnp.exp2(acol - ap_t[h:h + 1, :]), 0.0) * cb).astype(BF16))
                bws.append((bg_t * w_t[h:h + 1, :]).astype(BF16))
            lhs = jnp.concatenate([jnp.concatenate(ms, axis=1), jnp.concatenate(bws, axis=1)], axis=0)
            xpair = xs_bf[:, h0 * SSD_HEAD_DIM:(h0 + 2) * SSD_HEAD_DIM]
            zero = jnp.zeros_like(xpair)
            rhs = jnp.concatenate([jnp.where(lane < SSD_HEAD_DIM, xpair, zero),
                                   jnp.where(lane >= SSD_HEAD_DIM, xpair, zero)], axis=0)
            out = _dot(lhs, rhs)
            ea = jnp.exp2(jnp.where(lane < SSD_HEAD_DIM, acols[0], acols[1]))
            ys.append(out[:q] + yoff[:, pr * LANES:(pr + 1) * LANES] * ea)
            cols = slice(h0 * SSD_HEAD_DIM, (h0 + 2) * SSD_HEAD_DIM)
            ht_ref[:, cols] = ht_g[:, pr * LANES:(pr + 1) * LANES] * ea[q - 1:q, :] + out[q:]
    return jnp.concatenate(ys, axis=1)


P_XBC = 0
P_DT = P_XBC + SSD_CONV_DIM
P_UV = P_DT + LANES
P_POOL = P_UV + 2 * SGU_WIDTH
P_WIDTH = P_POOL + POOL_WIDTH
PROJ_PIECE = 512


def _branch_prompt_kernel(xa_ref, xn_ref, nm_ref, wxbc_ref, wdt_ref, wuv_ref, wpool_ref,
                          convw_ref, convb_ref, dtb_ref, alog_ref, dskip_ref,
                          lng_ref, lnb_ref, wsp_ref, bsp_ref, poolw_ref, pools_ref,
                          ya_ref, yb_ref, yc_ref, convo_ref, ssmo_ref, poolo_ref, vo_ref,
                          pa_ref, pb_ref, ht_ref, cc_ref, pc_ref, xcv_ref, *, tb, npair):
    i = pl.program_id(0)
    j = pl.program_id(1)
    nchunk = tb // SSD_CHUNK

    def project_pieces(x, p_ref):
        hb = _rmsnorm(x, nm_ref[...]).astype(BF16)
        pieces = []
        for w_ref, base, width in ((wxbc_ref, P_XBC, SSD_CONV_DIM), (wdt_ref, P_DT, LANES),
                                   (wpool_ref, P_POOL, POOL_WIDTH), (wuv_ref, P_UV, 2 * SGU_WIDTH)):
            for c0 in range(0, width, PROJ_PIECE):
                c1 = min(c0 + PROJ_PIECE, width)

                def piece(w_ref=w_ref, base=base, c0=c0, c1=c1):
                    p_ref[:, base + c0:base + c1] = _dot(hb, w_ref[:, c0:c1])
                pieces.append(piece)
        return pieces

    def process_phases(p_ref, half):
        orow = slice(half * tb, (half + 1) * tb)
        res = {}
        phases = []

        def conv(cols):
            xp = jnp.concatenate([cc_ref[:, cols], p_ref[:, P_XBC + cols.start:P_XBC + cols.stop]], axis=0)
            acc = xp[SUBLANES:, :] * convw_ref[SSD_CONV - 1:SSD_CONV, cols]
            for k in range(SSD_CONV - 1):
                acc = acc + pltpu.roll(xp, SSD_CONV - 1 - k, 0)[SUBLANES:, :] * convw_ref[k:k + 1, cols]
            xcv_ref[:, cols] = _silu(acc + convb_ref[:, cols])
        for c0 in range(0, SSD_CONV_DIM, PROJ_PIECE):
            phases.append(functools.partial(conv, slice(c0, c0 + PROJ_PIECE)))

        def scan(c):
            rows = slice(c * SSD_CHUNK, (c + 1) * SSD_CHUNK)
            xs = xcv_ref[rows, 0:SSD_INNER]
            y = _ssd_chunk(xs.astype(BF16),
                           xcv_ref[rows, SSD_INNER:SSD_INNER + SSD_GROUPS * SSD_STATE],
                           xcv_ref[rows, SSD_INNER + SSD_GROUPS * SSD_STATE:SSD_CONV_DIM],
                           _softplus(p_ref[rows, P_DT:P_UV] + dtb_ref[...]),
                           -jnp.exp(alog_ref[...]), ht_ref)
            ya_ref[half * tb + c * SSD_CHUNK:half * tb + (c + 1) * SSD_CHUNK, :] = (
                y + dskip_ref[...] * xs).astype(BF16)
            if c == nchunk - 1:
                res['tail'] = p_ref[tb - SUBLANES:tb, P_XBC:P_DT]
                cc_ref[...] = res['tail']
        for c in range(nchunk):
            phases.append(functools.partial(scan, c))

        def spatial():
            u = _gelu_tanh(p_ref[:, P_UV:P_UV + SGU_WIDTH])
            v = _gelu_tanh(p_ref[:, P_UV + SGU_WIDTH:P_POOL])
            mu = jnp.mean(v, axis=-1, keepdims=True)
            vc = v - mu
            var = jnp.mean(vc * vc, axis=-1, keepdims=True)
            vn = vc * lax.rsqrt(var + EPS) * lng_ref[...] + lnb_ref[...]
            res['vn'] = vn
            vn_bf = vn.astype(BF16)
            tri = _row_iota((SGU_CHUNK, SGU_CHUNK)) >= _lane_iota((SGU_CHUNK, SGU_CHUNK))
            s_groups = []
            for g in range(SGU_GROUPS):
                wm = jnp.where(tri, wsp_ref[g], 0.0).astype(BF16)
                vg = jnp.concatenate([vn_bf[c * SGU_CHUNK:(c + 1) * SGU_CHUNK, g * SGU_GW:(g + 1) * SGU_GW]
                                      for c in range(nchunk)], axis=1)
                s_groups.append(_dot(wm, vg))
            s = jnp.concatenate([jnp.concatenate([sg[:, c * SGU_GW:(c + 1) * SGU_GW] for sg in s_groups], axis=1)
                                 + bsp_ref[...] for c in range(nchunk)], axis=0)
            yb_ref[orow, :] = (u * s).astype(BF16)
        phases.append(spatial)

        def pool():
            pos = (2 * j + half) * tb + _row_iota((tb, POOL_GW))
            yc_cols = []
            for g, w in enumerate(POOL_WINDOWS):
                cols = slice(P_POOL + g * POOL_GW, P_POOL + (g + 1) * POOL_GW)
                cur = p_ref[:, cols]
                wsum = jnp.concatenate([pc_ref[:, g * POOL_GW:(g + 1) * POOL_GW], cur], axis=0)
                span = 1
                while span < w:
                    wsum = wsum + pltpu.roll(wsum, span, 0)
                    span *= 2
                cnt = jnp.minimum(pos + 1, w).astype(F32)
                d = wsum[2 * SUBLANES:, :] / cnt - cur
                yc_cols.append(_dot(d.astype(BF16), poolw_ref[g]))
            yc_ref[orow, :] = (jnp.concatenate(yc_cols, axis=1) * pools_ref[...]).astype(BF16)
            res['ptail'] = p_ref[tb - 2 * SUBLANES:tb, P_POOL:P_WIDTH]
            pc_ref[...] = res['ptail']
        phases.append(pool)
        return phases, res

    def run(p_cur, half, x_next, p_next):
        phases, res = process_phases(p_cur, half)
        pieces = project_pieces(x_next, p_next)
        done = 0
        for k, phase in enumerate(phases):
            phase()
            upto = (k + 1) * len(pieces) // len(phases)
            for piece in pieces[done:upto]:
                piece()
            done = upto
        return res

    @pl.when(jnp.logical_and(i == 0, j == 0))
    def _():
        for piece in project_pieces(xa_ref[0:tb, :], pa_ref):
            piece()

    @pl.when(j == 0)
    def _():
        ht_ref[...] = jnp.zeros_like(ht_ref)
        cc_ref[...] = jnp.zeros_like(cc_ref)
        pc_ref[...] = jnp.zeros_like(pc_ref)

    run(pa_ref, 0, xa_ref[tb:2 * tb, :], pb_ref)
    res = run(pb_ref, 1, xn_ref[...], pa_ref)

    @pl.when(j == npair - 1)
    def _():
        convo_ref[...] = res['tail']
        ssmo_ref[...] = ht_ref[...].T
        poolo_ref[...] = res['ptail']
        vo_ref[...] = res['vn'][tb - SGU_CHUNK:tb, :]


def _const_spec(arr):
    nd = arr.ndim
    return pl.BlockSpec(arr.shape, lambda *_: (0,) * nd, pipeline_mode=pl.Buffered(1))


def _layer_spec(arr, layer):
    nd = arr.ndim - 1
    return pl.BlockSpec((None,) + arr.shape[1:], lambda *_: (layer,) + (0,) * nd, pipeline_mode=pl.Buffered(1))


def _skip_refs(body, first, count):
    def wrapped(*refs):
        return body(*refs[:first], *refs[first + count:])
    return wrapped


BRANCH_WEIGHTS = ('norm_mix', 'w_xbc', 'w_dt', 'w_uv', 'w_pool', 'conv_w', 'conv_b', 'dt_bias', 'a_log',
                  'd_skip', 'ln_g', 'ln_b')
N_STATE_OUTPUTS = 4


def _branch_prompt(x, sw, layer, prev_states):
    b, l, _ = x.shape
    depth = sw['w_xbc'].shape[0]
    tb = PROMPT_BLOCK
    npair = l // (2 * tb)
    assert l % (2 * tb) == 0 and tb % SSD_CHUNK == 0 and tb >= 2 * SUBLANES
    stacked = [sw[k] for k in BRANCH_WEIGHTS + ('w_sp', 'b_sp', 'pool_w', 'pool_scale')]
    tok = lambda width: pl.BlockSpec((None, 2 * tb, width), lambda i, j: (i, j, 0))

    def next_block(i, j):
        flat = jnp.minimum(i * npair + j + 1, b * npair - 1)
        return flat // npair, (flat % npair) * 2, 0

    per_seq = lambda rows, width: pl.BlockSpec((None, None, rows, width), lambda i, j: (layer, i, 0, 0))
    out_shape = (jax.ShapeDtypeStruct((b, l, SSD_INNER), BF16),
                 jax.ShapeDtypeStruct((b, l, SGU_WIDTH), BF16),
                 jax.ShapeDtypeStruct((b, l, POOL_WIDTH), BF16),
                 jax.ShapeDtypeStruct((depth, b, SUBLANES, SSD_CONV_DIM), F32),
                 jax.ShapeDtypeStruct((depth, b, SSD_INNER, SSD_STATE), F32),
                 jax.ShapeDtypeStruct((depth, b, 2 * SUBLANES, POOL_WIDTH), F32),
                 jax.ShapeDtypeStruct((depth, b, SGU_CHUNK, SGU_WIDTH), F32))
    out_specs = (tok(SSD_INNER), tok(SGU_WIDTH), tok(POOL_WIDTH),
                 per_seq(SUBLANES, SSD_CONV_DIM), per_seq(SSD_INNER, SSD_STATE),
                 per_seq(2 * SUBLANES, POOL_WIDTH), per_seq(SGU_CHUNK, SGU_WIDTH))
    scratch = [pltpu.VMEM((tb, P_WIDTH), F32),
               pltpu.VMEM((tb, P_WIDTH), F32),
               pltpu.VMEM((SSD_STATE, SSD_INNER), F32),
               pltpu.VMEM((SUBLANES, SSD_CONV_DIM), F32),
               pltpu.VMEM((2 * SUBLANES, POOL_WIDTH), F32),
               pltpu.VMEM((tb, SSD_CONV_DIM), F32)]
    body = functools.partial(_branch_prompt_kernel, tb=tb, npair=npair)
    in_specs = [tok(D_MODEL), pl.BlockSpec((None, tb, D_MODEL), next_block)] + [_layer_spec(w, layer) for w in stacked]
    args = [x, x] + stacked
    aliases = {}
    if prev_states is not None:
        body = _skip_refs(body, len(args), N_STATE_OUTPUTS)
        aliases = {len(args) + k: 3 + k for k in range(N_STATE_OUTPUTS)}
        in_specs = in_specs + [pl.BlockSpec(memory_space=pl.ANY)] * N_STATE_OUTPUTS
        args = args + list(prev_states)
    ya, yb, yc, *states = pl.pallas_call(
        body, grid=(b, npair), in_specs=in_specs,
        out_specs=out_specs, out_shape=out_shape, scratch_shapes=scratch,
        input_output_aliases=aliases,
        compiler_params=pltpu.CompilerParams(dimension_semantics=("arbitrary", "arbitrary"),
                                             vmem_limit_bytes=VMEM_LIMIT),
        name="branch_prompt",
    )(*args)
    return ya, yb, yc, states


def _branch_sample_kernel(x_ref, convs_ref, ssms_ref, pools_in_ref, nm_ref, wxbc_ref, wdt_ref, wuv_ref,
                          wpool_ref, convw_ref, convb_ref, dtb_ref, alog_ref, dskip_ref,
                          lng_ref, lnb_ref, wd_ref, b8_ref, poolw_ref, pools_ref, e2_ref,
                          ya_ref, yb_ref, yc_ref, convo_ref, ssmo_ref, poolo_ref, vo_ref,
                          c_ref, bm_ref, xd_ref, daug_ref, yoff_ref, *, nb, seq, start):
    r = nb * seq
    assert seq == SUBLANES
    t128 = _row_iota((r, LANES)) % seq

    def tile_roll(v, j):
        width = v.shape[-1]
        return pltpu.roll(v.reshape(nb, seq, width), j, 1).reshape(r, width)

    def shift_rows(cur, prev, j, width):
        tt = _row_iota((r, width)) % seq
        if prev is None:
            return jnp.where(tt >= j, tile_roll(cur, j), 0.0)
        return jnp.where(tt >= j, tile_roll(cur, j), tile_roll(prev, j))

    hb = _rmsnorm(x_ref[...].reshape(r, D_MODEL), nm_ref[...]).astype(BF16)

    xbc = _dot(hb, wxbc_ref[...])
    cbuf = convs_ref[...].reshape(r, SSD_CONV_DIM)
    acc = xbc * convw_ref[SSD_CONV - 1:SSD_CONV, :]
    for k in range(SSD_CONV - 1):
        acc = acc + shift_rows(xbc, cbuf, SSD_CONV - 1 - k, SSD_CONV_DIM) * convw_ref[k:k + 1, :]
    xcv = _silu(acc + convb_ref[...])
    convo_ref[...] = xbc.reshape(nb, seq, SSD_CONV_DIM)
    xs = xcv[:, 0:SSD_INNER]
    bm = xcv[:, SSD_INNER:SSD_INNER + SSD_GROUPS * SSD_STATE]
    cm = xcv[:, SSD_INNER + SSD_GROUPS * SSD_STATE:SSD_CONV_DIM]
    dt = _softplus(_dot(hb, wdt_ref[...]) + dtb_ref[...])
    a_row = -jnp.exp(alog_ref[...])
    a_cs = dt * a_row
    for s in (1, 2, 4):
        a_cs = a_cs + jnp.where(t128 >= s, tile_roll(a_cs, s), 0.0)
    a3 = a_cs.reshape(nb, seq, LANES)
    tot = jnp.broadcast_to(a3[:, seq - 1:seq, :], (nb, seq, LANES)).reshape(r, LANES)
    e2 = e2_ref[...]
    exp_a_e = _expand_heads(jnp.exp(a_cs), e2)
    decst_e = _expand_heads(jnp.exp(tot - a_cs), e2)
    dt_e = _expand_heads(dt, e2)
    dectot_e = _expand_heads(jnp.exp(tot), e2)
    xdt = xs * dt_e
    lane = _lane_iota((r, LANES))
    y = jnp.zeros((r, SSD_INNER), F32)
    for j in range(seq):
        if j == 0:
            lj = jnp.ones((r, LANES), F32)
            b_sh, x_sh = bm, xdt
        else:
            lj = jnp.where(t128 >= j, jnp.exp(a_cs - tile_roll(a_cs, j)), 0.0)
            b_sh, x_sh = tile_roll(bm, j), tile_roll(xdt, j)
        prod = cm * b_sh
        cb0 = jnp.sum(prod[:, 0:SSD_STATE], axis=-1, keepdims=True)
        cb1 = jnp.sum(prod[:, SSD_STATE:2 * SSD_STATE], axis=-1, keepdims=True)
        mj = lj * jnp.where(lane < HEADS_PER_GROUP, cb0, cb1)
        y = y + _expand_heads(mj, e2) * x_sh
    c_ref[...] = cm
    bm_ref[...] = bm
    xd_ref[...] = xdt * decst_e
    dec_hi = dectot_e.astype(BF16).astype(F32)
    t1024 = _row_iota((r, SSD_INNER)) % seq
    daug_ref[...] = jnp.where(t1024 == 0, dec_hi, jnp.where(t1024 == 1, dectot_e - dec_hi, 0.0))
    t8 = _row_iota((seq, LANES))
    ones2 = jnp.where(t8 < 2, 1.0, 0.0)
    zeros8 = jnp.zeros((seq, LANES), F32)

    def per_seq(n, carry):
        rows = pl.ds(pl.multiple_of(n * seq, seq), seq)
        for g in range(SSD_GROUPS):
            cols = slice(g * GROUP_INNER, (g + 1) * GROUP_INNER)
            scol = slice(g * SSD_STATE, (g + 1) * SSD_STATE)
            h0 = ssms_ref[n, cols, :]
            cg = c_ref[rows, scol].astype(BF16)
            yoff_ref[rows, cols] = _dot_nt(cg, h0.astype(BF16))
            lhs = jnp.concatenate([xd_ref[rows, cols], daug_ref[rows, cols]], axis=0).astype(BF16)
            rhs = jnp.concatenate([jnp.concatenate([bm_ref[rows, scol], zeros8], axis=1),
                                   jnp.concatenate([zeros8, ones2], axis=1)], axis=0).astype(BF16)
            upd = _dot_tn(lhs, rhs)
            ssmo_ref[n, cols, :] = upd[:, SSD_STATE:] * h0 + upd[:, :SSD_STATE]
        return carry

    lax.fori_loop(0, nb, per_seq, 0, unroll=SAMPLE_UNROLL)
    ya_ref[...] = (y + yoff_ref[...] * exp_a_e + dskip_ref[...] * xs).astype(BF16)

    a = _gelu_tanh(_dot(hb, wuv_ref[...]))
    u = a[:, :SGU_WIDTH]
    v = a[:, SGU_WIDTH:]
    mu = jnp.mean(v, axis=-1, keepdims=True)
    vc = v - mu
    var = jnp.mean(vc * vc, axis=-1, keepdims=True)
    vn = vc * lax.rsqrt(var + EPS) * lng_ref[...] + lnb_ref[...]
    vo_ref[...] = vn
    s = vn.reshape(nb, seq, SGU_WIDTH) * wd_ref[0] + b8_ref[...]
    for j in range(1, seq):
        s = s + tile_roll(vn, j).reshape(nb, seq, SGU_WIDTH) * wd_ref[j]
    yb_ref[...] = (u * s.reshape(r, SGU_WIDTH)).astype(BF16)

    xc = _dot(hb, wpool_ref[...])
    pbuf = pools_in_ref[...]
    t0 = pbuf[:, 0:seq, :].reshape(r, POOL_WIDTH)
    t1 = pbuf[:, seq:2 * seq, :].reshape(r, POOL_WIDTH)
    poolo_ref[:, 0:seq, :] = pbuf[:, seq:2 * seq, :]
    poolo_ref[:, seq:2 * seq, :] = xc.reshape(nb, seq, POOL_WIDTH)
    pos = start + (_row_iota((r, POOL_GW)) % seq)
    yc_cols = []
    for g, w in enumerate(POOL_WINDOWS):
        cols = slice(g * POOL_GW, (g + 1) * POOL_GW)
        tiles = [t0[:, cols], t1[:, cols], xc[:, cols]]
        span = 1
        while span < min(w, seq):
            prev = [None] + tiles[:-1]
            tiles = [tl + shift_rows(tl, pv, span, POOL_GW) for tl, pv in zip(tiles, prev)]
            span *= 2
        wsum = tiles[2] if w <= seq else tiles[2] + tiles[1]
        cnt = jnp.minimum(pos + 1, w).astype(F32)
        d = wsum / cnt - xc[:, cols]
        yc_cols.append(_dot(d.astype(BF16), poolw_ref[g]))
    yc_ref[...] = (jnp.concatenate(yc_cols, axis=1) * pools_ref[...]).astype(BF16)


def _branch_sample(x, conv_pad, ssm_flat, pool_pad, sw, layer, prev_states, start):
    n, seq, _ = x.shape
    depth = sw['w_xbc'].shape[0]
    nb = SAMPLE_SEQS
    assert n % nb == 0 and seq == SUBLANES and start >= max(POOL_WINDOWS)
    r = nb * seq
    stacked = [sw[k] for k in BRANCH_WEIGHTS + ('w_diag8', 'b8', 'pool_w', 'pool_scale')]
    shared = [sw['e2']]
    seq3 = lambda rows, width: pl.BlockSpec((nb, rows, width), lambda i: (i, 0, 0))
    seq4 = lambda rows, width: pl.BlockSpec((None, nb, rows, width), lambda i: (layer, i, 0, 0))
    tok = lambda width: pl.BlockSpec((r, width), lambda i: (i, 0))
    out_shape = (jax.ShapeDtypeStruct((n * seq, SSD_INNER), BF16),
                 jax.ShapeDtypeStruct((n * seq, SGU_WIDTH), BF16),
                 jax.ShapeDtypeStruct((n * seq, POOL_WIDTH), BF16),
                 jax.ShapeDtypeStruct((depth, n, seq, SSD_CONV_DIM), F32),
                 jax.ShapeDtypeStruct((depth, n, SSD_INNER, SSD_STATE), F32),
                 jax.ShapeDtypeStruct((depth, n, 2 * seq, POOL_WIDTH), F32),
                 jax.ShapeDtypeStruct((depth, n * seq, SGU_WIDTH), F32))
    out_specs = (tok(SSD_INNER), tok(SGU_WIDTH), tok(POOL_WIDTH), seq4(seq, SSD_CONV_DIM),
                 seq4(SSD_INNER, SSD_STATE), seq4(2 * seq, POOL_WIDTH),
                 pl.BlockSpec((None, r, SGU_WIDTH), lambda i: (layer, i, 0)))
    scratch = [pltpu.VMEM((r, SSD_GROUPS * SSD_STATE), F32),
               pltpu.VMEM((r, SSD_GROUPS * SSD_STATE), F32),
               pltpu.VMEM((r, SSD_INNER), F32),
               pltpu.VMEM((r, SSD_INNER), F32),
               pltpu.VMEM((r, SSD_INNER), F32)]
    body = functools.partial(_branch_sample_kernel, nb=nb, seq=seq, start=start)
    in_specs = ([seq3(seq, D_MODEL), seq4(seq, SSD_CONV_DIM), seq4(SSD_INNER, SSD_STATE), seq4(2 * seq, POOL_WIDTH)]
                + [_layer_spec(w, layer) for w in stacked] + [_const_spec(w) for w in shared])
    args = [x, conv_pad, ssm_flat, pool_pad] + stacked + shared
    aliases = {}
    if prev_states is not None:
        body = _skip_refs(body, len(args), N_STATE_OUTPUTS)
        aliases = {len(args) + k: 3 + k for k in range(N_STATE_OUTPUTS)}
        in_specs = in_specs + [pl.BlockSpec(memory_space=pl.ANY)] * N_STATE_OUTPUTS
        args = args + list(prev_states)
    ya, yb, yc, *states = pl.pallas_call(
        body, grid=(n // nb,), in_specs=in_specs,
        out_specs=out_specs, out_shape=out_shape, scratch_shapes=scratch,
        input_output_aliases=aliases,
        compiler_params=pltpu.CompilerParams(dimension_semantics=("arbitrary",),
                                             vmem_limit_bytes=VMEM_LIMIT),
        name="branch_sample",
    )(*args)
    return ya, yb, yc, states


def _merge_kernel(x_ref, ya_ref, yb_ref, yc_ref, nm_ref, wz_ref, ssdn_ref, wg_ref, wa_ref, wb_ref, wc_ref, wo_ref,
                  o_ref):
    x = x_ref[...]
    hb = _rmsnorm(x, nm_ref[...]).astype(BF16)
    ya = _group_rmsnorm(ya_ref[...].astype(F32) * _silu(_dot(hb, wz_ref[...])), ssdn_ref[...]).astype(BF16)
    m = _sigmoid(_dot(hb, wg_ref[:, 0:D_MODEL])) * _dot(ya, wa_ref[...])
    m = m + _sigmoid(_dot(hb, wg_ref[:, D_MODEL:2 * D_MODEL])) * _dot(yb_ref[...], wb_ref[...])
    m = m + _sigmoid(_dot(hb, wg_ref[:, 2 * D_MODEL:3 * D_MODEL])) * _dot(yc_ref[...], wc_ref[...])
    o_ref[...] = x + _dot(m.astype(BF16), wo_ref[...])


def _merge(x, ya, yb, yc, sw, layer):
    t = x.shape[0]
    tm = min(TOKEN_BLOCK, t)
    assert t % tm == 0
    stacked = [sw[k] for k in ('norm_mix', 'w_z', 'ssd_norm', 'w_gate', 'w_br_a', 'w_br_b', 'w_br_c', 'w_out')]
    tok = lambda width: pl.BlockSpec((tm, width), lambda i: (i, 0))
    return pl.pallas_call(
        _merge_kernel, grid=(t // tm,),
        in_specs=[tok(D_MODEL), tok(SSD_INNER), tok(SGU_WIDTH), tok(POOL_WIDTH)] + [_layer_spec(w, layer) for w in stacked],
        out_specs=tok(D_MODEL), out_shape=jax.ShapeDtypeStruct((t, D_MODEL), F32),
        compiler_params=pltpu.CompilerParams(dimension_semantics=("arbitrary",), vmem_limit_bytes=VMEM_LIMIT),
        name="merge",
    )(x, ya, yb, yc, *stacked)


def _ffn_kernel(x_ref, p_ref, nf_ref, wgu_ref, wd_ref, np_ref, wpg_ref, wpu_ref, fn_ref, o_ref, *, final):
    x = x_ref[...]
    hb = _rmsnorm(x, nf_ref[...]).astype(BF16)
    acc = x
    for cols in (slice(0, FF_SPLIT), slice(FF_SPLIT, D_FF)):
        up_cols = slice(D_FF + cols.start, D_FF + cols.stop)
        act = (_silu(_dot(hb, wgu_ref[:, cols])) * _dot(hb, wgu_ref[:, up_cols])).astype(BF16)
        acc = acc + _dot(act, wd_ref[cols, :])
    hb = _rmsnorm(acc, np_ref[...]).astype(BF16)
    out = acc + _dot(p_ref[...].astype(BF16), wpu_ref[...]) * _sigmoid(_dot(hb, wpg_ref[...]))
    if final:
        out = _rmsnorm(out, fn_ref[...])
    o_ref[...] = out


def _ffn(x, p, sw, layer, final):
    t = x.shape[0]
    tm = min(TOKEN_BLOCK, t)
    assert t % tm == 0
    stacked = [sw[k] for k in ('norm_ffn', 'w_gate_up', 'w_down', 'norm_ple', 'w_ple_gate', 'w_ple_up')]
    tok = lambda width: pl.BlockSpec((tm, width), lambda i: (i, 0))
    return pl.pallas_call(
        functools.partial(_ffn_kernel, final=final), grid=(t // tm,),
        in_specs=([tok(D_MODEL), pl.BlockSpec((None, tm, PLE_DIM), lambda i: (layer, i, 0))]
                  + [_layer_spec(w, layer) for w in stacked] + [_const_spec(sw['final_norm'])]),
        out_specs=tok(D_MODEL), out_shape=jax.ShapeDtypeStruct((t, D_MODEL), F32),
        compiler_params=pltpu.CompilerParams(dimension_semantics=("arbitrary",), vmem_limit_bytes=VMEM_LIMIT),
        name="ffn",
    )(x, p, *stacked, sw['final_norm'])


IN_DIM = O_GATE + 3 * D_MODEL
W_IN_ROWS = 256


def _split_w_in_kernel(wt_ref, z_ref, xbc_ref, dt_ref, uv_ref, pool_ref, gate_ref):
    z_ref[...] = wt_ref[O_Z:O_XBC, :].T.astype(BF16)
    xbc_ref[...] = wt_ref[O_XBC:O_DT, :].T.astype(BF16)
    dt = wt_ref[O_DT:O_DT + LANES, :].T
    dt_ref[...] = jnp.where(_lane_iota(dt.shape) < SSD_HEADS, dt, 0.0).astype(BF16)
    uv_ref[...] = wt_ref[O_UV:O_POOL, :].T.astype(BF16)
    pool_ref[...] = wt_ref[O_POOL:O_GATE, :].T.astype(BF16)
    gate_ref[...] = wt_ref[O_GATE:IN_DIM, :].T.astype(BF16)


def _split_w_in(w_in):
    depth, d, width = w_in.shape
    assert width == IN_DIM and d % W_IN_ROWS == 0
    widths = (O_XBC - O_Z, O_DT - O_XBC, LANES, O_POOL - O_UV, O_GATE - O_POOL, IN_DIM - O_GATE)
    spec = lambda w: pl.BlockSpec((None, W_IN_ROWS, w), lambda i, j: (i, j, 0))
    return pl.pallas_call(
        _split_w_in_kernel, grid=(depth, d // W_IN_ROWS),
        in_specs=[pl.BlockSpec((None, IN_DIM, W_IN_ROWS), lambda i, j: (i, 0, j))],
        out_specs=tuple(spec(w) for w in widths),
        out_shape=tuple(jax.ShapeDtypeStruct((depth, d, w), BF16) for w in widths),
        compiler_params=pltpu.CompilerParams(dimension_semantics=("arbitrary", "arbitrary"),
                                             vmem_limit_bytes=VMEM_LIMIT),
        name="split_w_in",
    )(jnp.swapaxes(w_in, 1, 2))


def _head_expansion():
    rows = jnp.arange(2 * LANES)[:, None] % LANES
    cols = jnp.arange(SSD_INNER)[None, :] // SSD_HEAD_DIM
    return (rows == cols).astype(BF16)


def _prepare_weights(norm_mix, w_in, conv_w, conv_b, dt_bias, a_log, d_skip, ssd_norm, sgu_ln_g, sgu_ln_b,
                     w_spatial, b_spatial, pool_w, pool_scale, w_br_a, w_br_b, w_br_c, w_out, norm_ffn,
                     w_gate_up, w_down, norm_ple, w_ple_gate, w_ple_up, final_norm):
    depth = w_in.shape[0]
    row = lambda v: v.reshape(depth, 1, -1).astype(F32)
    pad_heads = lambda v: jnp.pad(v.reshape(depth, 1, -1).astype(F32), ((0, 0), (0, 0), (0, LANES - SSD_HEADS)))
    seq = SUBLANES
    w8 = w_spatial[:, :, :seq, :seq]
    tt = jnp.arange(seq)
    lag = tt[None, :] - tt[:, None]
    w_lag = jnp.where(lag >= 0, w8[:, :, tt[None, :], jnp.clip(lag, 0, seq - 1)], 0.0)
    w_diag8 = jnp.repeat(jnp.transpose(w_lag, (0, 2, 3, 1)), SGU_GW, axis=3).astype(F32)
    b_full = jnp.repeat(jnp.transpose(b_spatial, (0, 2, 1)), SGU_GW, axis=2).astype(F32)
    w_z, w_xbc, w_dt, w_uv, w_pool, w_gate = _split_w_in(w_in)
    return {
        'norm_mix': row(norm_mix),
        'w_z': w_z, 'w_xbc': w_xbc, 'w_dt': w_dt, 'w_uv': w_uv, 'w_pool': w_pool, 'w_gate': w_gate,
        'conv_w': conv_w.astype(F32), 'conv_b': row(conv_b),
        'dt_bias': pad_heads(dt_bias), 'a_log': pad_heads(a_log),
        'd_skip': row(jnp.repeat(d_skip, SSD_HEAD_DIM, axis=1)),
        'ssd_norm': row(ssd_norm),
        'ln_g': row(sgu_ln_g), 'ln_b': row(sgu_ln_b),
        'w_sp': w_spatial.astype(F32), 'b_sp': b_full,
        'w_diag8': w_diag8, 'b8': b_full[:, :seq],
        'pool_w': pool_w.astype(BF16), 'pool_scale': row(pool_scale),
        'w_br_a': w_br_a.astype(BF16), 'w_br_b': w_br_b.astype(BF16), 'w_br_c': w_br_c.astype(BF16),
        'w_out': w_out.astype(BF16),
        'norm_ffn': row(norm_ffn),
        'w_gate_up': w_gate_up.astype(BF16),
        'w_down': w_down.astype(BF16),
        'norm_ple': row(norm_ple),
        'w_ple_gate': w_ple_gate.astype(BF16), 'w_ple_up': w_ple_up.astype(BF16),
        'final_norm': final_norm.reshape(1, -1).astype(F32),
        'e2': _head_expansion(),
    }


def kernel(x_prompt, x_sample, state_conv, state_ssm, state_pool, p_prompt, p_sample, norm_mix, w_in, conv_w, conv_b, dt_bias, a_log, d_skip, ssd_norm, sgu_ln_g, sgu_ln_b, w_spatial, b_spatial, pool_w, pool_scale, w_br_a, w_br_b, w_br_c, w_out, norm_ffn, w_gate_up, w_down, norm_ple, w_ple_gate, w_ple_up, final_norm):
    b, l, _ = x_prompt.shape
    n, s, _ = x_sample.shape
    depth = w_in.shape[0]
    sw = _prepare_weights(norm_mix, w_in, conv_w, conv_b, dt_bias, a_log, d_skip, ssd_norm, sgu_ln_g, sgu_ln_b,
                          w_spatial, b_spatial, pool_w, pool_scale, w_br_a, w_br_b, w_br_c, w_out, norm_ffn,
                          w_gate_up, w_down, norm_ple, w_ple_gate, w_ple_up, final_norm)
    conv_pad = jnp.pad(state_conv, ((0, 0), (0, 0), (SUBLANES - (SSD_CONV - 1), 0), (0, 0)))
    pool_pad = jnp.pad(state_pool, ((0, 0), (0, 0), (2 * SUBLANES - POOL_BUF, 0), (0, 0)))
    ssm_flat = state_ssm.reshape(depth, n, SSD_INNER, SSD_STATE)
    pp = p_prompt.reshape(depth, b * l, PLE_DIM)
    ps = p_sample.reshape(depth, n * s, PLE_DIM)
    xp = x_prompt
    xs = x_sample.reshape(n * s, D_MODEL)
    st_p = None
    st_s = None
    for i in range(depth):
        final = i == depth - 1
        ya, yb, yc, st_p = _branch_prompt(xp, sw, i, st_p)
        x1 = _merge(xp.reshape(b * l, D_MODEL), ya.reshape(b * l, -1), yb.reshape(b * l, -1), yc.reshape(b * l, -1), sw, i)
        xp = _ffn(x1, pp, sw, i, final).reshape(b, l, D_MODEL)
        ya, yb, yc, st_s = _branch_sample(xs.reshape(n, s, D_MODEL), conv_pad, ssm_flat, pool_pad, sw, i, st_s, PAST_LEN)
        x1 = _merge(xs, ya, yb, yc, sw, i)
        xs = _ffn(x1, ps, sw, i, final)
    conv_p, ssm_p, pool_p, v_p = st_p
    conv_s, ssm_s, pool_s, v_s = st_s
    return (xp, xs.reshape(n, s, D_MODEL),
            conv_p[:, :, SUBLANES - (SSD_CONV - 1):, :],
            ssm_p.reshape(depth, b, SSD_HEADS, SSD_HEAD_DIM, SSD_STATE),
            pool_p[:, :, 2 * SUBLANES - POOL_BUF:, :],
            v_p,
            conv_s[:, :, s - (SSD_CONV - 1):, :],
            ssm_s.reshape(depth, n, SSD_HEADS, SSD_HEAD_DIM, SSD_STATE),
            pool_s[:, :, 2 * s - POOL_BUF:, :],
            v_s.reshape(depth, n, s, SGU_WIDTH))
```

```python
import functools
import math

import jax
import jax.numpy as jnp
from jax import lax
from jax.experimental import pallas as pl
from jax.experimental.pallas import tpu as pltpu

F32 = jnp.float32
BF16 = jnp.bfloat16

D_MODEL = 1024
DEPTH = 4
PAST_LEN = 16384
SSD_HEAD_DIM = 64
SSD_HEADS = 16
SSD_INNER = 1024
SSD_GROUPS = 2
SSD_STATE = 128
SSD_CONV = 4
SSD_CHUNK = 128
SSD_CONV_DIM = 1536
SGU_WIDTH = 512
SGU_GROUPS = 4
SGU_CHUNK = 128
SGU_GW = 128
POOL_WIDTH = 512
POOL_WINDOWS = (2, 4, 8, 16)
POOL_GW = 128
POOL_BUF = 15
D_FF = 2816
PLE_DIM = 256
EPS = 1e-6
LOG2_E = 1.4426950408889634
O_Z = 0
O_XBC = 1024
O_DT = 2560
O_UV = 2576
O_POOL = 3600
O_GATE = 4112

LANES = 128
SUBLANES = 8
HEADS_PER_GROUP = SSD_HEADS // SSD_GROUPS
GROUP_INNER = SSD_INNER // SSD_GROUPS
VMEM_LIMIT = 56 * 1024 * 1024

PROMPT_BLOCK = 256
SAMPLE_SEQS = 16
SAMPLE_UNROLL = 8
TOKEN_BLOCK = 512
MXU_DEPTH = 256
FF_SPLIT = (D_FF // MXU_DEPTH + 1) // 2 * MXU_DEPTH


def _dot(a, b):
    return jnp.dot(a, b, preferred_element_type=F32)


def _dot_nt(a, b):
    return lax.dot_general(a, b, (((1,), (1,)), ((), ())), preferred_element_type=F32)


def _dot_tn(a, b):
    return lax.dot_general(a, b, (((0,), (0,)), ((), ())), preferred_element_type=F32)


def _sigmoid(x):
    return 0.5 * jnp.tanh(0.5 * x) + 0.5


def _silu(x):
    h = 0.5 * x
    return h * jnp.tanh(h) + h


def _gelu_tanh(x):
    c = math.sqrt(2.0 / math.pi)
    return 0.5 * x * (1.0 + jnp.tanh(c * (x + 0.044715 * (x * x * x))))


def _softplus(x):
    return jnp.maximum(x, 0.0) + jnp.log1p(jnp.exp(-jnp.abs(x)))


def _rmsnorm(x, g):
    ms = jnp.mean(x * x, axis=-1, keepdims=True)
    return x * lax.rsqrt(ms + EPS) * g


def _split2(v):
    hi = v.astype(BF16)
    lo = (v - hi.astype(F32)).astype(BF16)
    return jnp.concatenate([hi, lo], axis=-1)


def _expand_heads(v, e2):
    return _dot(_split2(v), e2)


def _group_rmsnorm(y, g):
    parts = []
    for k in range(SSD_GROUPS):
        yk = y[:, k * GROUP_INNER:(k + 1) * GROUP_INNER]
        ms = jnp.mean(yk * yk, axis=-1, keepdims=True)
        parts.append(yk * lax.rsqrt(ms + EPS))
    return jnp.concatenate(parts, axis=-1) * g


def _row_iota(shape):
    return lax.broadcasted_iota(jnp.int32, shape, 0)


def _lane_iota(shape):
    return lax.broadcasted_iota(jnp.int32, shape, 1)


def _ssd_chunk(xs_bf, bm, cm, dtc, a_row, ht_ref):
    q = SSD_CHUNK
    tri = _row_iota((q, q)) >= _lane_iota((q, q))
    tril_bf = jnp.where(tri, 1.0, 0.0).astype(BF16)
    da = dtc * a_row
    hi = da.astype(BF16)
    r1 = da - hi.astype(F32)
    mid = r1.astype(BF16)
    lo = (r1 - mid.astype(F32)).astype(BF16)
    cs3 = _dot(tril_bf, jnp.concatenate([hi, mid, lo], axis=1))
    a_cs = (cs3[:, :LANES] + cs3[:, LANES:2 * LANES] + cs3[:, 2 * LANES:]) * LOG2_E
    a_t = a_cs.T
    ap_t = (a_cs - jnp.log2(dtc)).T
    w_t = jnp.exp2(a_t[:, q - 1:q] - ap_t)
    c_bf = cm.astype(BF16)
    lane = _lane_iota((q, LANES))
    ys = []
    for g in range(SSD_GROUPS):
        scol = slice(g * SSD_STATE, (g + 1) * SSD_STATE)
        cg = c_bf[:, scol]
        cb = _dot_nt(cg, bm[:, scol].astype(BF16))
        bg_t = bm[:, scol].T
        ht_g = ht_ref[:, g * GROUP_INNER:(g + 1) * GROUP_INNER]
        yoff = _dot(cg, ht_g.astype(BF16))
        for pr in range(HEADS_PER_GROUP // 2):
            h0 = g * HEADS_PER_GROUP + 2 * pr
            acols, ms, bws = [], [], []
            for h in (h0, h0 + 1):
                acol = jnp.broadcast_to(a_cs[:, h:h + 1], (q, q))
                acols.append(acol)
                ms.append((jnp.where(tri, jnp.exp2(acol - ap_t[h:h + 1, :]), 0.0) * cb).astype(BF16))
                bws.append((bg_t * w_t[h:h + 1, :]).astype(BF16))
            lhs = jnp.concatenate([jnp.concatenate(ms, axis=1), jnp.concatenate(bws, axis=1)], axis=0)
            xpair = xs_bf[:, h0 * SSD_HEAD_DIM:(h0 + 2) * SSD_HEAD_DIM]
            zero = jnp.zeros_like(xpair)
            rhs = jnp.concatenate([jnp.where(lane < SSD_HEAD_DIM, xpair, zero),
                                   jnp.where(lane >= SSD_HEAD_DIM, xpair, zero)], axis=0)
            out = _dot(lhs, rhs)
            ea = jnp.exp2(jnp.where(lane < SSD_HEAD_DIM, acols[0], acols[1]))
            ys.append(out[:q] + yoff[:, pr * LANES:(pr + 1) * LANES] * ea)
            cols = slice(h0 * SSD_HEAD_DIM, (h0 + 2) * SSD_HEAD_DIM)
            ht_ref[:, cols] = ht_g[:, pr * LANES:(pr + 1) * LANES] * ea[q - 1:q, :] + out[q:]
    return jnp.concatenate(ys, axis=1)


P_XBC = 0
P_DT = P_XBC + SSD_CONV_DIM
P_UV = P_DT + LANES
P_POOL = P_UV + 2 * SGU_WIDTH
P_WIDTH = P_POOL + POOL_WIDTH
PROJ_PIECE = 512


def _branch_prompt_kernel(xa_ref, xn_ref, nm_ref, wxbc_ref, wdt_ref, wuv_ref, wpool_ref,
                          convw_ref, convb_ref, dtb_ref, alog_ref, dskip_ref,
                          lng_ref, lnb_ref, wsp_ref, bsp_ref, poolw_ref, pools_ref,
                          ya_ref, yb_ref, yc_ref, convo_ref, ssmo_ref, poolo_ref, vo_ref,
                          pa_ref, pb_ref, ht_ref, cc_ref, pc_ref, xcv_ref, *, tb, npair):
    i = pl.program_id(0)
    j = pl.program_id(1)
    nchunk = tb // SSD_CHUNK

    def project_pieces(x, p_ref):
        hb = _rmsnorm(x, nm_ref[...]).astype(BF16)
        pieces = []
        for w_ref, base, width in ((wxbc_ref, P_XBC, SSD_CONV_DIM), (wdt_ref, P_DT, LANES),
                                   (wpool_ref, P_POOL, POOL_WIDTH), (wuv_ref, P_UV, 2 * SGU_WIDTH)):
            for c0 in range(0, width, PROJ_PIECE):
                c1 = min(c0 + PROJ_PIECE, width)

                def piece(w_ref=w_ref, base=base, c0=c0, c1=c1):
                    p_ref[:, base + c0:base + c1] = _dot(hb, w_ref[:, c0:c1])
                pieces.append(piece)
        return pieces

    def process_phases(p_ref, half):
        orow = slice(half * tb, (half + 1) * tb)
        res = {}
        phases = []

        def conv(cols):
            xp = jnp.concatenate([cc_ref[:, cols], p_ref[:, P_XBC + cols.start:P_XBC + cols.stop]], axis=0)
            acc = xp[SUBLANES:, :] * convw_ref[SSD_CONV - 1:SSD_CONV, cols]
            for k in range(SSD_CONV - 1):
                acc = acc + pltpu.roll(xp, SSD_CONV - 1 - k, 0)[SUBLANES:, :] * convw_ref[k:k + 1, cols]
            xcv_ref[:, cols] = _silu(acc + convb_ref[:, cols])
        for c0 in range(0, SSD_CONV_DIM, PROJ_PIECE):
            phases.append(functools.partial(conv, slice(c0, c0 + PROJ_PIECE)))

        def scan(c):
            rows = slice(c * SSD_CHUNK, (c + 1) * SSD_CHUNK)
            xs = xcv_ref[rows, 0:SSD_INNER]
            y = _ssd_chunk(xs.astype(BF16),
                           xcv_ref[rows, SSD_INNER:SSD_INNER + SSD_GROUPS * SSD_STATE],
                           xcv_ref[rows, SSD_INNER + SSD_GROUPS * SSD_STATE:SSD_CONV_DIM],
                           _softplus(p_ref[rows, P_DT:P_UV] + dtb_ref[...]),
                           -jnp.exp(alog_ref[...]), ht_ref)
            ya_ref[half * tb + c * SSD_CHUNK:half * tb + (c + 1) * SSD_CHUNK, :] = (
                y + dskip_ref[...] * xs).astype(BF16)
            if c == nchunk - 1:
                res['tail'] = p_ref[tb - SUBLANES:tb, P_XBC:P_DT]
                cc_ref[...] = res['tail']
        for c in range(nchunk):
            phases.append(functools.partial(scan, c))

        def spatial():
            u = _gelu_tanh(p_ref[:, P_UV:P_UV + SGU_WIDTH])
            v = _gelu_tanh(p_ref[:, P_UV + SGU_WIDTH:P_POOL])
            mu = jnp.mean(v, axis=-1, keepdims=True)
            vc = v - mu
            var = jnp.mean(vc * vc, axis=-1, keepdims=True)
            vn = vc * lax.rsqrt(var + EPS) * lng_ref[...] + lnb_ref[...]
            res['vn'] = vn
            vn_bf = vn.astype(BF16)
            tri = _row_iota((SGU_CHUNK, SGU_CHUNK)) >= _lane_iota((SGU_CHUNK, SGU_CHUNK))
            s_groups = []
            for g in range(SGU_GROUPS):
                wm = jnp.where(tri, wsp_ref[g], 0.0).astype(BF16)
                vg = jnp.concatenate([vn_bf[c * SGU_CHUNK:(c + 1) * SGU_CHUNK, g * SGU_GW:(g + 1) * SGU_GW]
                                      for c in range(nchunk)], axis=1)
                s_groups.append(_dot(wm, vg))
            s = jnp.concatenate([jnp.concatenate([sg[:, c * SGU_GW:(c + 1) * SGU_GW] for sg in s_groups], axis=1)
                                 + bsp_ref[...] for c in range(nchunk)], axis=0)
            yb_ref[orow, :] = (u * s).astype(BF16)
        phases.append(spatial)

        def pool():
            pos = (2 * j + half) * tb + _row_iota((tb, POOL_GW))
            yc_cols = []
            for g, w in enumerate(POOL_WINDOWS):
                cols = slice(P_POOL + g * POOL_GW, P_POOL + (g + 1) * POOL_GW)
                cur = p_ref[:, cols]
                wsum = jnp.concatenate([pc_ref[:, g * POOL_GW:(g + 1) * POOL_GW], cur], axis=0)
                span = 1
                while span < w:
                    wsum = wsum + pltpu.roll(wsum, span, 0)
                    span *= 2
                cnt = jnp.minimum(pos + 1, w).astype(F32)
                d = wsum[2 * SUBLANES:, :] / cnt - cur
                yc_cols.append(_dot(d.astype(BF16), poolw_ref[g]))
            yc_ref[orow, :] = (jnp.concatenate(yc_cols, axis=1) * pools_ref[...]).astype(BF16)
            res['ptail'] = p_ref[tb - 2 * SUBLANES:tb, P_POOL:P_WIDTH]
            pc_ref[...] = res['ptail']
        phases.append(pool)
        return phases, res

    def run(p_cur, half, x_next, p_next):
        phases, res = process_phases(p_cur, half)
        pieces = project_pieces(x_next, p_next)
        done = 0
        for k, phase in enumerate(phases):
            phase()
            upto = (k + 1) * len(pieces) // len(phases)
            for piece in pieces[done:upto]:
                piece()
            done = upto
        return res

    @pl.when(jnp.logical_and(i == 0, j == 0))
    def _():
        for piece in project_pieces(xa_ref[0:tb, :], pa_ref):
            piece()

    @pl.when(j == 0)
    def _():
        ht_ref[...] = jnp.zeros_like(ht_ref)
        cc_ref[...] = jnp.zeros_like(cc_ref)
        pc_ref[...] = jnp.zeros_like(pc_ref)

    run(pa_ref, 0, xa_ref[tb:2 * tb, :], pb_ref)
    res = run(pb_ref, 1, xn_ref[...], pa_ref)

    @pl.when(j == npair - 1)
    def _():
        convo_ref[...] = res['tail']
        ssmo_ref[...] = ht_ref[...].T
        poolo_ref[...] = res['ptail']
        vo_ref[...] = res['vn'][tb - SGU_CHUNK:tb, :]


def _const_spec(arr):
    nd = arr.ndim
    return pl.BlockSpec(arr.shape, lambda *_: (0,) * nd, pipeline_mode=pl.Buffered(1))


def _layer_spec(arr, layer):
    nd = arr.ndim - 1
    return pl.BlockSpec((None,) + arr.shape[1:], lambda *_: (layer,) + (0,) * nd, pipeline_mode=pl.Buffered(1))


def _skip_refs(body, first, count):
    def wrapped(*refs):
        return body(*refs[:first], *refs[first + count:])
    return wrapped


BRANCH_WEIGHTS = ('norm_mix', 'w_xbc', 'w_dt', 'w_uv', 'w_pool', 'conv_w', 'conv_b', 'dt_bias', 'a_log',
                  'd_skip', 'ln_g', 'ln_b')
N_STATE_OUTPUTS = 4


def _branch_prompt(x, sw, layer, prev_states):
    b, l, _ = x.shape
    depth = sw['w_xbc'].shape[0]
    tb = PROMPT_BLOCK
    npair = l // (2 * tb)
    assert l % (2 * tb) == 0 and tb % SSD_CHUNK == 0 and tb >= 2 * SUBLANES
    stacked = [sw[k] for k in BRANCH_WEIGHTS + ('w_sp', 'b_sp', 'pool_w', 'pool_scale')]
    tok = lambda width: pl.BlockSpec((None, 2 * tb, width), lambda i, j: (i, j, 0))

    def next_block(i, j):
        flat = jnp.minimum(i * npair + j + 1, b * npair - 1)
        return flat // npair, (flat % npair) * 2, 0

    per_seq = lambda rows, width: pl.BlockSpec((None, None, rows, width), lambda i, j: (layer, i, 0, 0))
    out_shape = (jax.ShapeDtypeStruct((b, l, SSD_INNER), BF16),
                 jax.ShapeDtypeStruct((b, l, SGU_WIDTH), BF16),
                 jax.ShapeDtypeStruct((b, l, POOL_WIDTH), BF16),
                 jax.ShapeDtypeStruct((depth, b, SUBLANES, SSD_CONV_DIM), F32),
                 jax.ShapeDtypeStruct((depth, b, SSD_INNER, SSD_STATE), F32),
                 jax.ShapeDtypeStruct((depth, b, 2 * SUBLANES, POOL_WIDTH), F32),
                 jax.ShapeDtypeStruct((depth, b, SGU_CHUNK, SGU_WIDTH), F32))
    out_specs = (tok(SSD_INNER), tok(SGU_WIDTH), tok(POOL_WIDTH),
                 per_seq(SUBLANES, SSD_CONV_DIM), per_seq(SSD_INNER, SSD_STATE),
                 per_seq(2 * SUBLANES, POOL_WIDTH), per_seq(SGU_CHUNK, SGU_WIDTH))
    scratch = [pltpu.VMEM((tb, P_WIDTH), F32),
               pltpu.VMEM((tb, P_WIDTH), F32),
               pltpu.VMEM((SSD_STATE, SSD_INNER), F32),
               pltpu.VMEM((SUBLANES, SSD_CONV_DIM), F32),
               pltpu.VMEM((2 * SUBLANES, POOL_WIDTH), F32),
               pltpu.VMEM((tb, SSD_CONV_DIM), F32)]
    body = functools.partial(_branch_prompt_kernel, tb=tb, npair=npair)
    in_specs = [tok(D_MODEL), pl.BlockSpec((None, tb, D_MODEL), next_block)] + [_layer_spec(w, layer) for w in stacked]
    args = [x, x] + stacked
    aliases = {}
    if prev_states is not None:
        body = _skip_refs(body, len(args), N_STATE_OUTPUTS)
        aliases = {len(args) + k: 3 + k for k in range(N_STATE_OUTPUTS)}
        in_specs = in_specs + [pl.BlockSpec(memory_space=pl.ANY)] * N_STATE_OUTPUTS
        args = args + list(prev_states)
    ya, yb, yc, *states = pl.pallas_call(
        body, grid=(b, npair), in_specs=in_specs,
        out_specs=out_specs, out_shape=out_shape, scratch_shapes=scratch,
        input_output_aliases=aliases,
        compiler_params=pltpu.CompilerParams(dimension_semantics=("arbitrary", "arbitrary"),
                                             vmem_limit_bytes=VMEM_LIMIT),
        name="branch_prompt",
    )(*args)
    return ya, yb, yc, states


def _branch_sample_kernel(x_ref, convs_ref, ssms_ref, pools_in_ref, nm_ref, wxbc_ref, wdt_ref, wuv_ref,
                          wpool_ref, convw_ref, convb_ref, dtb_ref, alog_ref, dskip_ref,
                          lng_ref, lnb_ref, wd_ref, b8_ref, poolw_ref, pools_ref, e2_ref,
                          ya_ref, yb_ref, yc_ref, convo_ref, ssmo_ref, poolo_ref, vo_ref,
                          c_ref, bm_ref, xd_ref, daug_ref, yoff_ref, *, nb, seq, start):
    r = nb * seq
    assert seq == SUBLANES
    t128 = _row_iota((r, LANES)) % seq

    def tile_roll(v, j):
        width = v.shape[-1]
        return pltpu.roll(v.reshape(nb, seq, width), j, 1).reshape(r, width)

    def shift_rows(cur, prev, j, width):
        tt = _row_iota((r, width)) % seq
        if prev is None:
            return jnp.where(tt >= j, tile_roll(cur, j), 0.0)
        return jnp.where(tt >= j, tile_roll(cur, j), tile_roll(prev, j))

    hb = _rmsnorm(x_ref[...].reshape(r, D_MODEL), nm_ref[...]).astype(BF16)

    xbc = _dot(hb, wxbc_ref[...])
    cbuf = convs_ref[...].reshape(r, SSD_CONV_DIM)
    acc = xbc * convw_ref[SSD_CONV - 1:SSD_CONV, :]
    for k in range(SSD_CONV - 1):
        acc = acc + shift_rows(xbc, cbuf, SSD_CONV - 1 - k, SSD_CONV_DIM) * convw_ref[k:k + 1, :]
    xcv = _silu(acc + convb_ref[...])
    convo_ref[...] = xbc.reshape(nb, seq, SSD_CONV_DIM)
    xs = xcv[:, 0:SSD_INNER]
    bm = xcv[:, SSD_INNER:SSD_INNER + SSD_GROUPS * SSD_STATE]
    cm = xcv[:, SSD_INNER + SSD_GROUPS * SSD_STATE:SSD_CONV_DIM]
    dt = _softplus(_dot(hb, wdt_ref[...]) + dtb_ref[...])
    a_row = -jnp.exp(alog_ref[...])
    a_cs = dt * a_row
    for s in (1, 2, 4):
        a_cs = a_cs + jnp.where(t128 >= s, tile_roll(a_cs, s), 0.0)
    a3 = a_cs.reshape(nb, seq, LANES)
    tot = jnp.broadcast_to(a3[:, seq - 1:seq, :], (nb, seq, LANES)).reshape(r, LANES)
    e2 = e2_ref[...]
    exp_a_e = _expand_heads(jnp.exp(a_cs), e2)
    decst_e = _expand_heads(jnp.exp(tot - a_cs), e2)
    dt_e = _expand_heads(dt, e2)
    dectot_e = _expand_heads(jnp.exp(tot), e2)
    xdt = xs * dt_e
    lane = _lane_iota((r, LANES))
    y = jnp.zeros((r, SSD_INNER), F32)
    for j in range(seq):
        if j == 0:
            lj = jnp.ones((r, LANES), F32)
            b_sh, x_sh = bm, xdt
        else:
            lj = jnp.where(t128 >= j, jnp.exp(a_cs - tile_roll(a_cs, j)), 0.0)
            b_sh, x_sh = tile_roll(bm, j), tile_roll(xdt, j)
        prod = cm * b_sh
        cb0 = jnp.sum(prod[:, 0:SSD_STATE], axis=-1, keepdims=True)
        cb1 = jnp.sum(prod[:, SSD_STATE:2 * SSD_STATE], axis=-1, keepdims=True)
        mj = lj * jnp.where(lane < HEADS_PER_GROUP, cb0, cb1)
        y = y + _expand_heads(mj, e2) * x_sh
    c_ref[...] = cm
    bm_ref[...] = bm
    xd_ref[...] = xdt * decst_e
    dec_hi = dectot_e.astype(BF16).astype(F32)
    t1024 = _row_iota((r, SSD_INNER)) % seq
    daug_ref[...] = jnp.where(t1024 == 0, dec_hi, jnp.where(t1024 == 1, dectot_e - dec_hi, 0.0))
    t8 = _row_iota((seq, LANES))
    ones2 = jnp.where(t8 < 2, 1.0, 0.0)
    zeros8 = jnp.zeros((seq, LANES), F32)

    def per_seq(n, carry):
        rows = pl.ds(pl.multiple_of(n * seq, seq), seq)
        for g in range(SSD_GROUPS):
            cols = slice(g * GROUP_INNER, (g + 1) * GROUP_INNER)
            scol = slice(g * SSD_STATE, (g + 1) * SSD_STATE)
            h0 = ssms_ref[n, cols, :]
            cg = c_ref[rows, scol].astype(BF16)
            yoff_ref[rows, cols] = _dot_nt(cg, h0.astype(BF16))
            lhs = jnp.concatenate([xd_ref[rows, cols], daug_ref[rows, cols]], axis=0).astype(BF16)
            rhs = jnp.concatenate([jnp.concatenate([bm_ref[rows, scol], zeros8], axis=1),
                                   jnp.concatenate([zeros8, ones2], axis=1)], axis=0).astype(BF16)
            upd = _dot_tn(lhs, rhs)
            ssmo_ref[n, cols, :] = upd[:, SSD_STATE:] * h0 + upd[:, :SSD_STATE]
        return carry

    lax.fori_loop(0, nb, per_seq, 0, unroll=SAMPLE_UNROLL)
    ya_ref[...] = (y + yoff_ref[...] * exp_a_e + dskip_ref[...] * xs).astype(BF16)

    a = _gelu_tanh(_dot(hb, wuv_ref[...]))
    u = a[:, :SGU_WIDTH]
    v = a[:, SGU_WIDTH:]
    mu = jnp.mean(v, axis=-1, keepdims=True)
    vc = v - mu
    var = jnp.mean(vc * vc, axis=-1, keepdims=True)
    vn = vc * lax.rsqrt(var + EPS) * lng_ref[...] + lnb_ref[...]
    vo_ref[...] = vn
    s = vn.reshape(nb, seq, SGU_WIDTH) * wd_ref[0] + b8_ref[...]
    for j in range(1, seq):
        s = s + tile_roll(vn, j).reshape(nb, seq, SGU_WIDTH) * wd_ref[j]
    yb_ref[...] = (u * s.reshape(r, SGU_WIDTH)).astype(BF16)

    xc = _dot(hb, wpool_ref[...])
    pbuf = pools_in_ref[...]
    t0 = pbuf[:, 0:seq, :].reshape(r, POOL_WIDTH)
    t1 = pbuf[:, seq:2 * seq, :].reshape(r, POOL_WIDTH)
    poolo_ref[:, 0:seq, :] = pbuf[:, seq:2 * seq, :]
    poolo_ref[:, seq:2 * seq, :] = xc.reshape(nb, seq, POOL_WIDTH)
    pos = start + (_row_iota((r, POOL_GW)) % seq)
    yc_cols = []
    for g, w in enumerate(POOL_WINDOWS):
        cols = slice(g * POOL_GW, (g + 1) * POOL_GW)
        tiles = [t0[:, cols], t1[:, cols], xc[:, cols]]
        span = 1
        while span < min(w, seq):
            prev = [None] + tiles[:-1]
            tiles = [tl + shift_rows(tl, pv, span, POOL_GW) for tl, pv in zip(tiles, prev)]
            span *= 2
        wsum = tiles[2] if w <= seq else tiles[2] + tiles[1]
        cnt = jnp.minimum(pos + 1, w).astype(F32)
        d = wsum / cnt - xc[:, cols]
        yc_cols.append(_dot(d.astype(BF16), poolw_ref[g]))
    yc_ref[...] = (jnp.concatenate(yc_cols, axis=1) * pools_ref[...]).astype(BF16)


def _branch_sample(x, conv_pad, ssm_flat, pool_pad, sw, layer, prev_states, start):
    n, seq, _ = x.shape
    depth = sw['w_xbc'].shape[0]
    nb = SAMPLE_SEQS
    assert n % nb == 0 and seq == SUBLANES and start >= max(POOL_WINDOWS)
    r = nb * seq
    stacked = [sw[k] for k in BRANCH_WEIGHTS + ('w_diag8', 'b8', 'pool_w', 'pool_scale')]
    shared = [sw['e2']]
    seq3 = lambda rows, width: pl.BlockSpec((nb, rows, width), lambda i: (i, 0, 0))
    seq4 = lambda rows, width: pl.BlockSpec((None, nb, rows, width), lambda i: (layer, i, 0, 0))
    tok = lambda width: pl.BlockSpec((r, width), lambda i: (i, 0))
    out_shape = (jax.ShapeDtypeStruct((n * seq, SSD_INNER), BF16),
                 jax.ShapeDtypeStruct((n * seq, SGU_WIDTH), BF16),
                 jax.ShapeDtypeStruct((n * seq, POOL_WIDTH), BF16),
                 jax.ShapeDtypeStruct((depth, n, seq, SSD_CONV_DIM), F32),
                 jax.ShapeDtypeStruct((depth, n, SSD_INNER, SSD_STATE), F32),
                 jax.ShapeDtypeStruct((depth, n, 2 * seq, POOL_WIDTH), F32),
                 jax.ShapeDtypeStruct((depth, n * seq, SGU_WIDTH), F32))
    out_specs = (tok(SSD_INNER), tok(SGU_WIDTH), tok(POOL_WIDTH), seq4(seq, SSD_CONV_DIM),
                 seq4(SSD_INNER, SSD_STATE), seq4(2 * seq, POOL_WIDTH),
                 pl.BlockSpec((None, r, SGU_WIDTH), lambda i: (layer, i, 0)))
    scratch = [pltpu.VMEM((r, SSD_GROUPS * SSD_STATE), F32),
               pltpu.VMEM((r, SSD_GROUPS * SSD_STATE), F32),
               pltpu.VMEM((r, SSD_INNER), F32),
               pltpu.VMEM((r, SSD_INNER), F32),
               pltpu.VMEM((r, SSD_INNER), F32)]
    body = functools.partial(_branch_sample_kernel, nb=nb, seq=seq, start=start)
    in_specs = ([seq3(seq, D_MODEL), seq4(seq, SSD_CONV_DIM), seq4(SSD_INNER, SSD_STATE), seq4(2 * seq, POOL_WIDTH)]
                + [_layer_spec(w, layer) for w in stacked] + [_const_spec(w) for w in shared])
    args = [x, conv_pad, ssm_flat, pool_pad] + stacked + shared
    aliases = {}
    if prev_states is not None:
        body = _skip_refs(body, len(args), N_STATE_OUTPUTS)
        aliases = {len(args) + k: 3 + k for k in range(N_STATE_OUTPUTS)}
        in_specs = in_specs + [pl.BlockSpec(memory_space=pl.ANY)] * N_STATE_OUTPUTS
        args = args + list(prev_states)
    ya, yb, yc, *states = pl.pallas_call(
        body, grid=(n // nb,), in_specs=in_specs,
        out_specs=out_specs, out_shape=out_shape, scratch_shapes=scratch,
        input_output_aliases=aliases,
        compiler_params=pltpu.CompilerParams(dimension_semantics=("arbitrary",),
                                             vmem_limit_bytes=VMEM_LIMIT),
        name="branch_sample",
    )(*args)
    return ya, yb, yc, states


def _merge_kernel(x_ref, ya_ref, yb_ref, yc_ref, nm_ref, wz_ref, ssdn_ref, wg_ref, wa_ref, wb_ref, wc_ref, wo_ref,
                  o_ref):
    x = x_ref[...]
    hb = _rmsnorm(x, nm_ref[...]).astype(BF16)
    ya = _group_rmsnorm(ya_ref[...].astype(F32) * _silu(_dot(hb, wz_ref[...])), ssdn_ref[...]).astype(BF16)
    m = _sigmoid(_dot(hb, wg_ref[:, 0:D_MODEL])) * _dot(ya, wa_ref[...])
    m = m + _sigmoid(_dot(hb, wg_ref[:, D_MODEL:2 * D_MODEL])) * _dot(yb_ref[...], wb_ref[...])
    m = m + _sigmoid(_dot(hb, wg_ref[:, 2 * D_MODEL:3 * D_MODEL])) * _dot(yc_ref[...], wc_ref[...])
    o_ref[...] = x + _dot(m.astype(BF16), wo_ref[...])


def _merge(x, ya, yb, yc, sw, layer):
    t = x.shape[0]
    tm = min(2 * TOKEN_BLOCK, t)
    assert t % tm == 0
    stacked = [sw[k] for k in ('norm_mix', 'w_z', 'ssd_norm', 'w_gate', 'w_br_a', 'w_br_b', 'w_br_c', 'w_out')]
    tok = lambda width: pl.BlockSpec((tm, width), lambda i: (i, 0))
    return pl.pallas_call(
        _merge_kernel, grid=(t // tm,),
        in_specs=[tok(D_MODEL), tok(SSD_INNER), tok(SGU_WIDTH), tok(POOL_WIDTH)] + [_layer_spec(w, layer) for w in stacked],
        out_specs=tok(D_MODEL), out_shape=jax.ShapeDtypeStruct((t, D_MODEL), F32),
        compiler_params=pltpu.CompilerParams(dimension_semantics=("arbitrary",), vmem_limit_bytes=VMEM_LIMIT),
        name="merge",
    )(x, ya, yb, yc, *stacked)


def _ffn_kernel(x_ref, p_ref, nf_ref, wgu_ref, wd_ref, np_ref, wpg_ref, wpu_ref, fn_ref, o_ref, *, final):
    x = x_ref[...]
    hb = _rmsnorm(x, nf_ref[...]).astype(BF16)
    acc = x
    for cols in (slice(0, FF_SPLIT), slice(FF_SPLIT, D_FF)):
        up_cols = slice(D_FF + cols.start, D_FF + cols.stop)
        act = (_silu(_dot(hb, wgu_ref[:, cols])) * _dot(hb, wgu_ref[:, up_cols])).astype(BF16)
        acc = acc + _dot(act, wd_ref[cols, :])
    hb = _rmsnorm(acc, np_ref[...]).astype(BF16)
    out = acc + _dot(p_ref[...].astype(BF16), wpu_ref[...]) * _sigmoid(_dot(hb, wpg_ref[...]))
    if final:
        out = _rmsnorm(out, fn_ref[...])
    o_ref[...] = out


def _ffn(x, p, sw, layer, final):
    t = x.shape[0]
    tm = min(TOKEN_BLOCK, t)
    assert t % tm == 0
    stacked = [sw[k] for k in ('norm_ffn', 'w_gate_up', 'w_down', 'norm_ple', 'w_ple_gate', 'w_ple_up')]
    tok = lambda width: pl.BlockSpec((tm, width), lambda i: (i, 0))
    return pl.pallas_call(
        functools.partial(_ffn_kernel, final=final), grid=(t // tm,),
        in_specs=([tok(D_MODEL), pl.BlockSpec((None, tm, PLE_DIM), lambda i: (layer, i, 0))]
                  + [_layer_spec(w, layer) for w in stacked] + [_const_spec(sw['final_norm'])]),
        out_specs=tok(D_MODEL), out_shape=jax.ShapeDtypeStruct((t, D_MODEL), F32),
        compiler_params=pltpu.CompilerParams(dimension_semantics=("arbitrary",), vmem_limit_bytes=VMEM_LIMIT),
        name="ffn",
    )(x, p, *stacked, sw['final_norm'])


IN_DIM = O_GATE + 3 * D_MODEL
W_IN_ROWS = 256


def _split_w_in_kernel(wt_ref, z_ref, xbc_ref, dt_ref, uv_ref, pool_ref, gate_ref):
    z_ref[...] = wt_ref[O_Z:O_XBC, :].T.astype(BF16)
    xbc_ref[...] = wt_ref[O_XBC:O_DT, :].T.astype(BF16)
    dt = wt_ref[O_DT:O_DT + LANES, :].T
    dt_ref[...] = jnp.where(_lane_iota(dt.shape) < SSD_HEADS, dt, 0.0).astype(BF16)
    uv_ref[...] = wt_ref[O_UV:O_POOL, :].T.astype(BF16)
    pool_ref[...] = wt_ref[O_POOL:O_GATE, :].T.astype(BF16)
    gate_ref[...] = wt_ref[O_GATE:IN_DIM, :].T.astype(BF16)


def _split_w_in(w_in):
    depth, d, width = w_in.shape
    assert width == IN_DIM and d % W_IN_ROWS == 0
    widths = (O_XBC - O_Z, O_DT - O_XBC, LANES, O_POOL - O_UV, O_GATE - O_POOL, IN_DIM - O_GATE)
    spec = lambda w: pl.BlockSpec((None, W_IN_ROWS, w), lambda i, j: (i, j, 0))
    return pl.pallas_call(
        _split_w_in_kernel, grid=(depth, d // W_IN_ROWS),
        in_specs=[pl.BlockSpec((None, IN_DIM, W_IN_ROWS), lambda i, j: (i, 0, j))],
        out_specs=tuple(spec(w) for w in widths),
        out_shape=tuple(jax.ShapeDtypeStruct((depth, d, w), BF16) for w in widths),
        compiler_params=pltpu.CompilerParams(dimension_semantics=("arbitrary", "arbitrary"),
                                             vmem_limit_bytes=VMEM_LIMIT),
        name="split_w_in",
    )(jnp.swapaxes(w_in, 1, 2))


def _head_expansion():
    rows = jnp.arange(2 * LANES)[:, None] % LANES
    cols = jnp.arange(SSD_INNER)[None, :] // SSD_HEAD_DIM
    return (rows == cols).astype(BF16)


def _prepare_weights(norm_mix, w_in, conv_w, conv_b, dt_bias, a_log, d_skip, ssd_norm, sgu_ln_g, sgu_ln_b,
                     w_spatial, b_spatial, pool_w, pool_scale, w_br_a, w_br_b, w_br_c, w_out, norm_ffn,
                     w_gate_up, w_down, norm_ple, w_ple_gate, w_ple_up, final_norm):
    depth = w_in.shape[0]
    row = lambda v: v.reshape(depth, 1, -1).astype(F32)
    pad_heads = lambda v: jnp.pad(v.reshape(depth, 1, -1).astype(F32), ((0, 0), (0, 0), (0, LANES - SSD_HEADS)))
    seq = SUBLANES
    w8 = w_spatial[:, :, :seq, :seq]
    tt = jnp.arange(seq)
    lag = tt[None, :] - tt[:, None]
    w_lag = jnp.where(lag >= 0, w8[:, :, tt[None, :], jnp.clip(lag, 0, seq - 1)], 0.0)
    w_diag8 = jnp.repeat(jnp.transpose(w_lag, (0, 2, 3, 1)), SGU_GW, axis=3).astype(F32)
    b_full = jnp.repeat(jnp.transpose(b_spatial, (0, 2, 1)), SGU_GW, axis=2).astype(F32)
    w_z, w_xbc, w_dt, w_uv, w_pool, w_gate = _split_w_in(w_in)
    return {
        'norm_mix': row(norm_mix),
        'w_z': w_z, 'w_xbc': w_xbc, 'w_dt': w_dt, 'w_uv': w_uv, 'w_pool': w_pool, 'w_gate': w_gate,
        'conv_w': conv_w.astype(F32), 'conv_b': row(conv_b),
        'dt_bias': pad_heads(dt_bias), 'a_log': pad_heads(a_log),
        'd_skip': row(jnp.repeat(d_skip, SSD_HEAD_DIM, axis=1)),
        'ssd_norm': row(ssd_norm),
        'ln_g': row(sgu_ln_g), 'ln_b': row(sgu_ln_b),
        'w_sp': w_spatial.astype(F32), 'b_sp': b_full,
        'w_diag8': w_diag8, 'b8': b_full[:, :seq],
        'pool_w': pool_w.astype(BF16), 'pool_scale': row(pool_scale),
        'w_br_a': w_br_a.astype(BF16), 'w_br_b': w_br_b.astype(BF16), 'w_br_c': w_br_c.astype(BF16),
        'w_out': w_out.astype(BF16),
        'norm_ffn': row(norm_ffn),
        'w_gate_up': w_gate_up.astype(BF16),
        'w_down': w_down.astype(BF16),
        'norm_ple': row(norm_ple),
        'w_ple_gate': w_ple_gate.astype(BF16), 'w_ple_up': w_ple_up.astype(BF16),
        'final_norm': final_norm.reshape(1, -1).astype(F32),
        'e2': _head_expansion(),
    }


def kernel(x_prompt, x_sample, state_conv, state_ssm, state_pool, p_prompt, p_sample, norm_mix, w_in, conv_w, conv_b, dt_bias, a_log, d_skip, ssd_norm, sgu_ln_g, sgu_ln_b, w_spatial, b_spatial, pool_w, pool_scale, w_br_a, w_br_b, w_br_c, w_out, norm_ffn, w_gate_up, w_down, norm_ple, w_ple_gate, w_ple_up, final_norm):
    b, l, _ = x_prompt.shape
    n, s, _ = x_sample.shape
    depth = w_in.shape[0]
    sw = _prepare_weights(norm_mix, w_in, conv_w, conv_b, dt_bias, a_log, d_skip, ssd_norm, sgu_ln_g, sgu_ln_b,
                          w_spatial, b_spatial, pool_w, pool_scale, w_br_a, w_br_b, w_br_c, w_out, norm_ffn,
                          w_gate_up, w_down, norm_ple, w_ple_gate, w_ple_up, final_norm)
    conv_pad = jnp.pad(state_conv, ((0, 0), (0, 0), (SUBLANES - (SSD_CONV - 1), 0), (0, 0)))
    pool_pad = jnp.pad(state_pool, ((0, 0), (0, 0), (2 * SUBLANES - POOL_BUF, 0), (0, 0)))
    ssm_flat = state_ssm.reshape(depth, n, SSD_INNER, SSD_STATE)
    pp = p_prompt.reshape(depth, b * l, PLE_DIM)
    ps = p_sample.reshape(depth, n * s, PLE_DIM)
    xp = x_prompt
    xs = x_sample.reshape(n * s, D_MODEL)
    st_p = None
    st_s = None
    for i in range(depth):
        final = i == depth - 1
        ya, yb, yc, st_p = _branch_prompt(xp, sw, i, st_p)
        x1 = _merge(xp.reshape(b * l, D_MODEL), ya.reshape(b * l, -1), yb.reshape(b * l, -1), yc.reshape(b * l, -1), sw, i)
        xp = _ffn(x1, pp, sw, i, final).reshape(b, l, D_MODEL)
        ya, yb, yc, st_s = _branch_sample(xs.reshape(n, s, D_MODEL), conv_pad, ssm_flat, pool_pad, sw, i, st_s, PAST_LEN)
        x1 = _merge(xs, ya, yb, yc, sw, i)
        xs = _ffn(x1, ps, sw, i, final)
    conv_p, ssm_p, pool_p, v_p = st_p
    conv_s, ssm_s, pool_s, v_s = st_s
    return (xp, xs.reshape(n, s, D_MODEL),
            conv_p[:, :, SUBLANES - (SSD_CONV - 1):, :],
            ssm_p.reshape(depth, b, SSD_HEADS, SSD_HEAD_DIM, SSD_STATE),
            pool_p[:, :, 2 * SUBLANES - POOL_BUF:, :],
            v_p,
            conv_s[:, :, s - (SSD_CONV - 1):, :],
            ssm_s.reshape(depth, n, SSD_HEADS, SSD_HEAD_DIM, SSD_STATE),
            pool_s[:, :, 2 * s - POOL_BUF:, :],
            v_s.reshape(depth, n, s, SGU_WIDTH))
```

```python
import functools
import math

import jax
import jax.numpy as jnp
from jax import lax
from jax.experimental import pallas as pl
from jax.experimental.pallas import tpu as pltpu

F32 = jnp.float32
BF16 = jnp.bfloat16

D_MODEL = 1024
DEPTH = 4
PAST_LEN = 16384
SSD_HEAD_DIM = 64
SSD_HEADS = 16
SSD_INNER = 1024
SSD_GROUPS = 2
SSD_STATE = 128
SSD_CONV = 4
SSD_CHUNK = 128
SSD_CONV_DIM = 1536
SGU_WIDTH = 512
SGU_GROUPS = 4
SGU_CHUNK = 128
SGU_GW = 128
POOL_WIDTH = 512
POOL_WINDOWS = (2, 4, 8, 16)
POOL_GW = 128
POOL_BUF = 15
D_FF = 2816
PLE_DIM = 256
EPS = 1e-6
LOG2_E = 1.4426950408889634
O_Z = 0
O_XBC = 1024
O_DT = 2560
O_UV = 2576
O_POOL = 3600
O_GATE = 4112

LANES = 128
SUBLANES = 8
HEADS_PER_GROUP = SSD_HEADS // SSD_GROUPS
GROUP_INNER = SSD_INNER // SSD_GROUPS
VMEM_LIMIT = 56 * 1024 * 1024

PROMPT_BLOCK = 256
SAMPLE_SEQS = 16
SAMPLE_UNROLL = 8
TOKEN_BLOCK = 512
MERGE_BLOCK = 1024
MXU_DEPTH = 256
FF_SPLIT = (D_FF // MXU_DEPTH + 1) // 2 * MXU_DEPTH


def _dot(a, b):
    return jnp.dot(a, b, preferred_element_type=F32)


def _dot_nt(a, b):
    return lax.dot_general(a, b, (((1,), (1,)), ((), ())), preferred_element_type=F32)


def _dot_tn(a, b):
    return lax.dot_general(a, b, (((0,), (0,)), ((), ())), preferred_element_type=F32)


def _sigmoid(x):
    return 0.5 * jnp.tanh(0.5 * x) + 0.5


def _silu(x):
    h = 0.5 * x
    return h * jnp.tanh(h) + h


def _gelu_tanh(x):
    c = math.sqrt(2.0 / math.pi)
    return 0.5 * x * (1.0 + jnp.tanh(c * (x + 0.044715 * (x * x * x))))


def _softplus(x):
    return jnp.maximum(x, 0.0) + jnp.log1p(jnp.exp(-jnp.abs(x)))


def _rmsnorm(x, g):
    ms = jnp.mean(x * x, axis=-1, keepdims=True)
    return x * lax.rsqrt(ms + EPS) * g


def _split2(v):
    hi = v.astype(BF16)
    lo = (v - hi.astype(F32)).astype(BF16)
    return jnp.concatenate([hi, lo], axis=-1)


def _expand_heads(v, e2):
    return _dot(_split2(v), e2)


def _group_rmsnorm(y, g):
    parts = []
    for k in range(SSD_GROUPS):
        yk = y[:, k * GROUP_INNER:(k + 1) * GROUP_INNER]
        ms = jnp.mean(yk * yk, axis=-1, keepdims=True)
        parts.append(yk * lax.rsqrt(ms + EPS))
    return jnp.concatenate(parts, axis=-1) * g


def _row_iota(shape):
    return lax.broadcasted_iota(jnp.int32, shape, 0)


def _lane_iota(shape):
    return lax.broadcasted_iota(jnp.int32, shape, 1)


def _ssd_chunk(xs_bf, bm, cm, dtc, a_row, ht_ref):
    q = SSD_CHUNK
    tri = _row_iota((q, q)) >= _lane_iota((q, q))
    tril_bf = jnp.where(tri, 1.0, 0.0).astype(BF16)
    da = dtc * a_row
    hi = da.astype(BF16)
    r1 = da - hi.astype(F32)
    mid = r1.astype(BF16)
    lo = (r1 - mid.astype(F32)).astype(BF16)
    cs3 = _dot(tril_bf, jnp.concatenate([hi, mid, lo], axis=1))
    a_cs = (cs3[:, :LANES] + cs3[:, LANES:2 * LANES] + cs3[:, 2 * LANES:]) * LOG2_E
    a_t = a_cs.T
    ap_t = (a_cs - jnp.log2(dtc)).T
    w_t = jnp.exp2(a_t[:, q - 1:q] - ap_t)
    c_bf = cm.astype(BF16)
    lane = _lane_iota((q, LANES))
    ys = []
    for g in range(SSD_GROUPS):
        scol = slice(g * SSD_STATE, (g + 1) * SSD_STATE)
        cg = c_bf[:, scol]
        cb = _dot_nt(cg, bm[:, scol].astype(BF16))
        bg_t = bm[:, scol].T
        ht_g = ht_ref[:, g * GROUP_INNER:(g + 1) * GROUP_INNER]
        yoff = _dot(cg, ht_g.astype(BF16))
        for pr in range(HEADS_PER_GROUP // 2):
            h0 = g * HEADS_PER_GROUP + 2 * pr
            acols, ms, bws = [], [], []
            for h in (h0, h0 + 1):
                acol = jnp.broadcast_to(a_cs[:, h:h + 1], (q, q))
                acols.append(acol)
                ms.append((jnp.where(tri, jnp.exp2(acol - ap_t[h:h + 1, :]), 0.0) * cb).astype(BF16))
                bws.append((bg_t * w_t[h:h + 1, :]).astype(BF16))
            lhs = jnp.concatenate([jnp.concatenate(ms, axis=1), jnp.concatenate(bws, axis=1)], axis=0)
            xpair = xs_bf[:, h0 * SSD_HEAD_DIM:(h0 + 2) * SSD_HEAD_DIM]
            zero = jnp.zeros_like(xpair)
            rhs = jnp.concatenate([jnp.where(lane < SSD_HEAD_DIM, xpair, zero),
                                   jnp.where(lane >= SSD_HEAD_DIM, xpair, zero)], axis=0)
            out = _dot(lhs, rhs)
            ea = jnp.exp2(jnp.where(lane < SSD_HEAD_DIM, acols[0], acols[1]))
            ys.append(out[:q] + yoff[:, pr * LANES:(pr + 1) * LANES] * ea)
            cols = slice(h0 * SSD_HEAD_DIM, (h0 + 2) * SSD_HEAD_DIM)
            ht_ref[:, cols] = ht_g[:, pr * LANES:(pr + 1) * LANES] * ea[q - 1:q, :] + out[q:]
    return jnp.concatenate(ys, axis=1)


P_XBC = 0
P_DT = P_XBC + SSD_CONV_DIM
P_UV = P_DT + LANES
P_POOL = P_UV + 2 * SGU_WIDTH
P_WIDTH = P_POOL + POOL_WIDTH
PROJ_PIECE = 512


def _branch_prompt_kernel(xa_ref, xn_ref, nm_ref, wxbc_ref, wdt_ref, wuv_ref, wpool_ref,
                          convw_ref, convb_ref, dtb_ref, alog_ref, dskip_ref,
                          lng_ref, lnb_ref, wsp_ref, bsp_ref, poolw_ref, pools_ref,
                          ya_ref, yb_ref, yc_ref, convo_ref, ssmo_ref, poolo_ref, vo_ref,
                          pa_ref, pb_ref, ht_ref, cc_ref, pc_ref, xcv_ref, *, tb, npair):
    i = pl.program_id(0)
    j = pl.program_id(1)
    nchunk = tb // SSD_CHUNK

    def project_pieces(x, p_ref):
        hb = _rmsnorm(x, nm_ref[...]).astype(BF16)
        pieces = []
        for w_ref, base, width in ((wxbc_ref, P_XBC, SSD_CONV_DIM), (wdt_ref, P_DT, LANES),
                                   (wpool_ref, P_POOL, POOL_WIDTH), (wuv_ref, P_UV, 2 * SGU_WIDTH)):
            for c0 in range(0, width, PROJ_PIECE):
                c1 = min(c0 + PROJ_PIECE, width)

                def piece(w_ref=w_ref, base=base, c0=c0, c1=c1):
                    p_ref[:, base + c0:base + c1] = _dot(hb, w_ref[:, c0:c1])
                pieces.append(piece)
        return pieces

    def process_phases(p_ref, half):
        orow = slice(half * tb, (half + 1) * tb)
        res = {}
        phases = []

        def conv(cols):
            xp = jnp.concatenate([cc_ref[:, cols], p_ref[:, P_XBC + cols.start:P_XBC + cols.stop]], axis=0)
            acc = xp[SUBLANES:, :] * convw_ref[SSD_CONV - 1:SSD_CONV, cols]
            for k in range(SSD_CONV - 1):
                acc = acc + pltpu.roll(xp, SSD_CONV - 1 - k, 0)[SUBLANES:, :] * convw_ref[k:k + 1, cols]
            xcv_ref[:, cols] = _silu(acc + convb_ref[:, cols])
        for c0 in range(0, SSD_CONV_DIM, PROJ_PIECE):
            phases.append(functools.partial(conv, slice(c0, c0 + PROJ_PIECE)))

        def scan(c):
            rows = slice(c * SSD_CHUNK, (c + 1) * SSD_CHUNK)
            xs = xcv_ref[rows, 0:SSD_INNER]
            y = _ssd_chunk(xs.astype(BF16),
                           xcv_ref[rows, SSD_INNER:SSD_INNER + SSD_GROUPS * SSD_STATE],
                           xcv_ref[rows, SSD_INNER + SSD_GROUPS * SSD_STATE:SSD_CONV_DIM],
                           _softplus(p_ref[rows, P_DT:P_UV] + dtb_ref[...]),
                           -jnp.exp(alog_ref[...]), ht_ref)
            ya_ref[half * tb + c * SSD_CHUNK:half * tb + (c + 1) * SSD_CHUNK, :] = (
                y + dskip_ref[...] * xs).astype(BF16)
            if c == nchunk - 1:
                res['tail'] = p_ref[tb - SUBLANES:tb, P_XBC:P_DT]
                cc_ref[...] = res['tail']
        for c in range(nchunk):
            phases.append(functools.partial(scan, c))

        def spatial():
            u = _gelu_tanh(p_ref[:, P_UV:P_UV + SGU_WIDTH])
            v = _gelu_tanh(p_ref[:, P_UV + SGU_WIDTH:P_POOL])
            mu = jnp.mean(v, axis=-1, keepdims=True)
            vc = v - mu
            var = jnp.mean(vc * vc, axis=-1, keepdims=True)
            vn = vc * lax.rsqrt(var + EPS) * lng_ref[...] + lnb_ref[...]
            res['vn'] = vn
            vn_bf = vn.astype(BF16)
            tri = _row_iota((SGU_CHUNK, SGU_CHUNK)) >= _lane_iota((SGU_CHUNK, SGU_CHUNK))
            s_groups = []
            for g in range(SGU_GROUPS):
                wm = jnp.where(tri, wsp_ref[g], 0.0).astype(BF16)
                vg = jnp.concatenate([vn_bf[c * SGU_CHUNK:(c + 1) * SGU_CHUNK, g * SGU_GW:(g + 1) * SGU_GW]
                                      for c in range(nchunk)], axis=1)
                s_groups.append(_dot(wm, vg))
            s = jnp.concatenate([jnp.concatenate([sg[:, c * SGU_GW:(c + 1) * SGU_GW] for sg in s_groups], axis=1)
                                 + bsp_ref[...] for c in range(nchunk)], axis=0)
            yb_ref[orow, :] = (u * s).astype(BF16)
        phases.append(spatial)

        def pool():
            pos = (2 * j + half) * tb + _row_iota((tb, POOL_GW))
            yc_cols = []
            for g, w in enumerate(POOL_WINDOWS):
                cols = slice(P_POOL + g * POOL_GW, P_POOL + (g + 1) * POOL_GW)
                cur = p_ref[:, cols]
                wsum = jnp.concatenate([pc_ref[:, g * POOL_GW:(g + 1) * POOL_GW], cur], axis=0)
                span = 1
                while span < w:
                    wsum = wsum + pltpu.roll(wsum, span, 0)
                    span *= 2
                cnt = jnp.minimum(pos + 1, w).astype(F32)
                d = wsum[2 * SUBLANES:, :] / cnt - cur
                yc_cols.append(_dot(d.astype(BF16), poolw_ref[g]))
            yc_ref[orow, :] = (jnp.concatenate(yc_cols, axis=1) * pools_ref[...]).astype(BF16)
            res['ptail'] = p_ref[tb - 2 * SUBLANES:tb, P_POOL:P_WIDTH]
            pc_ref[...] = res['ptail']
        phases.append(pool)
        return phases, res

    def run(p_cur, half, x_next, p_next):
        phases, res = process_phases(p_cur, half)
        pieces = project_pieces(x_next, p_next)
        done = 0
        for k, phase in enumerate(phases):
            phase()
            upto = (k + 1) * len(pieces) // len(phases)
            for piece in pieces[done:upto]:
                piece()
            done = upto
        return res

    @pl.when(jnp.logical_and(i == 0, j == 0))
    def _():
        for piece in project_pieces(xa_ref[0:tb, :], pa_ref):
            piece()

    @pl.when(j == 0)
    def _():
        ht_ref[...] = jnp.zeros_like(ht_ref)
        cc_ref[...] = jnp.zeros_like(cc_ref)
        pc_ref[...] = jnp.zeros_like(pc_ref)

    run(pa_ref, 0, xa_ref[tb:2 * tb, :], pb_ref)
    res = run(pb_ref, 1, xn_ref[...], pa_ref)

    @pl.when(j == npair - 1)
    def _():
        convo_ref[...] = res['tail']
        ssmo_ref[...] = ht_ref[...].T
        poolo_ref[...] = res['ptail']
        vo_ref[...] = res['vn'][tb - SGU_CHUNK:tb, :]


def _const_spec(arr):
    nd = arr.ndim
    return pl.BlockSpec(arr.shape, lambda *_: (0,) * nd, pipeline_mode=pl.Buffered(1))


def _layer_spec(arr, layer):
    nd = arr.ndim - 1
    return pl.BlockSpec((None,) + arr.shape[1:], lambda *_: (layer,) + (0,) * nd, pipeline_mode=pl.Buffered(1))


def _skip_refs(body, first, count):
    def wrapped(*refs):
        return body(*refs[:first], *refs[first + count:])
    return wrapped


BRANCH_WEIGHTS = ('norm_mix', 'w_xbc', 'w_dt', 'w_uv', 'w_pool', 'conv_w', 'conv_b', 'dt_bias', 'a_log',
                  'd_skip', 'ln_g', 'ln_b')
N_STATE_OUTPUTS = 4


def _branch_prompt(x, sw, layer, prev_states):
    b, l, _ = x.shape
    depth = sw['w_xbc'].shape[0]
    tb = PROMPT_BLOCK
    npair = l // (2 * tb)
    assert l % (2 * tb) == 0 and tb % SSD_CHUNK == 0 and tb >= 2 * SUBLANES
    stacked = [sw[k] for k in BRANCH_WEIGHTS + ('w_sp', 'b_sp', 'pool_w', 'pool_scale')]
    tok = lambda width: pl.BlockSpec((None, 2 * tb, width), lambda i, j: (i, j, 0))

    def next_block(i, j):
        flat = jnp.minimum(i * npair + j + 1, b * npair - 1)
        return flat // npair, (flat % npair) * 2, 0

    per_seq = lambda rows, width: pl.BlockSpec((None, None, rows, width), lambda i, j: (layer, i, 0, 0))
    out_shape = (jax.ShapeDtypeStruct((b, l, SSD_INNER), BF16),
                 jax.ShapeDtypeStruct((b, l, SGU_WIDTH), BF16),
                 jax.ShapeDtypeStruct((b, l, POOL_WIDTH), BF16),
                 jax.ShapeDtypeStruct((depth, b, SUBLANES, SSD_CONV_DIM), F32),
                 jax.ShapeDtypeStruct((depth, b, SSD_INNER, SSD_STATE), F32),
                 jax.ShapeDtypeStruct((depth, b, 2 * SUBLANES, POOL_WIDTH), F32),
                 jax.ShapeDtypeStruct((depth, b, SGU_CHUNK, SGU_WIDTH), F32))
    out_specs = (tok(SSD_INNER), tok(SGU_WIDTH), tok(POOL_WIDTH),
                 per_seq(SUBLANES, SSD_CONV_DIM), per_seq(SSD_INNER, SSD_STATE),
                 per_seq(2 * SUBLANES, POOL_WIDTH), per_seq(SGU_CHUNK, SGU_WIDTH))
    scratch = [pltpu.VMEM((tb, P_WIDTH), F32),
               pltpu.VMEM((tb, P_WIDTH), F32),
               pltpu.VMEM((SSD_STATE, SSD_INNER), F32),
               pltpu.VMEM((SUBLANES, SSD_CONV_DIM), F32),
               pltpu.VMEM((2 * SUBLANES, POOL_WIDTH), F32),
               pltpu.VMEM((tb, SSD_CONV_DIM), F32)]
    body = functools.partial(_branch_prompt_kernel, tb=tb, npair=npair)
    in_specs = [tok(D_MODEL), pl.BlockSpec((None, tb, D_MODEL), next_block)] + [_layer_spec(w, layer) for w in stacked]
    args = [x, x] + stacked
    aliases = {}
    if prev_states is not None:
        body = _skip_refs(body, len(args), N_STATE_OUTPUTS)
        aliases = {len(args) + k: 3 + k for k in range(N_STATE_OUTPUTS)}
        in_specs = in_specs + [pl.BlockSpec(memory_space=pl.ANY)] * N_STATE_OUTPUTS
        args = args + list(prev_states)
    ya, yb, yc, *states = pl.pallas_call(
        body, grid=(b, npair), in_specs=in_specs,
        out_specs=out_specs, out_shape=out_shape, scratch_shapes=scratch,
        input_output_aliases=aliases,
        compiler_params=pltpu.CompilerParams(dimension_semantics=("arbitrary", "arbitrary"),
                                             vmem_limit_bytes=VMEM_LIMIT),
        name="branch_prompt",
    )(*args)
    return ya, yb, yc, states


def _branch_sample_kernel(x_ref, convs_ref, ssms_ref, pools_in_ref, nm_ref, wxbc_ref, wdt_ref, wuv_ref,
                          wpool_ref, convw_ref, convb_ref, dtb_ref, alog_ref, dskip_ref,
                          lng_ref, lnb_ref, wd_ref, b8_ref, poolw_ref, pools_ref, e2_ref,
                          ya_ref, yb_ref, yc_ref, convo_ref, ssmo_ref, poolo_ref, vo_ref,
                          c_ref, bm_ref, xd_ref, daug_ref, yoff_ref, *, nb, seq, start):
    r = nb * seq
    assert seq == SUBLANES
    t128 = _row_iota((r, LANES)) % seq

    def tile_roll(v, j):
        width = v.shape[-1]
        return pltpu.roll(v.reshape(nb, seq, width), j, 1).reshape(r, width)

    def shift_rows(cur, prev, j, width):
        tt = _row_iota((r, width)) % seq
        if prev is None:
            return jnp.where(tt >= j, tile_roll(cur, j), 0.0)
        return jnp.where(tt >= j, tile_roll(cur, j), tile_roll(prev, j))

    hb = _rmsnorm(x_ref[...].reshape(r, D_MODEL), nm_ref[...]).astype(BF16)

    xbc = _dot(hb, wxbc_ref[...])
    cbuf = convs_ref[...].reshape(r, SSD_CONV_DIM)
    acc = xbc * convw_ref[SSD_CONV - 1:SSD_CONV, :]
    for k in range(SSD_CONV - 1):
        acc = acc + shift_rows(xbc, cbuf, SSD_CONV - 1 - k, SSD_CONV_DIM) * convw_ref[k:k + 1, :]
    xcv = _silu(acc + convb_ref[...])
    convo_ref[...] = xbc.reshape(nb, seq, SSD_CONV_DIM)
    xs = xcv[:, 0:SSD_INNER]
    bm = xcv[:, SSD_INNER:SSD_INNER + SSD_GROUPS * SSD_STATE]
    cm = xcv[:, SSD_INNER + SSD_GROUPS * SSD_STATE:SSD_CONV_DIM]
    dt = _softplus(_dot(hb, wdt_ref[...]) + dtb_ref[...])
    a_row = -jnp.exp(alog_ref[...])
    a_cs = dt * a_row
    for s in (1, 2, 4):
        a_cs = a_cs + jnp.where(t128 >= s, tile_roll(a_cs, s), 0.0)
    a3 = a_cs.reshape(nb, seq, LANES)
    tot = jnp.broadcast_to(a3[:, seq - 1:seq, :], (nb, seq, LANES)).reshape(r, LANES)
    e2 = e2_ref[...]
    exp_a_e = _expand_heads(jnp.exp(a_cs), e2)
    decst_e = _expand_heads(jnp.exp(tot - a_cs), e2)
    dt_e = _expand_heads(dt, e2)
    dectot_e = _expand_heads(jnp.exp(tot), e2)
    xdt = xs * dt_e
    lane = _lane_iota((r, LANES))
    y = jnp.zeros((r, SSD_INNER), F32)
    for j in range(seq):
        if j == 0:
            lj = jnp.ones((r, LANES), F32)
            b_sh, x_sh = bm, xdt
        else:
            lj = jnp.where(t128 >= j, jnp.exp(a_cs - tile_roll(a_cs, j)), 0.0)
            b_sh, x_sh = tile_roll(bm, j), tile_roll(xdt, j)
        prod = cm * b_sh
        cb0 = jnp.sum(prod[:, 0:SSD_STATE], axis=-1, keepdims=True)
        cb1 = jnp.sum(prod[:, SSD_STATE:2 * SSD_STATE], axis=-1, keepdims=True)
        mj = lj * jnp.where(lane < HEADS_PER_GROUP, cb0, cb1)
        y = y + _expand_heads(mj, e2) * x_sh
    c_ref[...] = cm
    bm_ref[...] = bm
    xd_ref[...] = xdt * decst_e
    dec_hi = dectot_e.astype(BF16).astype(F32)
    t1024 = _row_iota((r, SSD_INNER)) % seq
    daug_ref[...] = jnp.where(t1024 == 0, dec_hi, jnp.where(t1024 == 1, dectot_e - dec_hi, 0.0))
    t8 = _row_iota((seq, LANES))
    ones2 = jnp.where(t8 < 2, 1.0, 0.0)
    zeros8 = jnp.zeros((seq, LANES), F32)

    def per_seq(n, carry):
        rows = pl.ds(pl.multiple_of(n * seq, seq), seq)
        for g in range(SSD_GROUPS):
            cols = slice(g * GROUP_INNER, (g + 1) * GROUP_INNER)
            scol = slice(g * SSD_STATE, (g + 1) * SSD_STATE)
            h0 = ssms_ref[n, cols, :]
            cg = c_ref[rows, scol].astype(BF16)
            yoff_ref[rows, cols] = _dot_nt(cg, h0.astype(BF16))
            lhs = jnp.concatenate([xd_ref[rows, cols], daug_ref[rows, cols]], axis=0).astype(BF16)
            rhs = jnp.concatenate([jnp.concatenate([bm_ref[rows, scol], zeros8], axis=1),
                                   jnp.concatenate([zeros8, ones2], axis=1)], axis=0).astype(BF16)
            upd = _dot_tn(lhs, rhs)
            ssmo_ref[n, cols, :] = upd[:, SSD_STATE:] * h0 + upd[:, :SSD_STATE]
        return carry

    lax.fori_loop(0, nb, per_seq, 0, unroll=SAMPLE_UNROLL)
    ya_ref[...] = (y + yoff_ref[...] * exp_a_e + dskip_ref[...] * xs).astype(BF16)

    a = _gelu_tanh(_dot(hb, wuv_ref[...]))
    u = a[:, :SGU_WIDTH]
    v = a[:, SGU_WIDTH:]
    mu = jnp.mean(v, axis=-1, keepdims=True)
    vc = v - mu
    var = jnp.mean(vc * vc, axis=-1, keepdims=True)
    vn = vc * lax.rsqrt(var + EPS) * lng_ref[...] + lnb_ref[...]
    vo_ref[...] = vn
    s = vn.reshape(nb, seq, SGU_WIDTH) * wd_ref[0] + b8_ref[...]
    for j in range(1, seq):
        s = s + tile_roll(vn, j).reshape(nb, seq, SGU_WIDTH) * wd_ref[j]
    yb_ref[...] = (u * s.reshape(r, SGU_WIDTH)).astype(BF16)

    xc = _dot(hb, wpool_ref[...])
    pbuf = pools_in_ref[...]
    t0 = pbuf[:, 0:seq, :].reshape(r, POOL_WIDTH)
    t1 = pbuf[:, seq:2 * seq, :].reshape(r, POOL_WIDTH)
    poolo_ref[:, 0:seq, :] = pbuf[:, seq:2 * seq, :]
    poolo_ref[:, seq:2 * seq, :] = xc.reshape(nb, seq, POOL_WIDTH)
    pos = start + (_row_iota((r, POOL_GW)) % seq)
    yc_cols = []
    for g, w in enumerate(POOL_WINDOWS):
        cols = slice(g * POOL_GW, (g + 1) * POOL_GW)
        tiles = [t0[:, cols], t1[:, cols], xc[:, cols]]
        span = 1
        while span < min(w, seq):
            prev = [None] + tiles[:-1]
            tiles = [tl + shift_rows(tl, pv, span, POOL_GW) for tl, pv in zip(tiles, prev)]
            span *= 2
        wsum = tiles[2] if w <= seq else tiles[2] + tiles[1]
        cnt = jnp.minimum(pos + 1, w).astype(F32)
        d = wsum / cnt - xc[:, cols]
        yc_cols.append(_dot(d.astype(BF16), poolw_ref[g]))
    yc_ref[...] = (jnp.concatenate(yc_cols, axis=1) * pools_ref[...]).astype(BF16)


def _branch_sample(x, conv_pad, ssm_flat, pool_pad, sw, layer, prev_states, start):
    n, seq, _ = x.shape
    depth = sw['w_xbc'].shape[0]
    nb = SAMPLE_SEQS
    assert n % nb == 0 and seq == SUBLANES and start >= max(POOL_WINDOWS)
    r = nb * seq
    stacked = [sw[k] for k in BRANCH_WEIGHTS + ('w_diag8', 'b8', 'pool_w', 'pool_scale')]
    shared = [sw['e2']]
    seq3 = lambda rows, width: pl.BlockSpec((nb, rows, width), lambda i: (i, 0, 0))
    seq4 = lambda rows, width: pl.BlockSpec((None, nb, rows, width), lambda i: (layer, i, 0, 0))
    tok = lambda width: pl.BlockSpec((r, width), lambda i: (i, 0))
    out_shape = (jax.ShapeDtypeStruct((n * seq, SSD_INNER), BF16),
                 jax.ShapeDtypeStruct((n * seq, SGU_WIDTH), BF16),
                 jax.ShapeDtypeStruct((n * seq, POOL_WIDTH), BF16),
                 jax.ShapeDtypeStruct((depth, n, seq, SSD_CONV_DIM), F32),
                 jax.ShapeDtypeStruct((depth, n, SSD_INNER, SSD_STATE), F32),
                 jax.ShapeDtypeStruct((depth, n, 2 * seq, POOL_WIDTH), F32),
                 jax.ShapeDtypeStruct((depth, n * seq, SGU_WIDTH), F32))
    out_specs = (tok(SSD_INNER), tok(SGU_WIDTH), tok(POOL_WIDTH), seq4(seq, SSD_CONV_DIM),
                 seq4(SSD_INNER, SSD_STATE), seq4(2 * seq, POOL_WIDTH),
                 pl.BlockSpec((None, r, SGU_WIDTH), lambda i: (layer, i, 0)))
    scratch = [pltpu.VMEM((r, SSD_GROUPS * SSD_STATE), F32),
               pltpu.VMEM((r, SSD_GROUPS * SSD_STATE), F32),
               pltpu.VMEM((r, SSD_INNER), F32),
               pltpu.VMEM((r, SSD_INNER), F32),
               pltpu.VMEM((r, SSD_INNER), F32)]
    body = functools.partial(_branch_sample_kernel, nb=nb, seq=seq, start=start)
    in_specs = ([seq3(seq, D_MODEL), seq4(seq, SSD_CONV_DIM), seq4(SSD_INNER, SSD_STATE), seq4(2 * seq, POOL_WIDTH)]
                + [_layer_spec(w, layer) for w in stacked] + [_const_spec(w) for w in shared])
    args = [x, conv_pad, ssm_flat, pool_pad] + stacked + shared
    aliases = {}
    if prev_states is not None:
        body = _skip_refs(body, len(args), N_STATE_OUTPUTS)
        aliases = {len(args) + k: 3 + k for k in range(N_STATE_OUTPUTS)}
        in_specs = in_specs + [pl.BlockSpec(memory_space=pl.ANY)] * N_STATE_OUTPUTS
        args = args + list(prev_states)
    ya, yb, yc, *states = pl.pallas_call(
        body, grid=(n // nb,), in_specs=in_specs,
        out_specs=out_specs, out_shape=out_shape, scratch_shapes=scratch,
        input_output_aliases=aliases,
        compiler_params=pltpu.CompilerParams(dimension_semantics=("arbitrary",),
                                             vmem_limit_bytes=VMEM_LIMIT),
        name="branch_sample",
    )(*args)
    return ya, yb, yc, states


def _merge_kernel(x_ref, ya_ref, yb_ref, yc_ref, nm_ref, wz_ref, ssdn_ref, wg_ref, wa_ref, wb_ref, wc_ref, wo_ref,
                  o_ref):
    x = x_ref[...]
    hb = _rmsnorm(x, nm_ref[...]).astype(BF16)
    ya = _group_rmsnorm(ya_ref[...].astype(F32) * _silu(_dot(hb, wz_ref[...])), ssdn_ref[...]).astype(BF16)
    m = _sigmoid(_dot(hb, wg_ref[:, 0:D_MODEL])) * _dot(ya, wa_ref[...])
    m = m + _sigmoid(_dot(hb, wg_ref[:, D_MODEL:2 * D_MODEL])) * _dot(yb_ref[...], wb_ref[...])
    m = m + _sigmoid(_dot(hb, wg_ref[:, 2 * D_MODEL:3 * D_MODEL])) * _dot(yc_ref[...], wc_ref[...])
    o_ref[...] = x + _dot(m.astype(BF16), wo_ref[...])


def _merge(x, ya, yb, yc, sw, layer):
    t = x.shape[0]
    tm = min(MERGE_BLOCK, t)
    assert t % tm == 0
    stacked = [sw[k] for k in ('norm_mix', 'w_z', 'ssd_norm', 'w_gate', 'w_br_a', 'w_br_b', 'w_br_c', 'w_out')]
    tok = lambda width: pl.BlockSpec((tm, width), lambda i: (i, 0))
    return pl.pallas_call(
        _merge_kernel, grid=(t // tm,),
        in_specs=[tok(D_MODEL), tok(SSD_INNER), tok(SGU_WIDTH), tok(POOL_WIDTH)] + [_layer_spec(w, layer) for w in stacked],
        out_specs=tok(D_MODEL), out_shape=jax.ShapeDtypeStruct((t, D_MODEL), F32),
        compiler_params=pltpu.CompilerParams(dimension_semantics=("arbitrary",), vmem_limit_bytes=VMEM_LIMIT),
        name="merge",
    )(x, ya, yb, yc, *stacked)


def _ffn_kernel(x_ref, p_ref, nf_ref, wgu_ref, wd_ref, np_ref, wpg_ref, wpu_ref, fn_ref, o_ref, *, final):
    x = x_ref[...]
    hb = _rmsnorm(x, nf_ref[...]).astype(BF16)
    acc = x
    for cols in (slice(0, FF_SPLIT), slice(FF_SPLIT, D_FF)):
        up_cols = slice(D_FF + cols.start, D_FF + cols.stop)
        act = (_silu(_dot(hb, wgu_ref[:, cols])) * _dot(hb, wgu_ref[:, up_cols])).astype(BF16)
        acc = acc + _dot(act, wd_ref[cols, :])
    hb = _rmsnorm(acc, np_ref[...]).astype(BF16)
    out = acc + _dot(p_ref[...].astype(BF16), wpu_ref[...]) * _sigmoid(_dot(hb, wpg_ref[...]))
    if final:
        out = _rmsnorm(out, fn_ref[...])
    o_ref[...] = out


def _ffn(x, p, sw, layer, final):
    t = x.shape[0]
    tm = min(TOKEN_BLOCK, t)
    assert t % tm == 0
    stacked = [sw[k] for k in ('norm_ffn', 'w_gate_up', 'w_down', 'norm_ple', 'w_ple_gate', 'w_ple_up')]
    tok = lambda width: pl.BlockSpec((tm, width), lambda i: (i, 0))
    return pl.pallas_call(
        functools.partial(_ffn_kernel, final=final), grid=(t // tm,),
        in_specs=([tok(D_MODEL), pl.BlockSpec((None, tm, PLE_DIM), lambda i: (layer, i, 0))]
                  + [_layer_spec(w, layer) for w in stacked] + [_const_spec(sw['final_norm'])]),
        out_specs=tok(D_MODEL), out_shape=jax.ShapeDtypeStruct((t, D_MODEL), F32),
        compiler_params=pltpu.CompilerParams(dimension_semantics=("arbitrary",), vmem_limit_bytes=VMEM_LIMIT),
        name="ffn",
    )(x, p, *stacked, sw['final_norm'])


IN_DIM = O_GATE + 3 * D_MODEL
W_IN_ROWS = 512


def _split_w_in_kernel(wt_ref, z_ref, xbc_ref, dt_ref, uv_ref, pool_ref, gate_ref):
    z_ref[...] = wt_ref[O_Z:O_XBC, :].T.astype(BF16)
    xbc_ref[...] = wt_ref[O_XBC:O_DT, :].T.astype(BF16)
    dt = wt_ref[O_DT:O_DT + LANES, :].T
    dt_ref[...] = jnp.where(_lane_iota(dt.shape) < SSD_HEADS, dt, 0.0).astype(BF16)
    uv_ref[...] = wt_ref[O_UV:O_POOL, :].T.astype(BF16)
    pool_ref[...] = wt_ref[O_POOL:O_GATE, :].T.astype(BF16)
    gate_ref[...] = wt_ref[O_GATE:IN_DIM, :].T.astype(BF16)


def _split_w_in(w_in):
    depth, d, width = w_in.shape
    assert width == IN_DIM and d % W_IN_ROWS == 0
    widths = (O_XBC - O_Z, O_DT - O_XBC, LANES, O_POOL - O_UV, O_GATE - O_POOL, IN_DIM - O_GATE)
    spec = lambda w: pl.BlockSpec((None, W_IN_ROWS, w), lambda i, j: (i, j, 0))
    return pl.pallas_call(
        _split_w_in_kernel, grid=(depth, d // W_IN_ROWS),
        in_specs=[pl.BlockSpec((None, IN_DIM, W_IN_ROWS), lambda i, j: (i, 0, j))],
        out_specs=tuple(spec(w) for w in widths),
        out_shape=tuple(jax.ShapeDtypeStruct((depth, d, w), BF16) for w in widths),
        compiler_params=pltpu.CompilerParams(dimension_semantics=("arbitrary", "arbitrary"),
                                             vmem_limit_bytes=VMEM_LIMIT),
        name="split_w_in",
    )(jnp.swapaxes(w_in, 1, 2))


def _head_expansion():
    rows = jnp.arange(2 * LANES)[:, None] % LANES
    cols = jnp.arange(SSD_INNER)[None, :] // SSD_HEAD_DIM
    return (rows == cols).astype(BF16)


def _prepare_weights(norm_mix, w_in, conv_w, conv_b, dt_bias, a_log, d_skip, ssd_norm, sgu_ln_g, sgu_ln_b,
                     w_spatial, b_spatial, pool_w, pool_scale, w_br_a, w_br_b, w_br_c, w_out, norm_ffn,
                     w_gate_up, w_down, norm_ple, w_ple_gate, w_ple_up, final_norm):
    depth = w_in.shape[0]
    row = lambda v: v.reshape(depth, 1, -1).astype(F32)
    pad_heads = lambda v: jnp.pad(v.reshape(depth, 1, -1).astype(F32), ((0, 0), (0, 0), (0, LANES - SSD_HEADS)))
    seq = SUBLANES
    w8 = w_spatial[:, :, :seq, :seq]
    tt = jnp.arange(seq)
    lag = tt[None, :] - tt[:, None]
    w_lag = jnp.where(lag >= 0, w8[:, :, tt[None, :], jnp.clip(lag, 0, seq - 1)], 0.0)
    w_diag8 = jnp.repeat(jnp.transpose(w_lag, (0, 2, 3, 1)), SGU_GW, axis=3).astype(F32)
    b_full = jnp.repeat(jnp.transpose(b_spatial, (0, 2, 1)), SGU_GW, axis=2).astype(F32)
    w_z, w_xbc, w_dt, w_uv, w_pool, w_gate = _split_w_in(w_in)
    return {
        'norm_mix': row(norm_mix),
        'w_z': w_z, 'w_xbc': w_xbc, 'w_dt': w_dt, 'w_uv': w_uv, 'w_pool': w_pool, 'w_gate': w_gate,
        'conv_w': conv_w.astype(F32), 'conv_b': row(conv_b),
        'dt_bias': pad_heads(dt_bias), 'a_log': pad_heads(a_log),
        'd_skip': row(jnp.repeat(d_skip, SSD_HEAD_DIM, axis=1)),
        'ssd_norm': row(ssd_norm),
        'ln_g': row(sgu_ln_g), 'ln_b': row(sgu_ln_b),
        'w_sp': w_spatial.astype(F32), 'b_sp': b_full,
        'w_diag8': w_diag8, 'b8': b_full[:, :seq],
        'pool_w': pool_w.astype(BF16), 'pool_scale': row(pool_scale),
        'w_br_a': w_br_a.astype(BF16), 'w_br_b': w_br_b.astype(BF16), 'w_br_c': w_br_c.astype(BF16),
        'w_out': w_out.astype(BF16),
        'norm_ffn': row(norm_ffn),
        'w_gate_up': w_gate_up.astype(BF16),
        'w_down': w_down.astype(BF16),
        'norm_ple': row(norm_ple),
        'w_ple_gate': w_ple_gate.astype(BF16), 'w_ple_up': w_ple_up.astype(BF16),
        'final_norm': final_norm.reshape(1, -1).astype(F32),
        'e2': _head_expansion(),
    }


def kernel(x_prompt, x_sample, state_conv, state_ssm, state_pool, p_prompt, p_sample, norm_mix, w_in, conv_w, conv_b, dt_bias, a_log, d_skip, ssd_norm, sgu_ln_g, sgu_ln_b, w_spatial, b_spatial, pool_w, pool_scale, w_br_a, w_br_b, w_br_c, w_out, norm_ffn, w_gate_up, w_down, norm_ple, w_ple_gate, w_ple_up, final_norm):
    b, l, _ = x_prompt.shape
    n, s, _ = x_sample.shape
    depth = w_in.shape[0]
    sw = _prepare_weights(norm_mix, w_in, conv_w, conv_b, dt_bias, a_log, d_skip, ssd_norm, sgu_ln_g, sgu_ln_b,
                          w_spatial, b_spatial, pool_w, pool_scale, w_br_a, w_br_b, w_br_c, w_out, norm_ffn,
                          w_gate_up, w_down, norm_ple, w_ple_gate, w_ple_up, final_norm)
    conv_pad = jnp.pad(state_conv, ((0, 0), (0, 0), (SUBLANES - (SSD_CONV - 1), 0), (0, 0)))
    pool_pad = jnp.pad(state_pool, ((0, 0), (0, 0), (2 * SUBLANES - POOL_BUF, 0), (0, 0)))
    ssm_flat = state_ssm.reshape(depth, n, SSD_INNER, SSD_STATE)
    pp = p_prompt.reshape(depth, b * l, PLE_DIM)
    ps = p_sample.reshape(depth, n * s, PLE_DIM)
    xp = x_prompt
    xs = x_sample.reshape(n * s, D_MODEL)
    st_p = None
    st_s = None
    for i in range(depth):
        final = i == depth - 1
        ya, yb, yc, st_p = _branch_prompt(xp, sw, i, st_p)
        x1 = _merge(xp.reshape(b * l, D_MODEL), ya.reshape(b * l, -1), yb.reshape(b * l, -1), yc.reshape(b * l, -1), sw, i)
        xp = _ffn(x1, pp, sw, i, final).reshape(b, l, D_MODEL)
        ya, yb, yc, st_s = _branch_sample(xs.reshape(n, s, D_MODEL), conv_pad, ssm_flat, pool_pad, sw, i, st_s, PAST_LEN)
        x1 = _merge(xs, ya, yb, yc, sw, i)
        xs = _ffn(x1, ps, sw, i, final)
    conv_p, ssm_p, pool_p, v_p = st_p
    conv_s, ssm_s, pool_s, v_s = st_s
    return (xp, xs.reshape(n, s, D_MODEL),
            conv_p[:, :, SUBLANES - (SSD_CONV - 1):, :],
            ssm_p.reshape(depth, b, SSD_HEADS, SSD_HEAD_DIM, SSD_STATE),
            pool_p[:, :, 2 * SUBLANES - POOL_BUF:, :],
            v_p,
            conv_s[:, :, s - (SSD_CONV - 1):, :],
            ssm_s.reshape(depth, n, SSD_HEADS, SSD_HEAD_DIM, SSD_STATE),
            pool_s[:, :, 2 * s - POOL_BUF:, :],
            v_s.reshape(depth, n, s, SGU_WIDTH))
```

```python
import functools
import math

import jax
import jax.numpy as jnp
from jax import lax
from jax.experimental import pallas as pl
from jax.experimental.pallas import tpu as pltpu

F32 = jnp.float32
BF16 = jnp.bfloat16

D_MODEL = 1024
DEPTH = 4
PAST_LEN = 16384
SSD_HEAD_DIM = 64
SSD_HEADS = 16
SSD_INNER = 1024
SSD_GROUPS = 2
SSD_STATE = 128
SSD_CONV = 4
SSD_CHUNK = 128
SSD_CONV_DIM = 1536
SGU_WIDTH = 512
SGU_GROUPS = 4
SGU_CHUNK = 128
SGU_GW = 128
POOL_WIDTH = 512
POOL_WINDOWS = (2, 4, 8, 16)
POOL_GW = 128
POOL_BUF = 15
D_FF = 2816
PLE_DIM = 256
EPS = 1e-6
LOG2_E = 1.4426950408889634
O_Z = 0
O_XBC = 1024
O_DT = 2560
O_UV = 2576
O_POOL = 3600
O_GATE = 4112

LANES = 128
SUBLANES = 8
HEADS_PER_GROUP = SSD_HEADS // SSD_GROUPS
GROUP_INNER = SSD_INNER // SSD_GROUPS
VMEM_LIMIT = 56 * 1024 * 1024

PROMPT_BLOCK = 256
SAMPLE_SEQS = 16
SAMPLE_UNROLL = 8
TOKEN_BLOCK = 512
MERGE_BLOCK = 1024
MXU_DEPTH = 256
FF_SPLIT = (D_FF // MXU_DEPTH + 1) // 2 * MXU_DEPTH


def _dot(a, b):
    return jnp.dot(a, b, preferred_element_type=F32)


def _dot_nt(a, b):
    return lax.dot_general(a, b, (((1,), (1,)), ((), ())), preferred_element_type=F32)


def _dot_tn(a, b):
    return lax.dot_general(a, b, (((0,), (0,)), ((), ())), preferred_element_type=F32)


def _sigmoid(x):
    return 0.5 * jnp.tanh(0.5 * x) + 0.5


def _silu(x):
    h = 0.5 * x
    return h * jnp.tanh(h) + h


def _gelu_tanh(x):
    c = math.sqrt(2.0 / math.pi)
    return 0.5 * x * (1.0 + jnp.tanh(c * (x + 0.044715 * (x * x * x))))


def _softplus(x):
    return jnp.maximum(x, 0.0) + jnp.log1p(jnp.exp(-jnp.abs(x)))


def _rmsnorm(x, g):
    ms = jnp.mean(x * x, axis=-1, keepdims=True)
    return x * lax.rsqrt(ms + EPS) * g


def _split2(v):
    hi = v.astype(BF16)
    lo = (v - hi.astype(F32)).astype(BF16)
    return jnp.concatenate([hi, lo], axis=-1)


def _expand_heads(v, e2):
    return _dot(_split2(v), e2)


def _group_rmsnorm(y, g):
    parts = []
    for k in range(SSD_GROUPS):
        yk = y[:, k * GROUP_INNER:(k + 1) * GROUP_INNER]
        ms = jnp.mean(yk * yk, axis=-1, keepdims=True)
        parts.append(yk * lax.rsqrt(ms + EPS))
    return jnp.concatenate(parts, axis=-1) * g


def _row_iota(shape):
    return lax.broadcasted_iota(jnp.int32, shape, 0)


def _lane_iota(shape):
    return lax.broadcasted_iota(jnp.int32, shape, 1)


def _ssd_chunk(xs_bf, bm, cm, dtc, a_row, ht_ref):
    q = SSD_CHUNK
    tri = _row_iota((q, q)) >= _lane_iota((q, q))
    tril_bf = jnp.where(tri, 1.0, 0.0).astype(BF16)
    da = dtc * a_row
    hi = da.astype(BF16)
    r1 = da - hi.astype(F32)
    mid = r1.astype(BF16)
    lo = (r1 - mid.astype(F32)).astype(BF16)
    cs3 = _dot(tril_bf, jnp.concatenate([hi, mid, lo], axis=1))
    a_cs = (cs3[:, :LANES] + cs3[:, LANES:2 * LANES] + cs3[:, 2 * LANES:]) * LOG2_E
    a_t = a_cs.T
    ap_t = (a_cs - jnp.log2(dtc)).T
    w_t = jnp.exp2(a_t[:, q - 1:q] - ap_t)
    c_bf = cm.astype(BF16)
    lane = _lane_iota((q, LANES))
    ys = []
    for g in range(SSD_GROUPS):
        scol = slice(g * SSD_STATE, (g + 1) * SSD_STATE)
        cg = c_bf[:, scol]
        cb = _dot_nt(cg, bm[:, scol].astype(BF16))
        bg_t = bm[:, scol].T
        ht_g = ht_ref[:, g * GROUP_INNER:(g + 1) * GROUP_INNER]
        yoff = _dot(cg, ht_g.astype(BF16))
        for pr in range(HEADS_PER_GROUP // 2):
            h0 = g * HEADS_PER_GROUP + 2 * pr
            acols, ms, bws = [], [], []
            for h in (h0, h0 + 1):
                acol = jnp.broadcast_to(a_cs[:, h:h + 1], (q, q))
                acols.append(acol)
                ms.append((jnp.where(tri, jnp.exp2(acol - ap_t[h:h + 1, :]), 0.0) * cb).astype(BF16))
                bws.append((bg_t * w_t[h:h + 1, :]).astype(BF16))
            lhs = jnp.concatenate([jnp.concatenate(ms, axis=1), jnp.concatenate(bws, axis=1)], axis=0)
            xpair = xs_bf[:, h0 * SSD_HEAD_DIM:(h0 + 2) * SSD_HEAD_DIM]
            zero = jnp.zeros_like(xpair)
            rhs = jnp.concatenate([jnp.where(lane < SSD_HEAD_DIM, xpair, zero),
                                   jnp.where(lane >= SSD_HEAD_DIM, xpair, zero)], axis=0)
            out = _dot(lhs, rhs)
            ea = jnp.exp2(jnp.where(lane < SSD_HEAD_DIM, acols[0], acols[1]))
            ys.append(out[:q] + yoff[:, pr * LANES:(pr + 1) * LANES] * ea)
            cols = slice(h0 * SSD_HEAD_DIM, (h0 + 2) * SSD_HEAD_DIM)
            ht_ref[:, cols] = ht_g[:, pr * LANES:(pr + 1) * LANES] * ea[q - 1:q, :] + out[q:]
    return jnp.concatenate(ys, axis=1)


P_XBC = 0
P_DT = P_XBC + SSD_CONV_DIM
P_UV = P_DT + LANES
P_POOL = P_UV + 2 * SGU_WIDTH
P_WIDTH = P_POOL + POOL_WIDTH
PROJ_PIECE = 512


def _branch_prompt_kernel(xa_ref, xn_ref, nm_ref, wxbc_ref, wdt_ref, wuv_ref, wpool_ref,
                          convw_ref, convb_ref, dtb_ref, alog_ref, dskip_ref,
                          lng_ref, lnb_ref, wsp_ref, bsp_ref, poolw_ref, pools_ref,
                          ya_ref, yb_ref, yc_ref, convo_ref, ssmo_ref, poolo_ref, vo_ref,
                          pa_ref, pb_ref, ht_ref, cc_ref, pc_ref, xcv_ref, *, tb, npair):
    i = pl.program_id(0)
    j = pl.program_id(1)
    nchunk = tb // SSD_CHUNK

    def project_pieces(x, p_ref):
        hb = _rmsnorm(x, nm_ref[...]).astype(BF16)
        pieces = []
        for w_ref, base, width in ((wxbc_ref, P_XBC, SSD_CONV_DIM), (wdt_ref, P_DT, LANES),
                                   (wpool_ref, P_POOL, POOL_WIDTH), (wuv_ref, P_UV, 2 * SGU_WIDTH)):
            for c0 in range(0, width, PROJ_PIECE):
                c1 = min(c0 + PROJ_PIECE, width)

                def piece(w_ref=w_ref, base=base, c0=c0, c1=c1):
                    p_ref[:, base + c0:base + c1] = _dot(hb, w_ref[:, c0:c1])
                pieces.append(piece)
        return pieces

    def process_phases(p_ref, half):
        orow = slice(half * tb, (half + 1) * tb)
        res = {}
        phases = []

        def conv(cols):
            xp = jnp.concatenate([cc_ref[:, cols], p_ref[:, P_XBC + cols.start:P_XBC + cols.stop]], axis=0)
            acc = xp[SUBLANES:, :] * convw_ref[SSD_CONV - 1:SSD_CONV, cols]
            for k in range(SSD_CONV - 1):
                acc = acc + pltpu.roll(xp, SSD_CONV - 1 - k, 0)[SUBLANES:, :] * convw_ref[k:k + 1, cols]
            xcv_ref[:, cols] = _silu(acc + convb_ref[:, cols])
        for c0 in range(0, SSD_CONV_DIM, PROJ_PIECE):
            phases.append(functools.partial(conv, slice(c0, c0 + PROJ_PIECE)))

        def scan(c):
            rows = slice(c * SSD_CHUNK, (c + 1) * SSD_CHUNK)
            xs = xcv_ref[rows, 0:SSD_INNER]
            y = _ssd_chunk(xs.astype(BF16),
                           xcv_ref[rows, SSD_INNER:SSD_INNER + SSD_GROUPS * SSD_STATE],
                           xcv_ref[rows, SSD_INNER + SSD_GROUPS * SSD_STATE:SSD_CONV_DIM],
                           _softplus(p_ref[rows, P_DT:P_UV] + dtb_ref[...]),
                           -jnp.exp(alog_ref[...]), ht_ref)
            ya_ref[half * tb + c * SSD_CHUNK:half * tb + (c + 1) * SSD_CHUNK, :] = (
                y + dskip_ref[...] * xs).astype(BF16)
            if c == nchunk - 1:
                res['tail'] = p_ref[tb - SUBLANES:tb, P_XBC:P_DT]
                cc_ref[...] = res['tail']
        for c in range(nchunk):
            phases.append(functools.partial(scan, c))

        def spatial():
            u = _gelu_tanh(p_ref[:, P_UV:P_UV + SGU_WIDTH])
            v = _gelu_tanh(p_ref[:, P_UV + SGU_WIDTH:P_POOL])
            mu = jnp.mean(v, axis=-1, keepdims=True)
            vc = v - mu
            var = jnp.mean(vc * vc, axis=-1, keepdims=True)
            vn = vc * lax.rsqrt(var + EPS) * lng_ref[...] + lnb_ref[...]
            res['vn'] = vn
            vn_bf = vn.astype(BF16)
            tri = _row_iota((SGU_CHUNK, SGU_CHUNK)) >= _lane_iota((SGU_CHUNK, SGU_CHUNK))
            s_groups = []
            for g in range(SGU_GROUPS):
                wm = jnp.where(tri, wsp_ref[g], 0.0).astype(BF16)
                vg = jnp.concatenate([vn_bf[c * SGU_CHUNK:(c + 1) * SGU_CHUNK, g * SGU_GW:(g + 1) * SGU_GW]
                                      for c in range(nchunk)], axis=1)
                s_groups.append(_dot(wm, vg))
            s = jnp.concatenate([jnp.concatenate([sg[:, c * SGU_GW:(c + 1) * SGU_GW] for sg in s_groups], axis=1)
                                 + bsp_ref[...] for c in range(nchunk)], axis=0)
            yb_ref[orow, :] = (u * s).astype(BF16)
        phases.append(spatial)

        def pool():
            pos = (2 * j + half) * tb + _row_iota((tb, POOL_GW))
            yc_cols = []
            for g, w in enumerate(POOL_WINDOWS):
                cols = slice(P_POOL + g * POOL_GW, P_POOL + (g + 1) * POOL_GW)
                cur = p_ref[:, cols]
                wsum = jnp.concatenate([pc_ref[:, g * POOL_GW:(g + 1) * POOL_GW], cur], axis=0)
                span = 1
                while span < w:
                    wsum = wsum + pltpu.roll(wsum, span, 0)
                    span *= 2
                cnt = jnp.minimum(pos + 1, w).astype(F32)
                d = wsum[2 * SUBLANES:, :] / cnt - cur
                yc_cols.append(_dot(d.astype(BF16), poolw_ref[g]))
            yc_ref[orow, :] = (jnp.concatenate(yc_cols, axis=1) * pools_ref[...]).astype(BF16)
            res['ptail'] = p_ref[tb - 2 * SUBLANES:tb, P_POOL:P_WIDTH]
            pc_ref[...] = res['ptail']
        phases.append(pool)
        return phases, res

    def run(p_cur, half, x_next, p_next):
        phases, res = process_phases(p_cur, half)
        pieces = project_pieces(x_next, p_next)
        done = 0
        for k, phase in enumerate(phases):
            phase()
            upto = (k + 1) * len(pieces) // len(phases)
            for piece in pieces[done:upto]:
                piece()
            done = upto
        return res

    @pl.when(jnp.logical_and(i == 0, j == 0))
    def _():
        for piece in project_pieces(xa_ref[0:tb, :], pa_ref):
            piece()

    @pl.when(j == 0)
    def _():
        ht_ref[...] = jnp.zeros_like(ht_ref)
        cc_ref[...] = jnp.zeros_like(cc_ref)
        pc_ref[...] = jnp.zeros_like(pc_ref)

    run(pa_ref, 0, xa_ref[tb:2 * tb, :], pb_ref)
    res = run(pb_ref, 1, xn_ref[...], pa_ref)

    @pl.when(j == npair - 1)
    def _():
        convo_ref[...] = res['tail']
        ssmo_ref[...] = ht_ref[...].T
        poolo_ref[...] = res['ptail']
        vo_ref[...] = res['vn'][tb - SGU_CHUNK:tb, :]


def _const_spec(arr):
    nd = arr.ndim
    return pl.BlockSpec(arr.shape, lambda *_: (0,) * nd, pipeline_mode=pl.Buffered(1))


def _layer_spec(arr, layer):
    nd = arr.ndim - 1
    return pl.BlockSpec((None,) + arr.shape[1:], lambda *_: (layer,) + (0,) * nd, pipeline_mode=pl.Buffered(1))


def _skip_refs(body, first, count):
    def wrapped(*refs):
        return body(*refs[:first], *refs[first + count:])
    return wrapped


BRANCH_WEIGHTS = ('norm_mix', 'w_xbc', 'w_dt', 'w_uv', 'w_pool', 'conv_w', 'conv_b', 'dt_bias', 'a_log',
                  'd_skip', 'ln_g', 'ln_b')
N_STATE_OUTPUTS = 4


def _branch_prompt(x, sw, layer, prev_states):
    b, l, _ = x.shape
    depth = sw['w_xbc'].shape[0]
    tb = PROMPT_BLOCK
    npair = l // (2 * tb)
    assert l % (2 * tb) == 0 and tb % SSD_CHUNK == 0 and tb >= 2 * SUBLANES
    stacked = [sw[k] for k in BRANCH_WEIGHTS + ('w_sp', 'b_sp', 'pool_w', 'pool_scale')]
    tok = lambda width: pl.BlockSpec((None, 2 * tb, width), lambda i, j: (i, j, 0))

    def next_block(i, j):
        flat = jnp.minimum(i * npair + j + 1, b * npair - 1)
        return flat // npair, (flat % npair) * 2, 0

    per_seq = lambda rows, width: pl.BlockSpec((None, None, rows, width), lambda i, j: (layer, i, 0, 0))
    out_shape = (jax.ShapeDtypeStruct((b, l, SSD_INNER), BF16),
                 jax.ShapeDtypeStruct((b, l, SGU_WIDTH), BF16),
                 jax.ShapeDtypeStruct((b, l, POOL_WIDTH), BF16),
                 jax.ShapeDtypeStruct((depth, b, SUBLANES, SSD_CONV_DIM), F32),
                 jax.ShapeDtypeStruct((depth, b, SSD_INNER, SSD_STATE), F32),
                 jax.ShapeDtypeStruct((depth, b, 2 * SUBLANES, POOL_WIDTH), F32),
                 jax.ShapeDtypeStruct((depth, b, SGU_CHUNK, SGU_WIDTH), F32))
    out_specs = (tok(SSD_INNER), tok(SGU_WIDTH), tok(POOL_WIDTH),
                 per_seq(SUBLANES, SSD_CONV_DIM), per_seq(SSD_INNER, SSD_STATE),
                 per_seq(2 * SUBLANES, POOL_WIDTH), per_seq(SGU_CHUNK, SGU_WIDTH))
    scratch = [pltpu.VMEM((tb, P_WIDTH), F32),
               pltpu.VMEM((tb, P_WIDTH), F32),
               pltpu.VMEM((SSD_STATE, SSD_INNER), F32),
               pltpu.VMEM((SUBLANES, SSD_CONV_DIM), F32),
               pltpu.VMEM((2 * SUBLANES, POOL_WIDTH), F32),
               pltpu.VMEM((tb, SSD_CONV_DIM), F32)]
    body = functools.partial(_branch_prompt_kernel, tb=tb, npair=npair)
    in_specs = [tok(D_MODEL), pl.BlockSpec((None, tb, D_MODEL), next_block)] + [_layer_spec(w, layer) for w in stacked]
    args = [x, x] + stacked
    aliases = {}
    if prev_states is not None:
        body = _skip_refs(body, len(args), N_STATE_OUTPUTS)
        aliases = {len(args) + k: 3 + k for k in range(N_STATE_OUTPUTS)}
        in_specs = in_specs + [pl.BlockSpec(memory_space=pl.ANY)] * N_STATE_OUTPUTS
        args = args + list(prev_states)
    ya, yb, yc, *states = pl.pallas_call(
        body, grid=(b, npair), in_specs=in_specs,
        out_specs=out_specs, out_shape=out_shape, scratch_shapes=scratch,
        input_output_aliases=aliases,
        compiler_params=pltpu.CompilerParams(dimension_semantics=("arbitrary", "arbitrary"),
                                             vmem_limit_bytes=VMEM_LIMIT),
        name="branch_prompt",
    )(*args)
    return ya, yb, yc, states


def _branch_sample_kernel(x_ref, convs_ref, ssms_ref, pools_in_ref, nm_ref, wxbc_ref, wdt_ref, wuv_ref,
                          wpool_ref, convw_ref, convb_ref, dtb_ref, alog_ref, dskip_ref,
                          lng_ref, lnb_ref, wd_ref, b8_ref, poolw_ref, pools_ref, e2_ref,
                          ya_ref, yb_ref, yc_ref, convo_ref, ssmo_ref, poolo_ref, vo_ref,
                          c_ref, bm_ref, xd_ref, daug_ref, yoff_ref, *, nb, seq, start):
    r = nb * seq
    assert seq == SUBLANES
    t128 = _row_iota((r, LANES)) % seq

    def tile_roll(v, j):
        width = v.shape[-1]
        return pltpu.roll(v.reshape(nb, seq, width), j, 1).reshape(r, width)

    def shift_rows(cur, prev, j, width):
        tt = _row_iota((r, width)) % seq
        if prev is None:
            return jnp.where(tt >= j, tile_roll(cur, j), 0.0)
        return jnp.where(tt >= j, tile_roll(cur, j), tile_roll(prev, j))

    hb = _rmsnorm(x_ref[...].reshape(r, D_MODEL), nm_ref[...]).astype(BF16)

    xbc = _dot(hb, wxbc_ref[...])
    cbuf = convs_ref[...].reshape(r, SSD_CONV_DIM)
    acc = xbc * convw_ref[SSD_CONV - 1:SSD_CONV, :]
    for k in range(SSD_CONV - 1):
        acc = acc + shift_rows(xbc, cbuf, SSD_CONV - 1 - k, SSD_CONV_DIM) * convw_ref[k:k + 1, :]
    xcv = _silu(acc + convb_ref[...])
    convo_ref[...] = xbc.reshape(nb, seq, SSD_CONV_DIM)
    xs = xcv[:, 0:SSD_INNER]
    bm = xcv[:, SSD_INNER:SSD_INNER + SSD_GROUPS * SSD_STATE]
    cm = xcv[:, SSD_INNER + SSD_GROUPS * SSD_STATE:SSD_CONV_DIM]
    dt = _softplus(_dot(hb, wdt_ref[...]) + dtb_ref[...])
    a_row = -jnp.exp(alog_ref[...])
    a_cs = dt * a_row
    for s in (1, 2, 4):
        a_cs = a_cs + jnp.where(t128 >= s, tile_roll(a_cs, s), 0.0)
    a3 = a_cs.reshape(nb, seq, LANES)
    tot = jnp.broadcast_to(a3[:, seq - 1:seq, :], (nb, seq, LANES)).reshape(r, LANES)
    e2 = e2_ref[...]
    exp_a_e = _expand_heads(jnp.exp(a_cs), e2)
    decst_e = _expand_heads(jnp.exp(tot - a_cs), e2)
    dt_e = _expand_heads(dt, e2)
    dectot_e = _expand_heads(jnp.exp(tot), e2)
    xdt = xs * dt_e
    lane = _lane_iota((r, LANES))
    y = jnp.zeros((r, SSD_INNER), F32)
    for j in range(seq):
        if j == 0:
            lj = jnp.ones((r, LANES), F32)
            b_sh, x_sh = bm, xdt
        else:
            lj = jnp.where(t128 >= j, jnp.exp(a_cs - tile_roll(a_cs, j)), 0.0)
            b_sh, x_sh = tile_roll(bm, j), tile_roll(xdt, j)
        prod = cm * b_sh
        cb0 = jnp.sum(prod[:, 0:SSD_STATE], axis=-1, keepdims=True)
        cb1 = jnp.sum(prod[:, SSD_STATE:2 * SSD_STATE], axis=-1, keepdims=True)
        mj = lj * jnp.where(lane < HEADS_PER_GROUP, cb0, cb1)
        y = y + _expand_heads(mj, e2) * x_sh
    c_ref[...] = cm
    bm_ref[...] = bm
    xd_ref[...] = xdt * decst_e
    dec_hi = dectot_e.astype(BF16).astype(F32)
    t1024 = _row_iota((r, SSD_INNER)) % seq
    daug_ref[...] = jnp.where(t1024 == 0, dec_hi, jnp.where(t1024 == 1, dectot_e - dec_hi, 0.0))
    t8 = _row_iota((seq, LANES))
    ones2 = jnp.where(t8 < 2, 1.0, 0.0)
    zeros8 = jnp.zeros((seq, LANES), F32)

    def per_seq(n, carry):
        rows = pl.ds(pl.multiple_of(n * seq, seq), seq)
        for g in range(SSD_GROUPS):
            cols = slice(g * GROUP_INNER, (g + 1) * GROUP_INNER)
            scol = slice(g * SSD_STATE, (g + 1) * SSD_STATE)
            h0 = ssms_ref[n, cols, :]
            cg = c_ref[rows, scol].astype(BF16)
            yoff_ref[rows, cols] = _dot_nt(cg, h0.astype(BF16))
            lhs = jnp.concatenate([xd_ref[rows, cols], daug_ref[rows, cols]], axis=0).astype(BF16)
            rhs = jnp.concatenate([jnp.concatenate([bm_ref[rows, scol], zeros8], axis=1),
                                   jnp.concatenate([zeros8, ones2], axis=1)], axis=0).astype(BF16)
            upd = _dot_tn(lhs, rhs)
            ssmo_ref[n, cols, :] = upd[:, SSD_STATE:] * h0 + upd[:, :SSD_STATE]
        return carry

    lax.fori_loop(0, nb, per_seq, 0, unroll=SAMPLE_UNROLL)
    ya_ref[...] = (y + yoff_ref[...] * exp_a_e + dskip_ref[...] * xs).astype(BF16)

    a = _gelu_tanh(_dot(hb, wuv_ref[...]))
    u = a[:, :SGU_WIDTH]
    v = a[:, SGU_WIDTH:]
    mu = jnp.mean(v, axis=-1, keepdims=True)
    vc = v - mu
    var = jnp.mean(vc * vc, axis=-1, keepdims=True)
    vn = vc * lax.rsqrt(var + EPS) * lng_ref[...] + lnb_ref[...]
    vo_ref[...] = vn
    s = vn.reshape(nb, seq, SGU_WIDTH) * wd_ref[0] + b8_ref[...]
    for j in range(1, seq):
        s = s + tile_roll(vn, j).reshape(nb, seq, SGU_WIDTH) * wd_ref[j]
    yb_ref[...] = (u * s.reshape(r, SGU_WIDTH)).astype(BF16)

    xc = _dot(hb, wpool_ref[...])
    pbuf = pools_in_ref[...]
    t0 = pbuf[:, 0:seq, :].reshape(r, POOL_WIDTH)
    t1 = pbuf[:, seq:2 * seq, :].reshape(r, POOL_WIDTH)
    poolo_ref[:, 0:seq, :] = pbuf[:, seq:2 * seq, :]
    poolo_ref[:, seq:2 * seq, :] = xc.reshape(nb, seq, POOL_WIDTH)
    pos = start + (_row_iota((r, POOL_GW)) % seq)
    yc_cols = []
    for g, w in enumerate(POOL_WINDOWS):
        cols = slice(g * POOL_GW, (g + 1) * POOL_GW)
        tiles = [t0[:, cols], t1[:, cols], xc[:, cols]]
        span = 1
        while span < min(w, seq):
            prev = [None] + tiles[:-1]
            tiles = [tl + shift_rows(tl, pv, span, POOL_GW) for tl, pv in zip(tiles, prev)]
            span *= 2
        wsum = tiles[2] if w <= seq else tiles[2] + tiles[1]
        cnt = jnp.minimum(pos + 1, w).astype(F32)
        d = wsum / cnt - xc[:, cols]
        yc_cols.append(_dot(d.astype(BF16), poolw_ref[g]))
    yc_ref[...] = (jnp.concatenate(yc_cols, axis=1) * pools_ref[...]).astype(BF16)


def _branch_sample(x, conv_pad, ssm_flat, pool_pad, sw, layer, prev_states, start):
    n, seq, _ = x.shape
    depth = sw['w_xbc'].shape[0]
    nb = SAMPLE_SEQS
    assert n % nb == 0 and seq == SUBLANES and start >= max(POOL_WINDOWS)
    r = nb * seq
    stacked = [sw[k] for k in BRANCH_WEIGHTS + ('w_diag8', 'b8', 'pool_w', 'pool_scale')]
    shared = [sw['e2']]
    seq3 = lambda rows, width: pl.BlockSpec((nb, rows, width), lambda i: (i, 0, 0))
    seq4 = lambda rows, width: pl.BlockSpec((None, nb, rows, width), lambda i: (layer, i, 0, 0))
    tok = lambda width: pl.BlockSpec((r, width), lambda i: (i, 0))
    out_shape = (jax.ShapeDtypeStruct((n * seq, SSD_INNER), BF16),
                 jax.ShapeDtypeStruct((n * seq, SGU_WIDTH), BF16),
                 jax.ShapeDtypeStruct((n * seq, POOL_WIDTH), BF16),
                 jax.ShapeDtypeStruct((depth, n, seq, SSD_CONV_DIM), F32),
                 jax.ShapeDtypeStruct((depth, n, SSD_INNER, SSD_STATE), F32),
                 jax.ShapeDtypeStruct((depth, n, 2 * seq, POOL_WIDTH), F32),
                 jax.ShapeDtypeStruct((depth, n * seq, SGU_WIDTH), F32))
    out_specs = (tok(SSD_INNER), tok(SGU_WIDTH), tok(POOL_WIDTH), seq4(seq, SSD_CONV_DIM),
                 seq4(SSD_INNER, SSD_STATE), seq4(2 * seq, POOL_WIDTH),
                 pl.BlockSpec((None, r, SGU_WIDTH), lambda i: (layer, i, 0)))
    scratch = [pltpu.VMEM((r, SSD_GROUPS * SSD_STATE), F32),
               pltpu.VMEM((r, SSD_GROUPS * SSD_STATE), F32),
               pltpu.VMEM((r, SSD_INNER), F32),
               pltpu.VMEM((r, SSD_INNER), F32),
               pltpu.VMEM((r, SSD_INNER), F32)]
    body = functools.partial(_branch_sample_kernel, nb=nb, seq=seq, start=start)
    in_specs = ([seq3(seq, D_MODEL), seq4(seq, SSD_CONV_DIM), seq4(SSD_INNER, SSD_STATE), seq4(2 * seq, POOL_WIDTH)]
                + [_layer_spec(w, layer) for w in stacked] + [_const_spec(w) for w in shared])
    args = [x, conv_pad, ssm_flat, pool_pad] + stacked + shared
    aliases = {}
    if prev_states is not None:
        body = _skip_refs(body, len(args), N_STATE_OUTPUTS)
        aliases = {len(args) + k: 3 + k for k in range(N_STATE_OUTPUTS)}
        in_specs = in_specs + [pl.BlockSpec(memory_space=pl.ANY)] * N_STATE_OUTPUTS
        args = args + list(prev_states)
    ya, yb, yc, *states = pl.pallas_call(
        body, grid=(n // nb,), in_specs=in_specs,
        out_specs=out_specs, out_shape=out_shape, scratch_shapes=scratch,
        input_output_aliases=aliases,
        compiler_params=pltpu.CompilerParams(dimension_semantics=("parallel",),
                                             vmem_limit_bytes=VMEM_LIMIT),
        name="branch_sample",
    )(*args)
    return ya, yb, yc, states


def _merge_kernel(x_ref, ya_ref, yb_ref, yc_ref, nm_ref, wz_ref, ssdn_ref, wg_ref, wa_ref, wb_ref, wc_ref, wo_ref,
                  o_ref):
    x = x_ref[...]
    hb = _rmsnorm(x, nm_ref[...]).astype(BF16)
    ya = _group_rmsnorm(ya_ref[...].astype(F32) * _silu(_dot(hb, wz_ref[...])), ssdn_ref[...]).astype(BF16)
    m = _sigmoid(_dot(hb, wg_ref[:, 0:D_MODEL])) * _dot(ya, wa_ref[...])
    m = m + _sigmoid(_dot(hb, wg_ref[:, D_MODEL:2 * D_MODEL])) * _dot(yb_ref[...], wb_ref[...])
    m = m + _sigmoid(_dot(hb, wg_ref[:, 2 * D_MODEL:3 * D_MODEL])) * _dot(yc_ref[...], wc_ref[...])
    o_ref[...] = x + _dot(m.astype(BF16), wo_ref[...])


def _merge(x, ya, yb, yc, sw, layer):
    t = x.shape[0]
    tm = min(MERGE_BLOCK, t)
    assert t % tm == 0
    stacked = [sw[k] for k in ('norm_mix', 'w_z', 'ssd_norm', 'w_gate', 'w_br_a', 'w_br_b', 'w_br_c', 'w_out')]
    tok = lambda width: pl.BlockSpec((tm, width), lambda i: (i, 0))
    return pl.pallas_call(
        _merge_kernel, grid=(t // tm,),
        in_specs=[tok(D_MODEL), tok(SSD_INNER), tok(SGU_WIDTH), tok(POOL_WIDTH)] + [_layer_spec(w, layer) for w in stacked],
        out_specs=tok(D_MODEL), out_shape=jax.ShapeDtypeStruct((t, D_MODEL), F32),
        compiler_params=pltpu.CompilerParams(dimension_semantics=("parallel",), vmem_limit_bytes=VMEM_LIMIT),
        name="merge",
    )(x, ya, yb, yc, *stacked)


def _ffn_kernel(x_ref, p_ref, nf_ref, wgu_ref, wd_ref, np_ref, wpg_ref, wpu_ref, fn_ref, o_ref, *, final):
    x = x_ref[...]
    hb = _rmsnorm(x, nf_ref[...]).astype(BF16)
    acc = x
    for cols in (slice(0, FF_SPLIT), slice(FF_SPLIT, D_FF)):
        up_cols = slice(D_FF + cols.start, D_FF + cols.stop)
        act = (_silu(_dot(hb, wgu_ref[:, cols])) * _dot(hb, wgu_ref[:, up_cols])).astype(BF16)
        acc = acc + _dot(act, wd_ref[cols, :])
    hb = _rmsnorm(acc, np_ref[...]).astype(BF16)
    out = acc + _dot(p_ref[...].astype(BF16), wpu_ref[...]) * _sigmoid(_dot(hb, wpg_ref[...]))
    if final:
        out = _rmsnorm(out, fn_ref[...])
    o_ref[...] = out


def _ffn(x, p, sw, layer, final):
    t = x.shape[0]
    tm = min(TOKEN_BLOCK, t)
    assert t % tm == 0
    stacked = [sw[k] for k in ('norm_ffn', 'w_gate_up', 'w_down', 'norm_ple', 'w_ple_gate', 'w_ple_up')]
    tok = lambda width: pl.BlockSpec((tm, width), lambda i: (i, 0))
    return pl.pallas_call(
        functools.partial(_ffn_kernel, final=final), grid=(t // tm,),
        in_specs=([tok(D_MODEL), pl.BlockSpec((None, tm, PLE_DIM), lambda i: (layer, i, 0))]
                  + [_layer_spec(w, layer) for w in stacked] + [_const_spec(sw['final_norm'])]),
        out_specs=tok(D_MODEL), out_shape=jax.ShapeDtypeStruct((t, D_MODEL), F32),
        compiler_params=pltpu.CompilerParams(dimension_semantics=("parallel",), vmem_limit_bytes=VMEM_LIMIT),
        name="ffn",
    )(x, p, *stacked, sw['final_norm'])


IN_DIM = O_GATE + 3 * D_MODEL
W_IN_ROWS = 512


def _split_w_in_kernel(wt_ref, z_ref, xbc_ref, dt_ref, uv_ref, pool_ref, gate_ref):
    z_ref[...] = wt_ref[O_Z:O_XBC, :].T.astype(BF16)
    xbc_ref[...] = wt_ref[O_XBC:O_DT, :].T.astype(BF16)
    dt = wt_ref[O_DT:O_DT + LANES, :].T
    dt_ref[...] = jnp.where(_lane_iota(dt.shape) < SSD_HEADS, dt, 0.0).astype(BF16)
    uv_ref[...] = wt_ref[O_UV:O_POOL, :].T.astype(BF16)
    pool_ref[...] = wt_ref[O_POOL:O_GATE, :].T.astype(BF16)
    gate_ref[...] = wt_ref[O_GATE:IN_DIM, :].T.astype(BF16)


def _split_w_in(w_in):
    depth, d, width = w_in.shape
    assert width == IN_DIM and d % W_IN_ROWS == 0
    widths = (O_XBC - O_Z, O_DT - O_XBC, LANES, O_POOL - O_UV, O_GATE - O_POOL, IN_DIM - O_GATE)
    spec = lambda w: pl.BlockSpec((None, W_IN_ROWS, w), lambda i, j: (i, j, 0))
    return pl.pallas_call(
        _split_w_in_kernel, grid=(depth, d // W_IN_ROWS),
        in_specs=[pl.BlockSpec((None, IN_DIM, W_IN_ROWS), lambda i, j: (i, 0, j))],
        out_specs=tuple(spec(w) for w in widths),
        out_shape=tuple(jax.ShapeDtypeStruct((depth, d, w), BF16) for w in widths),
        compiler_params=pltpu.CompilerParams(dimension_semantics=("arbitrary", "arbitrary"),
                                             vmem_limit_bytes=VMEM_LIMIT),
        name="split_w_in",
    )(jnp.swapaxes(w_in, 1, 2))


def _head_expansion():
    rows = jnp.arange(2 * LANES)[:, None] % LANES
    cols = jnp.arange(SSD_INNER)[None, :] // SSD_HEAD_DIM
    return (rows == cols).astype(BF16)


def _prepare_weights(norm_mix, w_in, conv_w, conv_b, dt_bias, a_log, d_skip, ssd_norm, sgu_ln_g, sgu_ln_b,
                     w_spatial, b_spatial, pool_w, pool_scale, w_br_a, w_br_b, w_br_c, w_out, norm_ffn,
                     w_gate_up, w_down, norm_ple, w_ple_gate, w_ple_up, final_norm):
    depth = w_in.shape[0]
    row = lambda v: v.reshape(depth, 1, -1).astype(F32)
    pad_heads = lambda v: jnp.pad(v.reshape(depth, 1, -1).astype(F32), ((0, 0), (0, 0), (0, LANES - SSD_HEADS)))
    seq = SUBLANES
    w8 = w_spatial[:, :, :seq, :seq]
    tt = jnp.arange(seq)
    lag = tt[None, :] - tt[:, None]
    w_lag = jnp.where(lag >= 0, w8[:, :, tt[None, :], jnp.clip(lag, 0, seq - 1)], 0.0)
    w_diag8 = jnp.repeat(jnp.transpose(w_lag, (0, 2, 3, 1)), SGU_GW, axis=3).astype(F32)
    b_full = jnp.repeat(jnp.transpose(b_spatial, (0, 2, 1)), SGU_GW, axis=2).astype(F32)
    w_z, w_xbc, w_dt, w_uv, w_pool, w_gate = _split_w_in(w_in)
    return {
        'norm_mix': row(norm_mix),
        'w_z': w_z, 'w_xbc': w_xbc, 'w_dt': w_dt, 'w_uv': w_uv, 'w_pool': w_pool, 'w_gate': w_gate,
        'conv_w': conv_w.astype(F32), 'conv_b': row(conv_b),
        'dt_bias': pad_heads(dt_bias), 'a_log': pad_heads(a_log),
        'd_skip': row(jnp.repeat(d_skip, SSD_HEAD_DIM, axis=1)),
        'ssd_norm': row(ssd_norm),
        'ln_g': row(sgu_ln_g), 'ln_b': row(sgu_ln_b),
        'w_sp': w_spatial.astype(F32), 'b_sp': b_full,
        'w_diag8': w_diag8, 'b8': b_full[:, :seq],
        'pool_w': pool_w.astype(BF16), 'pool_scale': row(pool_scale),
        'w_br_a': w_br_a.astype(BF16), 'w_br_b': w_br_b.astype(BF16), 'w_br_c': w_br_c.astype(BF16),
        'w_out': w_out.astype(BF16),
        'norm_ffn': row(norm_ffn),
        'w_gate_up': w_gate_up.astype(BF16),
        'w_down': w_down.astype(BF16),
        'norm_ple': row(norm_ple),
        'w_ple_gate': w_ple_gate.astype(BF16), 'w_ple_up': w_ple_up.astype(BF16),
        'final_norm': final_norm.reshape(1, -1).astype(F32),
        'e2': _head_expansion(),
    }


def kernel(x_prompt, x_sample, state_conv, state_ssm, state_pool, p_prompt, p_sample, norm_mix, w_in, conv_w, conv_b, dt_bias, a_log, d_skip, ssd_norm, sgu_ln_g, sgu_ln_b, w_spatial, b_spatial, pool_w, pool_scale, w_br_a, w_br_b, w_br_c, w_out, norm_ffn, w_gate_up, w_down, norm_ple, w_ple_gate, w_ple_up, final_norm):
    b, l, _ = x_prompt.shape
    n, s, _ = x_sample.shape
    depth = w_in.shape[0]
    sw = _prepare_weights(norm_mix, w_in, conv_w, conv_b, dt_bias, a_log, d_skip, ssd_norm, sgu_ln_g, sgu_ln_b,
                          w_spatial, b_spatial, pool_w, pool_scale, w_br_a, w_br_b, w_br_c, w_out, norm_ffn,
                          w_gate_up, w_down, norm_ple, w_ple_gate, w_ple_up, final_norm)
    conv_pad = jnp.pad(state_conv, ((0, 0), (0, 0), (SUBLANES - (SSD_CONV - 1), 0), (0, 0)))
    pool_pad = jnp.pad(state_pool, ((0, 0), (0, 0), (2 * SUBLANES - POOL_BUF, 0), (0, 0)))
    ssm_flat = state_ssm.reshape(depth, n, SSD_INNER, SSD_STATE)
    pp = p_prompt.reshape(depth, b * l, PLE_DIM)
    ps = p_sample.reshape(depth, n * s, PLE_DIM)
    xp = x_prompt
    xs = x_sample.reshape(n * s, D_MODEL)
    st_p = None
    st_s = None
    for i in range(depth):
        final = i == depth - 1
        ya, yb, yc, st_p = _branch_prompt(xp, sw, i, st_p)
        x1 = _merge(xp.reshape(b * l, D_MODEL), ya.reshape(b * l, -1), yb.reshape(b * l, -1), yc.reshape(b * l, -1), sw, i)
        xp = _ffn(x1, pp, sw, i, final).reshape(b, l, D_MODEL)
        ya, yb, yc, st_s = _branch_sample(xs.reshape(n, s, D_MODEL), conv_pad, ssm_flat, pool_pad, sw, i, st_s, PAST_LEN)
        x1 = _merge(xs, ya, yb, yc, sw, i)
        xs = _ffn(x1, ps, sw, i, final)
    conv_p, ssm_p, pool_p, v_p = st_p
    conv_s, ssm_s, pool_s, v_s = st_s
    return (xp, xs.reshape(n, s, D_MODEL),
            conv_p[:, :, SUBLANES - (SSD_CONV - 1):, :],
            ssm_p.reshape(depth, b, SSD_HEADS, SSD_HEAD_DIM, SSD_STATE),
            pool_p[:, :, 2 * SUBLANES - POOL_BUF:, :],
            v_p,
            conv_s[:, :, s - (SSD_CONV - 1):, :],
            ssm_s.reshape(depth, n, SSD_HEADS, SSD_HEAD_DIM, SSD_STATE),
            pool_s[:, :, 2 * s - POOL_BUF:, :],
            v_s.reshape(depth, n, s, SGU_WIDTH))
```
